```python
import jax
import jax.numpy as jnp
from jax import lax
import numpy as np


D_MODEL = 2048
BATCH = 2
SEQ = 4096
DEPTH = 4

N_MIXERS = 4
HEAD_DIM = 128
N_HEADS = D_MODEL // HEAD_DIM
D_FF = 4 * D_MODEL
EPS = 1e-6
NEG = -1e30
BIG = 1e4
CONV_WIDTH = 3
NSA_KV_GROUPS = 4
NSA_HPG = N_HEADS // NSA_KV_GROUPS
CMP_LEN = 32
CMP_STRIDE = 16
SEL_LEN = 64
SEL_TOPK = 16
WINDOW = 512
Q_BLOCK = 128
SEL_Q_CHUNK = 64
POOL_WINDOWS = (2, 4, 8, 16)
POOL_GROUP = D_MODEL // len(POOL_WINDOWS)
NSA_IN_WIDTH = N_HEADS * HEAD_DIM + 6 * NSA_KV_GROUPS * HEAD_DIM + 3 * N_HEADS

kernel_name = 'hybrid_interleaved_conv_nsa_pool_stickbreak'


def rmsnorm(x, g):
    xf = x.astype(jnp.float32)
    y = xf * lax.rsqrt(jnp.mean(xf * xf, axis=-1, keepdims=True) + EPS)
    return (y * g.astype(jnp.float32)).astype(x.dtype)


def short_conv_mixer(h, w_in, conv_w, w_out):
    d = h.shape[-1]
    b_gate, c_gate, v = jnp.split(h @ w_in, 3, axis=-1)
    u = c_gate * v
    y = lax.conv_general_dilated(
        u, conv_w[:, None, :].astype(u.dtype), window_strides=(1,),
        padding=[(CONV_WIDTH - 1, 0)], dimension_numbers=('NWC', 'WIO', 'NWC'),
        feature_group_count=d)
    return (b_gate * y) @ w_out


def _compress(t, pos, w1, w2):
    bsz, seq, g, dh = t.shape
    n_c = (seq - CMP_LEN) // CMP_STRIDE + 1
    r = CMP_LEN // CMP_STRIDE
    chunks = t.reshape(bsz, seq // CMP_STRIDE, CMP_STRIDE, g, dh)
    blocks = jnp.concatenate([chunks[:, j:j + n_c] for j in range(r)], axis=2)
    blocks = blocks + pos[:, None, :].astype(t.dtype)
    flat = jnp.moveaxis(blocks, 3, 2).reshape(bsz, n_c, g, CMP_LEN * dh)
    return jax.nn.silu(flat @ w1) @ w2


def nsa_mixer(h, w_in, q_gain, k_gain, cmp_pos, cmp_w1, cmp_w2, w_out):
    bsz, seq, _ = h.shape
    H, G, hpg, dh = N_HEADS, NSA_KV_GROUPS, NSA_HPG, HEAD_DIM
    dt = h.dtype
    scale = dh ** -0.5
    offs = [int(o) for o in np.cumsum([H * dh] + [G * dh] * 6)]
    q, kc, vc, ks, vs, kw, vw, gates = jnp.split(h @ w_in, offs, axis=-1)
    q = rmsnorm(q.reshape(bsz, seq, G, hpg, dh), q_gain)
    pos_q = jnp.arange(seq)

    kcc = rmsnorm(_compress(kc.reshape(bsz, seq, G, dh), cmp_pos[0], cmp_w1[0], cmp_w2[0]), k_gain[0])
    vcc = _compress(vc.reshape(bsz, seq, G, dh), cmp_pos[1], cmp_w1[1], cmp_w2[1])
    n_c = kcc.shape[1]
    c_end = jnp.arange(n_c) * CMP_STRIDE + CMP_LEN - 1
    c_valid = (c_end[None, :] <= pos_q[:, None])[None, :, None, None, :]
    s_c = jnp.einsum('bqghd,bcgd->bqghc', q, kcc).astype(jnp.float32) * scale
    p_c = jnp.where(c_valid, jax.nn.softmax(jnp.where(c_valid, s_c, NEG), axis=-1), 0.0)
    o_cmp = jnp.einsum('bqghc,bcgd->bqghd', p_c.astype(dt), vcc)

    n_s = seq // SEL_LEN
    ci = np.arange(n_c)[:, None]
    sj = np.arange(n_s)[None, :]
    overlap = ((ci * CMP_STRIDE < (sj + 1) * SEL_LEN) &
               (ci * CMP_STRIDE + CMP_LEN > sj * SEL_LEN)).astype(np.float32)
    imp = jnp.einsum('bqgc,cs->bqgs', p_c.sum(axis=3), jnp.asarray(overlap))
    blk = jnp.arange(n_s)[None, :]
    cur = (pos_q // SEL_LEN)[:, None]
    forced = ((blk == 0) | (blk == cur) | (blk == cur - 1))[:, None, :]
    blk_valid = (blk * SEL_LEN <= pos_q[:, None])[:, None, :]
    imp = jnp.where(forced, BIG, jnp.where(blk_valid, imp, -BIG))
    k_top = min(SEL_TOPK, n_s)
    _, idx = lax.top_k(imp, k_top)

    ks_b = rmsnorm(ks.reshape(bsz, seq, G, dh), k_gain[1]).reshape(bsz, n_s, SEL_LEN, G, dh).transpose(0, 3, 1, 2, 4)
    vs_b = vs.reshape(bsz, n_s, SEL_LEN, G, dh).transpose(0, 3, 1, 2, 4)
    n_q = seq // SEL_Q_CHUNK
    q_ch = q.reshape(bsz, n_q, SEL_Q_CHUNK, G, hpg, dh).swapaxes(0, 1)
    idx_ch = idx.reshape(bsz, n_q, SEL_Q_CHUNK, G, k_top).swapaxes(0, 1)
    pos_ch = pos_q.reshape(n_q, SEL_Q_CHUNK)
    bi = jnp.arange(bsz)[:, None, None, None]
    gi = jnp.arange(G)[None, None, :, None]
    lane = jnp.arange(SEL_LEN)

    def sel_chunk(args):
        qc, ic, pc = args
        kg = ks_b[bi, gi, ic]
        vg = vs_b[bi, gi, ic]
        s = jnp.einsum('bcghd,bcgkld->bcghkl', qc, kg).astype(jnp.float32) * scale
        kpos = ic[..., None] * SEL_LEN + lane
        valid = (kpos <= pc[None, :, None, None, None])[:, :, :, None]
        s = jnp.where(valid, s, NEG)
        sh = s.shape
        p = jax.nn.softmax(s.reshape(sh[:4] + (sh[4] * sh[5],)), axis=-1).reshape(sh)
        return jnp.einsum('bcghkl,bcgkld->bcghd', p.astype(dt), vg)

    o_sel = lax.map(sel_chunk, (q_ch, idx_ch, pos_ch)).swapaxes(0, 1).reshape(bsz, seq, G, hpg, dh)

    nb = seq // Q_BLOCK
    nw = WINDOW // Q_BLOCK
    kwn = rmsnorm(kw.reshape(bsz, seq, G, dh), k_gain[2])
    kp = jnp.pad(kwn, ((0, 0), (WINDOW, 0), (0, 0), (0, 0))).reshape(bsz, nb + nw, Q_BLOCK, G, dh)
    vp = jnp.pad(vw.reshape(bsz, seq, G, dh), ((0, 0), (WINDOW, 0), (0, 0), (0, 0))).reshape(bsz, nb + nw, Q_BLOCK, G, dh)
    k_band = jnp.concatenate([kp[:, j:j + nb] for j in range(nw + 1)], axis=2)
    v_band = jnp.concatenate([vp[:, j:j + nb] for j in range(nw + 1)], axis=2)
    qpos = jnp.arange(nb)[:, None] * Q_BLOCK + jnp.arange(Q_BLOCK)[None, :]
    kpos = jnp.arange(nb)[:, None] * Q_BLOCK - WINDOW + jnp.arange((nw + 1) * Q_BLOCK)[None, :]
    w_valid = ((kpos[:, None, :] <= qpos[:, :, None]) & (qpos[:, :, None] - kpos[:, None, :] < WINDOW)
               & (kpos[:, None, :] >= 0))[None, :, :, None, None, :]
    qb = q.reshape(bsz, nb, Q_BLOCK, G, hpg, dh)
    s_w = jnp.einsum('bnqghd,bnkgd->bnqghk', qb, k_band).astype(jnp.float32) * scale
    p_w = jax.nn.softmax(jnp.where(w_valid, s_w, NEG), axis=-1)
    o_win = jnp.einsum('bnqghk,bnkgd->bnqghd', p_w.astype(dt), v_band).reshape(bsz, seq, G, hpg, dh)

    g = jax.nn.sigmoid(gates.astype(jnp.float32)).astype(dt).reshape(bsz, seq, G, hpg, 3)
    o = g[..., 0:1] * o_cmp + g[..., 1:2] * o_sel + g[..., 2:3] * o_win
    return o.reshape(bsz, seq, H * dh) @ w_out


def pool_mixer(h, w_grp, scale):
    bsz, seq, d = h.shape
    n_g = len(POOL_WINDOWS)
    hg = h.reshape(bsz, seq, n_g, POOL_GROUP).astype(jnp.float32)
    c0 = jnp.concatenate([jnp.zeros((bsz, 1, n_g, POOL_GROUP), jnp.float32), jnp.cumsum(hg, axis=1)], axis=1)
    win = jnp.array(POOL_WINDOWS)[None, :]
    pos = jnp.arange(seq)[:, None]
    lo = jnp.maximum(pos + 1 - win, 0)
    cnt = jnp.minimum(pos + 1, win).astype(jnp.float32)
    lower = c0[:, lo, jnp.arange(n_g)[None, :]]
    pooled = (c0[:, 1:] - lower) / cnt[None, :, :, None] - hg
    y = jnp.einsum('bsgc,gce->bsge', pooled.astype(h.dtype), w_grp).reshape(bsz, seq, d)
    return y * scale


def stick_breaking_mixer(h, w_in, q_gain, k_gain, w_out):
    bsz, seq, _ = h.shape
    H, dh = N_HEADS, HEAD_DIM
    q, k, v = jnp.split(h @ w_in, 3, axis=-1)
    q = rmsnorm(q.reshape(bsz, seq, H, dh), q_gain)
    k = rmsnorm(k.reshape(bsz, seq, H, dh), k_gain)
    v = v.reshape(bsz, seq, H, dh)
    nb = seq // Q_BLOCK
    qb = q.reshape(bsz, nb, Q_BLOCK, H, dh).swapaxes(0, 1)
    kpos = jnp.arange(seq)
    scale = dh ** -0.5

    def block(args):
        qc, i = args
        qpos = i * Q_BLOCK + jnp.arange(Q_BLOCK)
        z = jnp.einsum('bqhd,bkhd->bhqk', qc, k).astype(jnp.float32) * scale
        before = kpos[None, :] < qpos[:, None]
        log_1m = jnp.where(before, jax.nn.log_sigmoid(-z), 0.0)
        tail = lax.cumsum(log_1m, axis=3, reverse=True) - log_1m
        a = jnp.where(before, jnp.exp(jax.nn.log_sigmoid(z) + tail), 0.0)
        return jnp.einsum('bhqk,bkhd->bqhd', a.astype(v.dtype), v)

    o = lax.map(block, (qb, jnp.arange(nb))).swapaxes(0, 1).reshape(bsz, seq, H * dh)
    return o @ w_out


def squared_relu_mlp(h, w1, w2):
    a = jax.nn.relu(h @ w1)
    return (a * a) @ w2


def _n_layers_of(kind):
    return len(range(kind, DEPTH, N_MIXERS))


def setup_inputs(seed: int = 0) -> dict:
    key = jax.random.key(seed)
    ks = iter(jax.random.split(key, 32))
    f32 = jnp.float32
    D, dh, L = D_MODEL, HEAD_DIM, CMP_LEN
    nA, nB, nC, nD = (_n_layers_of(kd) for kd in range(N_MIXERS))

    def nrm(shape, fan_in, mult=1.0):
        return jax.random.normal(next(ks), shape, f32) * (mult * fan_in ** -0.5)

    def gain(shape):
        return 1.0 + 0.05 * jax.random.normal(next(ks), shape, f32)

    return {
        'x': jax.random.normal(next(ks), (BATCH, SEQ, D), f32),
        'mix_norm': gain((DEPTH, D)),
        'mlp_norm': gain((DEPTH, D)),
        'mlp_w1': nrm((DEPTH, D, D_FF), D),
        'mlp_w2': nrm((DEPTH, D_FF, D), D_FF),
        'conv_w_in': nrm((nA, D, 3 * D), D),
        'conv_w': nrm((nA, CONV_WIDTH, D), CONV_WIDTH),
        'conv_w_out': nrm((nA, D, D), D),
        'nsa_w_in': nrm((nB, D, NSA_IN_WIDTH), D),
        'nsa_q_gain': gain((nB, dh)),
        'nsa_k_gain': gain((nB, 3, dh)),
        'nsa_cmp_pos': 0.5 * jax.random.normal(next(ks), (nB, 2, L, dh), f32),
        'nsa_cmp_w1': nrm((nB, 2, L * dh, dh), L * dh),
        'nsa_cmp_w2': nrm((nB, 2, dh, dh), dh),
        'nsa_w_out': nrm((nB, N_HEADS * dh, D), N_HEADS * dh),
        'pool_w': nrm((nC, len(POOL_WINDOWS), POOL_GROUP, POOL_GROUP), POOL_GROUP),
        'pool_scale': gain((nC, D)),
        'sb_w_in': nrm((nD, D, 3 * N_HEADS * dh), D),
        'sb_q_gain': gain((nD, dh)),
        'sb_k_gain': gain((nD, dh)),
        'sb_w_out': nrm((nD, N_HEADS * dh, D), N_HEADS * dh),
    }


def reference(x, mix_norm, mlp_norm, mlp_w1, mlp_w2, conv_w_in, conv_w, conv_w_out,
              nsa_w_in, nsa_q_gain, nsa_k_gain, nsa_cmp_pos, nsa_cmp_w1, nsa_cmp_w2, nsa_w_out,
              pool_w, pool_scale, sb_w_in, sb_q_gain, sb_k_gain, sb_w_out):
    for i in range(DEPTH):
        kind, j = i % N_MIXERS, i // N_MIXERS
        h = rmsnorm(x, mix_norm[i])
        if kind == 0:
            y = short_conv_mixer(h, conv_w_in[j], conv_w[j], conv_w_out[j])
        elif kind == 1:
            y = nsa_mixer(h, nsa_w_in[j], nsa_q_gain[j], nsa_k_gain[j], nsa_cmp_pos[j],
                          nsa_cmp_w1[j], nsa_cmp_w2[j], nsa_w_out[j])
        elif kind == 2:
            y = pool_mixer(h, pool_w[j], pool_scale[j])
        else:
            y = stick_breaking_mixer(h, sb_w_in[j], sb_q_gain[j], sb_k_gain[j], sb_w_out[j])
        x = x + y
        x = x + squared_relu_mlp(rmsnorm(x, mlp_norm[i]), mlp_w1[i], mlp_w2[i])
    return x
```

```python
import functools

import jax
import jax.numpy as jnp
from jax import lax
from jax.experimental import pallas as pl
from jax.experimental.pallas import tpu as pltpu

F32 = jnp.float32
BF16 = jnp.bfloat16

HEAD_DIM = 128
EPS = 1e-6
NEG = -1e30
BIG = 1e4
CONV_WIDTH = 3
NSA_KV_GROUPS = 4
CMP_LEN = 32
CMP_STRIDE = 16
SEL_LEN = 64
SEL_SHIFT = SEL_LEN.bit_length() - 1
assert 1 << SEL_SHIFT == SEL_LEN
SEL_TOPK = 16
WINDOW = 512
POOL_WINDOWS = (2, 4, 8, 16)
LANES = 128
VMEM_LIMIT = 56 * 1024 * 1024


def _params(*sem):
    return pltpu.CompilerParams(dimension_semantics=sem, vmem_limit_bytes=VMEM_LIMIT)


def _rms(xv, g):
    ms = jnp.mean(xv * xv, axis=-1, keepdims=True)
    return xv * lax.rsqrt(ms + EPS) * g


def _dot(a, b):
    return jnp.dot(a, b, preferred_element_type=F32)


def _dot_nt(a, b):
    return lax.dot_general(a, b, (((1,), (1,)), ((), ())), preferred_element_type=F32)


def _split_dot(a, b):
    hi = a.astype(BF16)
    r1 = a - hi.astype(F32)
    mid = r1.astype(BF16)
    lo = (r1 - mid.astype(F32)).astype(BF16)
    return _dot(hi, b) + _dot(mid, b) + _dot(lo, b)


def _norm_proj_body(modes, tn, has_gate, x_ref, g_ref, w_ref, gain_ref, *rest):
    if has_gate:
        o_ref, og_ref, h_scr = rest
    else:
        o_ref, h_scr = rest
    j = pl.program_id(1)

    @pl.when(j == 0)
    def _():
        h_scr[...] = _rms(x_ref[...], g_ref[...]).astype(BF16)

    acc = _dot(h_scr[...], w_ref[...])

    def tiles_of(mode):
        return [t for t, m in enumerate(modes) if m == mode]

    def any_of(tiles):
        c = j == tiles[0]
        for t in tiles[1:]:
            c = c | (j == t)
        return c

    if tiles_of("plain"):
        @pl.when(any_of(tiles_of("plain")))
        def _():
            o_ref[...] = acc.astype(BF16)

    if tiles_of("norm"):
        @pl.when(any_of(tiles_of("norm")))
        def _():
            gain = gain_ref[0]
            for h in range(tn // HEAD_DIM):
                sl = slice(h * HEAD_DIM, (h + 1) * HEAD_DIM)
                o_ref[:, sl] = _rms(acc[:, sl], gain[:, sl]).astype(BF16)

    if tiles_of("gate"):
        @pl.when(any_of(tiles_of("gate")))
        def _():
            o_ref[...] = acc.astype(BF16)
            og_ref[...] = jax.nn.sigmoid(acc[:, :LANES])


def _norm_proj(x, g, w, gains, modes, tm, tn):
    T, D = x.shape
    N = w.shape[1]
    assert N == len(modes) * tn and T % tm == 0
    has_gate = "gate" in modes
    out_shape = [jax.ShapeDtypeStruct((T, N), BF16)]
    out_specs = [pl.BlockSpec((tm, tn), lambda i, j: (i, j))]
    if has_gate:
        out_shape.append(jax.ShapeDtypeStruct((T, LANES), F32))
        out_specs.append(pl.BlockSpec((tm, LANES), lambda i, j: (i, 0)))
    res = pl.pallas_call(
        functools.partial(_norm_proj_body, tuple(modes), tn, has_gate),
        grid=(T // tm, N // tn),
        in_specs=[
            pl.BlockSpec((tm, D), lambda i, j: (i, 0)),
            pl.BlockSpec((1, D), lambda i, j: (0, 0)),
            pl.BlockSpec((D, tn), lambda i, j: (0, j)),
            pl.BlockSpec((1, 1, tn), lambda i, j: (j, 0, 0)),
        ],
        out_specs=out_specs,
        out_shape=out_shape,
        scratch_shapes=[pltpu.VMEM((tm, D), BF16)],
        compiler_params=_params("parallel", "arbitrary"),
        name="norm_proj",
    )(x, g.reshape(1, D), w, gains)
    return res if has_gate else res[0]


def _res_proj_body(prologue, n_rows, *refs):
    row_refs = refs[:n_rows]
    w_ref, x_ref, o_ref = refs[n_rows:n_rows + 3]
    a = prologue(*row_refs)
    o_ref[...] = x_ref[...] + _dot(a, w_ref[...])


def _res_proj(rows, row_specs, prologue, w, x, tm, name):
    T, D = x.shape
    K = w.shape[0]
    return pl.pallas_call(
        functools.partial(_res_proj_body, prologue, len(rows)),
        grid=(T // tm,),
        in_specs=list(row_specs) + [
            pl.BlockSpec((K, D), lambda i: (0, 0)),
            pl.BlockSpec((tm, D), lambda i: (i, 0)),
        ],
        out_specs=pl.BlockSpec((tm, D), lambda i: (i, 0)),
        out_shape=jax.ShapeDtypeStruct((T, D), F32),
        compiler_params=_params("parallel"),
        name=name,
    )(*rows, w, x)


def _mlp_body(x_ref, g_ref, w1_ref, w2_ref, o_ref, h_scr):
    f = pl.program_id(1)

    @pl.when(f == 0)
    def _():
        xv = x_ref[...]
        h_scr[...] = _rms(xv, g_ref[...]).astype(BF16)
        o_ref[...] = xv

    a = jnp.maximum(_dot(h_scr[...], w1_ref[...]), 0.0)
    o_ref[...] += _dot((a * a).astype(BF16), w2_ref[...])


def _mlp(x, g, w1, w2, tm, tf):
    T, D = x.shape
    FF = w1.shape[1]
    return pl.pallas_call(
        _mlp_body,
        grid=(T // tm, FF // tf),
        in_specs=[
            pl.BlockSpec((tm, D), lambda i, f: (i, 0)),
            pl.BlockSpec((1, D), lambda i, f: (0, 0)),
            pl.BlockSpec((D, tf), lambda i, f: (0, f)),
            pl.BlockSpec((tf, D), lambda i, f: (f, 0)),
        ],
        out_specs=pl.BlockSpec((tm, D), lambda i, f: (i, 0)),
        out_shape=jax.ShapeDtypeStruct((T, D), F32),
        scratch_shapes=[pltpu.VMEM((tm, D), BF16)],
        compiler_params=_params("parallel", "arbitrary"),
        name="mlp",
    )(x, g.reshape(1, D), w1, w2)


HALO = 16


def _conv_prologue(tm, tiles_per_seq, b_ref, c_ref, v_ref, cp_ref, vp_ref, cw_ref):
    i = pl.program_id(0)
    first = (i % tiles_per_seq) == 0
    u = c_ref[...].astype(F32) * v_ref[...].astype(F32)
    up = cp_ref[...].astype(F32) * vp_ref[...].astype(F32)
    up = jnp.where(first, 0.0, up)
    row = lax.broadcasted_iota(jnp.int32, u.shape, 0)
    r1 = jnp.where(row == 0, up[HALO - 1:HALO, :], pltpu.roll(u, 1, 0))
    r2 = pltpu.roll(u, 2, 0)
    r2 = jnp.where(row == 0, up[HALO - 2:HALO - 1, :], jnp.where(row == 1, up[HALO - 1:HALO, :], r2))
    cw = cw_ref[...]
    y = cw[0:1, :] * r2 + cw[1:2, :] * r1 + cw[2:3, :] * u
    return (b_ref[...].astype(F32) * y).astype(BF16)


def _conv_out(bcv, conv_w, w_out, x, seq, tm):
    T, D = x.shape
    hb = tm // HALO
    prev = lambda col: (lambda i: (jnp.maximum(i * hb - 1, 0), col))
    specs = [
        pl.BlockSpec((tm, D), lambda i: (i, 0)),
        pl.BlockSpec((tm, D), lambda i: (i, 1)),
        pl.BlockSpec((tm, D), lambda i: (i, 2)),
        pl.BlockSpec((HALO, D), prev(1)),
        pl.BlockSpec((HALO, D), prev(2)),
        pl.BlockSpec((CONV_WIDTH, D), lambda i: (0, 0)),
    ]
    prologue = functools.partial(_conv_prologue, tm, seq // tm)
    return _res_proj([bcv, bcv, bcv, bcv, bcv, conv_w], specs, prologue, w_out, x, tm, "conv_out")


def _pool_body(tm, tiles_per_seq, x_ref, xp_ref, g_ref, w_ref, sc_ref, o_ref):
    i = pl.program_id(0)
    first = (i % tiles_per_seq) == 0
    xv = x_ref[...]
    g = g_ref[...]
    h = _rms(xv, g)
    hp = jnp.where(first, 0.0, _rms(xp_ref[...], g))
    pos = (i % tiles_per_seq) * tm + lax.broadcasted_iota(jnp.int32, (tm, 1), 0)
    cg = h.shape[1] // len(POOL_WINDOWS)
    for gi, win in enumerate(POOL_WINDOWS):
        sl = slice(gi * cg, (gi + 1) * cg)
        hg = h[:, sl]
        s = jnp.concatenate([hp[:, sl], hg], axis=0)
        k = 1
        while k < win:
            s = s + pltpu.roll(s, k, 0)
            k *= 2
        cnt = jnp.minimum(pos + 1, win).astype(F32)
        pooled = s[HALO:, :] / cnt - hg
        y = _dot(pooled.astype(BF16), w_ref[gi])
        o_ref[:, sl] = xv[:, sl] + y * sc_ref[:, sl]


def _pool_mixer(x, g, w, scale, seq, tm):
    T, D = x.shape
    ng, cg, _ = w.shape
    hb = tm // HALO
    return pl.pallas_call(
        functools.partial(_pool_body, tm, seq // tm),
        grid=(T // tm,),
        in_specs=[
            pl.BlockSpec((tm, D), lambda i: (i, 0)),
            pl.BlockSpec((HALO, D), lambda i: (jnp.maximum(i * hb - 1, 0), 0)),
            pl.BlockSpec((1, D), lambda i: (0, 0)),
            pl.BlockSpec((ng, cg, cg), lambda i: (0, 0, 0)),
            pl.BlockSpec((1, D), lambda i: (0, 0)),
        ],
        out_specs=pl.BlockSpec((tm, D), lambda i: (i, 0)),
        out_shape=jax.ShapeDtypeStruct((T, D), F32),
        compiler_params=_params("parallel"),
        name="pool_mixer",
    )(x, x, g.reshape(1, D), w, scale.reshape(1, D))


def _sb_body(tq, q_ref, k_ref, v_ref, o_ref):
    i = pl.program_id(2)
    q = q_ref[...]
    row = lax.broadcasted_iota(jnp.int32, (tq, tq), 0)
    col = lax.broadcasted_iota(jnp.int32, (tq, tq), 1)
    later = (row > col).astype(BF16)

    def step(n, carry):
        c, acc = carry
        j = i - n
        off = pl.multiple_of(j * tq, tq)
        k = k_ref[pl.ds(off, tq), :]
        v = v_ref[pl.ds(off, tq), :]
        z = _dot_nt(q, k)
        before = (j * tq + col) < (i * tq + row)
        log_1m = -(jnp.maximum(z, 0.0) + jnp.log1p(jnp.exp(-jnp.abs(z))))
        log_1m = jnp.where(before, log_1m, 0.0)
        hi = log_1m.astype(BF16)
        mid = (log_1m - hi.astype(F32)).astype(BF16)
        tail = _dot(hi, later) + _dot(mid, later)
        a = jnp.where(before, jnp.exp(z + log_1m + tail + c), 0.0)
        acc = acc + _dot(a.astype(BF16), v)
        c = c + tail[:, 0:1] + log_1m[:, 0:1]
        return c, acc

    init = (jnp.zeros((tq, 1), F32), jnp.zeros((tq, HEAD_DIM), F32))
    _, acc = lax.fori_loop(0, i + 1, step, init)
    o_ref[...] = acc.astype(BF16)


def _sb_attention(qkv, bsz, seq, n_heads, tq):
    T = bsz * seq
    nq = seq // tq
    return pl.pallas_call(
        functools.partial(_sb_body, tq),
        grid=(bsz, n_heads, nq),
        in_specs=[
            pl.BlockSpec((tq, HEAD_DIM), lambda b, h, i: (b * nq + i, h)),
            pl.BlockSpec((seq, HEAD_DIM), lambda b, h, i: (b, n_heads + h)),
            pl.BlockSpec((seq, HEAD_DIM), lambda b, h, i: (b, 2 * n_heads + h)),
        ],
        out_specs=pl.BlockSpec((tq, HEAD_DIM), lambda b, h, i: (b * nq + i, h)),
        out_shape=jax.ShapeDtypeStruct((T, n_heads * HEAD_DIM), BF16),
        compiler_params=_params("parallel", "parallel", "arbitrary"),
        name="sb_attention",
    )(qkv, qkv, qkv)


def _compress_body(n_chunk, a_ref, w1_ref, w2_ref, pos_ref, gain_ref, o_ref):
    kv = pl.program_id(0)
    a = a_ref[0, 0, 0]
    half = a.shape[1]
    w1 = w1_ref[0]
    const = _dot(pos_ref[0], w1)[0:1, :]
    first = _dot(a, w1[:half, :])
    second = _dot(a, w1[half:, :])
    pre = first + pltpu.roll(second, n_chunk - 1, 0) + const
    hid = pre * jax.nn.sigmoid(pre)
    out = _dot(hid.astype(BF16), w2_ref[0])
    normed = _rms(out, gain_ref[...])
    o_ref[0, 0, 0] = jnp.where(kv == 0, normed, out).astype(BF16)


def _compress(a, w1, w2, pos, gain):
    _, bsz, ng, n_chunk, width = a.shape
    dh = HEAD_DIM
    return pl.pallas_call(
        functools.partial(_compress_body, n_chunk),
        grid=(2, bsz, ng),
        in_specs=[
            pl.BlockSpec((1, 1, 1, n_chunk, width), lambda s, b, g: (s, b, g, 0, 0)),
            pl.BlockSpec((1, 2 * width, dh), lambda s, b, g: (s, 0, 0)),
            pl.BlockSpec((1, dh, dh), lambda s, b, g: (s, 0, 0)),
            pl.BlockSpec((1, 8, 2 * width), lambda s, b, g: (s, 0, 0)),
            pl.BlockSpec((1, dh), lambda s, b, g: (0, 0)),
        ],
        out_specs=pl.BlockSpec((1, 1, 1, n_chunk, dh), lambda s, b, g: (s, b, g, 0, 0)),
        out_shape=jax.ShapeDtypeStruct((2, bsz, ng, n_chunk, dh), BF16),
        compiler_params=_params("parallel", "parallel", "parallel"),
        name="nsa_compress",
    )(a, w1, w2, pos, gain)


def _stack_heads(q_ref, hpg):
    return jnp.concatenate([q_ref[:, h * HEAD_DIM:(h + 1) * HEAD_DIM] for h in range(hpg)], axis=0)


def _gate_column(gates, col):
    lane = lax.broadcasted_iota(jnp.int32, gates.shape, 1)
    return jnp.sum(jnp.where(lane == col, gates, 0.0), axis=1, keepdims=True)


def _nsa_cmp_body(tq, hpg, n_sel, q_ref, kc_ref, vc_ref, gates_ref, o_ref, sel_ref, vt_scr, st_scr):
    g = pl.program_id(1)
    i = pl.program_id(2)
    qs = _stack_heads(q_ref, hpg)
    kc = kc_ref[0, 0, 0]
    vc = vc_ref[0, 0, 0]
    n_chunk = kc.shape[0]
    s = _dot_nt(qs, kc)
    t1 = i * tq + lax.broadcasted_iota(jnp.int32, (tq, n_chunk), 0)
    c1 = lax.broadcasted_iota(jnp.int32, (tq, n_chunk), 1)
    valid1 = c1 * CMP_STRIDE + (CMP_LEN - 1) <= t1
    valid = jnp.concatenate([valid1] * hpg, axis=0)
    s = jnp.where(valid, s, NEG)
    m = jnp.max(s, axis=1, keepdims=True)
    e = jnp.where(valid, jnp.exp(s - m), 0.0)
    l = jnp.sum(e, axis=1, keepdims=True)
    p = e / jnp.where(l > 0.0, l, 1.0)
    o = _dot(p.astype(BF16), vc)
    gates = gates_ref[...]
    for h in range(hpg):
        gc = _gate_column(gates, 3 * (g * hpg + h) + 0)
        o_ref[:, h * HEAD_DIM:(h + 1) * HEAD_DIM] = (gc * o[h * tq:(h + 1) * tq, :]).astype(BF16)

    p_sum = p[0:tq, :]
    for h in range(1, hpg):
        p_sum = p_sum + p[h * tq:(h + 1) * tq, :]
    ci = lax.broadcasted_iota(jnp.int32, (n_chunk, LANES), 0)
    sj = lax.broadcasted_iota(jnp.int32, (n_chunk, LANES), 1)
    overlap = ((ci * CMP_STRIDE < (sj + 1) * SEL_LEN) & (ci * CMP_STRIDE + CMP_LEN > sj * SEL_LEN)
               & (ci < n_chunk - 1) & (sj < n_sel)).astype(BF16)
    imp = _split_dot(p_sum, overlap)
    t2 = i * tq + lax.broadcasted_iota(jnp.int32, (tq, LANES), 0)
    blk = lax.broadcasted_iota(jnp.int32, (tq, LANES), 1)
    cur = lax.shift_right_logical(t2, SEL_SHIFT)
    forced = (blk == 0) | (blk == cur) | (blk == cur - 1)
    blk_valid = blk * SEL_LEN <= t2
    score = jnp.where(forced, BIG, jnp.where(blk_valid, imp, -BIG))

    vt_scr[...] = score.T
    vt = vt_scr[0:n_sel, :]
    jrow = lax.broadcasted_iota(jnp.int32, (n_sel, tq), 0)
    cnt = jnp.zeros((n_sel, tq), F32)
    for b in range(n_sel):
        vb = vt_scr[b:b + 1, :]
        ahead = (vb > vt) | ((vb == vt) & (jrow > b))
        cnt = cnt + ahead.astype(F32)
    st_scr[...] = jnp.zeros_like(st_scr)
    st_scr[0:n_sel, :] = (cnt < float(min(SEL_TOPK, n_sel))).astype(F32)
    sel_ref[0, 0] = st_scr[...].T.astype(BF16)


def _nsa_compressed(qkv, kvc, gates, bsz, seq, ng, hpg, tq):
    T = bsz * seq
    nq = seq // tq
    n_chunk = kvc.shape[3]
    n_sel = seq // SEL_LEN
    assert n_sel <= LANES
    gw = hpg * HEAD_DIM
    return pl.pallas_call(
        functools.partial(_nsa_cmp_body, tq, hpg, n_sel),
        grid=(bsz, ng, nq),
        in_specs=[
            pl.BlockSpec((tq, gw), lambda b, g, i: (b * nq + i, g)),
            pl.BlockSpec((1, 1, 1, n_chunk, HEAD_DIM), lambda b, g, i: (0, b, g, 0, 0)),
            pl.BlockSpec((1, 1, 1, n_chunk, HEAD_DIM), lambda b, g, i: (1, b, g, 0, 0)),
            pl.BlockSpec((tq, LANES), lambda b, g, i: (b * nq + i, 0)),
        ],
        out_specs=[
            pl.BlockSpec((tq, gw), lambda b, g, i: (b * nq + i, g)),
            pl.BlockSpec((1, 1, tq, LANES), lambda b, g, i: (b, g, i, 0)),
        ],
        out_shape=[
            jax.ShapeDtypeStruct((T, ng * gw), BF16),
            jax.ShapeDtypeStruct((bsz, ng, seq, LANES), BF16),
        ],
        scratch_shapes=[pltpu.VMEM((LANES, tq), F32), pltpu.VMEM((LANES, tq), F32)],
        compiler_params=_params("parallel", "parallel", "parallel"),
        name="nsa_compressed",
    )(qkv, kvc, kvc, gates)


def _nsa_flash_body(mode, tq, tk, hpg, *refs):
    if mode == "sel":
        q_ref, k_ref, v_ref, gates_ref, sel_ref, o_ref, m_scr, l_scr, acc_scr = refs
    else:
        q_ref, k_ref, v_ref, gates_ref, o_ref, m_scr, l_scr, acc_scr = refs
    g = pl.program_id(1)
    i = pl.program_id(2)
    qs = _stack_heads(q_ref, hpg)
    m_scr[...] = jnp.full_like(m_scr, NEG)
    l_scr[...] = jnp.zeros_like(l_scr)
    acc_scr[...] = jnp.zeros_like(acc_scr)
    qpos = i * tq + lax.broadcasted_iota(jnp.int32, (tq, tk), 0)
    kcol = lax.broadcasted_iota(jnp.int32, (tq, tk), 1)
    if mode == "sel":
        sel = sel_ref[0, 0]
        eb = lax.broadcasted_iota(jnp.int32, (LANES, tk), 0)
        ek = lax.broadcasted_iota(jnp.int32, (LANES, tk), 1)

    def step(j, carry):
        off = pl.multiple_of(j * tk, tk)
        k = k_ref[pl.ds(off, tk), :]
        v = v_ref[pl.ds(off, tk), :]
        s = _dot_nt(qs, k)
        kpos = j * tk + kcol
        if mode == "sel":
            expand = (eb == lax.shift_right_logical(j * tk + ek, SEL_SHIFT)).astype(BF16)
            mask1 = (_dot(sel, expand) > 0.5) & (kpos <= qpos)
        else:
            mask1 = (kpos <= qpos) & (qpos - kpos < WINDOW)
        mask = jnp.concatenate([mask1] * hpg, axis=0)
        s = jnp.where(mask, s, NEG)
        m_old = m_scr[...]
        m_new = jnp.maximum(m_old, jnp.max(s, axis=1, keepdims=True))
        p = jnp.where(mask, jnp.exp(s - m_new), 0.0)
        alpha = jnp.exp(m_old - m_new)
        l_scr[...] = alpha * l_scr[...] + jnp.sum(p, axis=1, keepdims=True)
        acc_scr[...] = alpha * acc_scr[...] + _dot(p.astype(BF16), v)
        m_scr[...] = m_new
        return carry

    hi = ((i + 1) * tq + tk - 1) // tk
    if mode == "sel":
        lo = 0
    else:
        lo = jnp.maximum(i * tq - (WINDOW - 1), 0) // tk
    lax.fori_loop(lo, hi, step, 0)

    gates = gates_ref[...]
    branch = 1 if mode == "sel" else 2
    out = acc_scr[...] / l_scr[...]
    for h in range(hpg):
        gc = _gate_column(gates, 3 * (g * hpg + h) + branch)
        o_ref[:, h * HEAD_DIM:(h + 1) * HEAD_DIM] = (gc * out[h * tq:(h + 1) * tq, :]).astype(BF16)


def _nsa_flash(mode, qkv, gates, sel, k_col, v_col, bsz, seq, ng, hpg, tq, tk):
    T = bsz * seq
    nq = seq // tq
    gw = hpg * HEAD_DIM
    in_specs = [
        pl.BlockSpec((tq, gw), lambda b, g, i: (b * nq + i, g)),
        pl.BlockSpec((seq, HEAD_DIM), lambda b, g, i: (b, k_col + g)),
        pl.BlockSpec((seq, HEAD_DIM), lambda b, g, i: (b, v_col + g)),
        pl.BlockSpec((tq, LANES), lambda b, g, i: (b * nq + i, 0)),
    ]
    args = [qkv, qkv, qkv, gates]
    if mode == "sel":
        in_specs.append(pl.BlockSpec((1, 1, tq, LANES), lambda b, g, i: (b, g, i, 0)))
        args.append(sel)
    rows = hpg * tq
    return pl.pallas_call(
        functools.partial(_nsa_flash_body, mode, tq, tk, hpg),
        grid=(bsz, ng, nq),
        in_specs=in_specs,
        out_specs=pl.BlockSpec((tq, gw), lambda b, g, i: (b * nq + i, g)),
        out_shape=jax.ShapeDtypeStruct((T, ng * gw), BF16),
        scratch_shapes=[pltpu.VMEM((rows, 1), F32), pltpu.VMEM((rows, 1), F32), pltpu.VMEM((rows, HEAD_DIM), F32)],
        compiler_params=_params("parallel", "parallel", "arbitrary"),
        name="nsa_" + mode,
    )(*args)


def _sum3_prologue(a_ref, b_ref, c_ref):
    return (a_ref[...].astype(F32) + b_ref[...].astype(F32) + c_ref[...].astype(F32)).astype(BF16)


def _ident_prologue(a_ref):
    return a_ref[...]


TM_PROJ = 512
TN_PROJ = 512
TM_OUT = 256
TM_MLP = 512
TF_MLP = 1024
TM_POOL = 256
TQ_SB = 256
TQ_NSA = 128
TK_SEL = 256
TK_WIN = 128


def _tile_gains(per_tile, tn):
    rows = [jnp.tile(g.astype(F32), tn // HEAD_DIM) if g is not None else jnp.ones((tn,), F32) for g in per_tile]
    return jnp.stack(rows).reshape(len(per_tile), 1, tn)


def _conv_layer(x, seq, norm_g, w_in, conv_w, w_out):
    n_tiles = w_in.shape[1] // TN_PROJ
    bcv = _norm_proj(x, norm_g, w_in.astype(BF16), _tile_gains([None] * n_tiles, TN_PROJ),
                     ["plain"] * n_tiles, TM_PROJ, TN_PROJ)
    return _conv_out(bcv, conv_w, w_out.astype(BF16), x, seq, TM_OUT)


def _nsa_layer(x, bsz, seq, norm_g, w_in, q_gain, k_gain, cmp_pos, cmp_w1, cmp_w2, w_out):
    T, D = x.shape
    dh, G = HEAD_DIM, NSA_KV_GROUPS
    H = D // dh
    hpg = H // G
    gw = hpg * dh
    assert gw == TN_PROJ and G * dh == TN_PROJ
    scale = dh ** -0.5
    width = w_in.shape[1]
    n_tiles = -(-width // TN_PROJ)
    w_pad = jnp.pad(w_in, ((0, 0), (0, n_tiles * TN_PROJ - width))).astype(BF16)
    modes = ["norm"] * G + ["plain", "plain", "norm", "plain", "norm", "plain", "gate"]
    gains = _tile_gains([q_gain * scale] * G + [None, None, k_gain[1], None, k_gain[2], None, None], TN_PROJ)
    qkv, gates = _norm_proj(x, norm_g, w_pad, gains, modes, TM_PROJ, TN_PROJ)

    n_chunk = seq // CMP_STRIDE
    kvc_in = qkv[:, H * dh:H * dh + 2 * G * dh].reshape(bsz, n_chunk, CMP_STRIDE, 2, G, dh)
    kvc_in = kvc_in.transpose(3, 0, 4, 1, 2, 5).reshape(2, bsz, G, n_chunk, CMP_STRIDE * dh)
    pos = jnp.broadcast_to(cmp_pos.reshape(2, 1, CMP_LEN * dh), (2, 8, CMP_LEN * dh)).astype(BF16)
    kvc = _compress(kvc_in, cmp_w1.astype(BF16), cmp_w2.astype(BF16), pos, k_gain[0].reshape(1, dh))

    col = lambda t: t * (TN_PROJ // dh)
    o_cmp, sel = _nsa_compressed(qkv, kvc, gates, bsz, seq, G, hpg, TQ_NSA)
    o_sel = _nsa_flash("sel", qkv, gates, sel, col(G + 2), col(G + 3), bsz, seq, G, hpg, TQ_NSA, TK_SEL)
    o_win = _nsa_flash("win", qkv, gates, None, col(G + 4), col(G + 5), bsz, seq, G, hpg, TQ_NSA, TK_WIN)
    spec = pl.BlockSpec((TM_OUT, H * dh), lambda i: (i, 0))
    return _res_proj([o_cmp, o_sel, o_win], [spec] * 3, _sum3_prologue, w_out.astype(BF16), x, TM_OUT, "nsa_out")


def _sb_layer(x, bsz, seq, norm_g, w_in, q_gain, k_gain, w_out):
    T, D = x.shape
    dh = HEAD_DIM
    H = D // dh
    scale = dh ** -0.5
    per = (H * dh) // TN_PROJ
    modes = ["norm"] * (2 * per) + ["plain"] * per
    gains = _tile_gains([q_gain * scale] * per + [k_gain] * per + [None] * per, TN_PROJ)
    qkv = _norm_proj(x, norm_g, w_in.astype(BF16), gains, modes, TM_PROJ, TN_PROJ)
    o = _sb_attention(qkv, bsz, seq, H, TQ_SB)
    spec = pl.BlockSpec((TM_OUT, H * dh), lambda i: (i, 0))
    return _res_proj([o], [spec], _ident_prologue, w_out.astype(BF16), x, TM_OUT, "sb_out")


def kernel(x, mix_norm, mlp_norm, mlp_w1, mlp_w2, conv_w_in, conv_w, conv_w_out, nsa_w_in, nsa_q_gain, nsa_k_gain, nsa_cmp_pos, nsa_cmp_w1, nsa_cmp_w2, nsa_w_out, pool_w, pool_scale, sb_w_in, sb_q_gain, sb_k_gain, sb_w_out):
    bsz, seq, d = x.shape
    depth = mix_norm.shape[0]
    n_mixers = 4
    xf = x.reshape(bsz * seq, d)
    for i in range(depth):
        kind, j = i % n_mixers, i // n_mixers
        if kind == 0:
            xf = _conv_layer(xf, seq, mix_norm[i], conv_w_in[j], conv_w[j], conv_w_out[j])
        elif kind == 1:
            xf = _nsa_layer(xf, bsz, seq, mix_norm[i], nsa_w_in[j], nsa_q_gain[j], nsa_k_gain[j],
                            nsa_cmp_pos[j], nsa_cmp_w1[j], nsa_cmp_w2[j], nsa_w_out[j])
        elif kind == 2:
            xf = _pool_mixer(xf, mix_norm[i], pool_w[j].astype(BF16), pool_scale[j], seq, TM_POOL)
        else:
            xf = _sb_layer(xf, bsz, seq, mix_norm[i], sb_w_in[j], sb_q_gain[j], sb_k_gain[j], sb_w_out[j])
        xf = _mlp(xf, mlp_norm[i], mlp_w1[i].astype(BF16), mlp_w2[i].astype(BF16), TM_MLP, TF_MLP)
    return xf.reshape(bsz, seq, d)
```

```python
import functools

import jax
import jax.numpy as jnp
from jax import lax
from jax.experimental import pallas as pl
from jax.experimental.pallas import tpu as pltpu

F32 = jnp.float32
BF16 = jnp.bfloat16

HEAD_DIM = 128
EPS = 1e-6
NEG = -1e30
BIG = 1e4
CONV_WIDTH = 3
NSA_KV_GROUPS = 4
CMP_LEN = 32
CMP_STRIDE = 16
SEL_LEN = 64
SEL_SHIFT = SEL_LEN.bit_length() - 1
assert 1 << SEL_SHIFT == SEL_LEN
SEL_TOPK = 16
WINDOW = 512
POOL_WINDOWS = (2, 4, 8, 16)
LANES = 128
VMEM_LIMIT = 56 * 1024 * 1024


def _params(*sem):
    return pltpu.CompilerParams(dimension_semantics=sem, vmem_limit_bytes=VMEM_LIMIT)


def _rms(xv, g):
    ms = jnp.mean(xv * xv, axis=-1, keepdims=True)
    return xv * lax.rsqrt(ms + EPS) * g


def _dot(a, b):
    return jnp.dot(a, b, preferred_element_type=F32)


def _dot_nt(a, b):
    return lax.dot_general(a, b, (((1,), (1,)), ((), ())), preferred_element_type=F32)


def _split_dot(a, b):
    hi = a.astype(BF16)
    r1 = a - hi.astype(F32)
    mid = r1.astype(BF16)
    lo = (r1 - mid.astype(F32)).astype(BF16)
    return _dot(hi, b) + _dot(mid, b) + _dot(lo, b)


def _norm_proj_body(modes, tn, has_gate, x_ref, g_ref, w_ref, gain_ref, *rest):
    if has_gate:
        o_ref, og_ref, h_scr = rest
    else:
        o_ref, h_scr = rest
    j = pl.program_id(1)

    @pl.when(j == 0)
    def _():
        h_scr[...] = _rms(x_ref[...], g_ref[...]).astype(BF16)

    acc = _dot(h_scr[...], w_ref[...])

    def tiles_of(mode):
        return [t for t, m in enumerate(modes) if m == mode]

    def any_of(tiles):
        c = j == tiles[0]
        for t in tiles[1:]:
            c = c | (j == t)
        return c

    if tiles_of("plain"):
        @pl.when(any_of(tiles_of("plain")))
        def _():
            o_ref[...] = acc.astype(BF16)

    if tiles_of("norm"):
        @pl.when(any_of(tiles_of("norm")))
        def _():
            gain = gain_ref[0]
            for h in range(tn // HEAD_DIM):
                sl = slice(h * HEAD_DIM, (h + 1) * HEAD_DIM)
                o_ref[:, sl] = _rms(acc[:, sl], gain[:, sl]).astype(BF16)

    if tiles_of("gate"):
        @pl.when(any_of(tiles_of("gate")))
        def _():
            o_ref[...] = acc.astype(BF16)
            og_ref[...] = jax.nn.sigmoid(acc[:, :LANES])


def _norm_proj(x, g, w, gains, modes, tm, tn):
    T, D = x.shape
    N = w.shape[1]
    assert N == len(modes) * tn and T % tm == 0
    has_gate = "gate" in modes
    out_shape = [jax.ShapeDtypeStruct((T, N), BF16)]
    out_specs = [pl.BlockSpec((tm, tn), lambda i, j: (i, j))]
    if has_gate:
        out_shape.append(jax.ShapeDtypeStruct((T, LANES), F32))
        out_specs.append(pl.BlockSpec((tm, LANES), lambda i, j: (i, 0)))
    res = pl.pallas_call(
        functools.partial(_norm_proj_body, tuple(modes), tn, has_gate),
        grid=(T // tm, N // tn),
        in_specs=[
            pl.BlockSpec((tm, D), lambda i, j: (i, 0)),
            pl.BlockSpec((1, D), lambda i, j: (0, 0)),
            pl.BlockSpec((D, tn), lambda i, j: (0, j)),
            pl.BlockSpec((1, 1, tn), lambda i, j: (j, 0, 0)),
        ],
        out_specs=out_specs,
        out_shape=out_shape,
        scratch_shapes=[pltpu.VMEM((tm, D), BF16)],
        compiler_params=_params("parallel", "arbitrary"),
        name="norm_proj",
    )(x, g.reshape(1, D), w, gains)
    return res if has_gate else res[0]


def _res_proj_body(prologue, n_rows, *refs):
    row_refs = refs[:n_rows]
    w_ref, x_ref, o_ref = refs[n_rows:n_rows + 3]
    a = prologue(*row_refs)
    o_ref[...] = x_ref[...] + _dot(a, w_ref[...])


def _res_proj(rows, row_specs, prologue, w, x, tm, name):
    T, D = x.shape
    K = w.shape[0]
    return pl.pallas_call(
        functools.partial(_res_proj_body, prologue, len(rows)),
        grid=(T // tm,),
        in_specs=list(row_specs) + [
            pl.BlockSpec((K, D), lambda i: (0, 0)),
            pl.BlockSpec((tm, D), lambda i: (i, 0)),
        ],
        out_specs=pl.BlockSpec((tm, D), lambda i: (i, 0)),
        out_shape=jax.ShapeDtypeStruct((T, D), F32),
        compiler_params=_params("parallel"),
        name=name,
    )(*rows, w, x)


def _mlp_body(x_ref, g_ref, w1_ref, w2_ref, o_ref, h_scr):
    f = pl.program_id(1)

    @pl.when(f == 0)
    def _():
        xv = x_ref[...]
        h_scr[...] = _rms(xv, g_ref[...]).astype(BF16)
        o_ref[...] = xv

    a = jnp.maximum(_dot(h_scr[...], w1_ref[...]), 0.0)
    o_ref[...] += _dot((a * a).astype(BF16), w2_ref[...])


def _mlp(x, g, w1, w2, tm, tf):
    T, D = x.shape
    FF = w1.shape[1]
    return pl.pallas_call(
        _mlp_body,
        grid=(T // tm, FF // tf),
        in_specs=[
            pl.BlockSpec((tm, D), lambda i, f: (i, 0)),
            pl.BlockSpec((1, D), lambda i, f: (0, 0)),
            pl.BlockSpec((D, tf), lambda i, f: (0, f)),
            pl.BlockSpec((tf, D), lambda i, f: (f, 0)),
        ],
        out_specs=pl.BlockSpec((tm, D), lambda i, f: (i, 0)),
        out_shape=jax.ShapeDtypeStruct((T, D), F32),
        scratch_shapes=[pltpu.VMEM((tm, D), BF16)],
        compiler_params=_params("parallel", "arbitrary"),
        name="mlp",
    )(x, g.reshape(1, D), w1, w2)


HALO = 16


def _conv_prologue(tm, tiles_per_seq, b_ref, c_ref, v_ref, cp_ref, vp_ref, cw_ref):
    i = pl.program_id(0)
    first = (i % tiles_per_seq) == 0
    u = c_ref[...].astype(F32) * v_ref[...].astype(F32)
    up = cp_ref[...].astype(F32) * vp_ref[...].astype(F32)
    up = jnp.where(first, 0.0, up)
    row = lax.broadcasted_iota(jnp.int32, u.shape, 0)
    r1 = jnp.where(row == 0, up[HALO - 1:HALO, :], pltpu.roll(u, 1, 0))
    r2 = pltpu.roll(u, 2, 0)
    r2 = jnp.where(row == 0, up[HALO - 2:HALO - 1, :], jnp.where(row == 1, up[HALO - 1:HALO, :], r2))
    cw = cw_ref[...]
    y = cw[0:1, :] * r2 + cw[1:2, :] * r1 + cw[2:3, :] * u
    return (b_ref[...].astype(F32) * y).astype(BF16)


def _conv_out(bcv, conv_w, w_out, x, seq, tm):
    T, D = x.shape
    hb = tm // HALO
    prev = lambda col: (lambda i: (jnp.maximum(i * hb - 1, 0), col))
    specs = [
        pl.BlockSpec((tm, D), lambda i: (i, 0)),
        pl.BlockSpec((tm, D), lambda i: (i, 1)),
        pl.BlockSpec((tm, D), lambda i: (i, 2)),
        pl.BlockSpec((HALO, D), prev(1)),
        pl.BlockSpec((HALO, D), prev(2)),
        pl.BlockSpec((CONV_WIDTH, D), lambda i: (0, 0)),
    ]
    prologue = functools.partial(_conv_prologue, tm, seq // tm)
    return _res_proj([bcv, bcv, bcv, bcv, bcv, conv_w], specs, prologue, w_out, x, tm, "conv_out")


def _pool_body(tm, tiles_per_seq, x_ref, xp_ref, g_ref, w_ref, sc_ref, o_ref):
    i = pl.program_id(0)
    first = (i % tiles_per_seq) == 0
    xv = x_ref[...]
    g = g_ref[...]
    h = _rms(xv, g)
    hp = jnp.where(first, 0.0, _rms(xp_ref[...], g))
    pos = (i % tiles_per_seq) * tm + lax.broadcasted_iota(jnp.int32, (tm, 1), 0)
    cg = h.shape[1] // len(POOL_WINDOWS)
    for gi, win in enumerate(POOL_WINDOWS):
        sl = slice(gi * cg, (gi + 1) * cg)
        hg = h[:, sl]
        s = jnp.concatenate([hp[:, sl], hg], axis=0)
        k = 1
        while k < win:
            s = s + pltpu.roll(s, k, 0)
            k *= 2
        cnt = jnp.minimum(pos + 1, win).astype(F32)
        pooled = s[HALO:, :] / cnt - hg
        y = _dot(pooled.astype(BF16), w_ref[gi])
        o_ref[:, sl] = xv[:, sl] + y * sc_ref[:, sl]


def _pool_mixer(x, g, w, scale, seq, tm):
    T, D = x.shape
    ng, cg, _ = w.shape
    hb = tm // HALO
    return pl.pallas_call(
        functools.partial(_pool_body, tm, seq // tm),
        grid=(T // tm,),
        in_specs=[
            pl.BlockSpec((tm, D), lambda i: (i, 0)),
            pl.BlockSpec((HALO, D), lambda i: (jnp.maximum(i * hb - 1, 0), 0)),
            pl.BlockSpec((1, D), lambda i: (0, 0)),
            pl.BlockSpec((ng, cg, cg), lambda i: (0, 0, 0)),
            pl.BlockSpec((1, D), lambda i: (0, 0)),
        ],
        out_specs=pl.BlockSpec((tm, D), lambda i: (i, 0)),
        out_shape=jax.ShapeDtypeStruct((T, D), F32),
        compiler_params=_params("parallel"),
        name="pool_mixer",
    )(x, x, g.reshape(1, D), w, scale.reshape(1, D))


LOG2E = 1.4426950408889634
SB_STOP_LOG2 = -151.0


def _sb_body(tq, n_chain, q_ref, k_ref, v_ref, o_ref):
    nq = q_ref.shape[0] // tq
    row = lax.broadcasted_iota(jnp.int32, (tq, tq), 0)
    col = lax.broadcasted_iota(jnp.int32, (tq, tq), 1)
    before_diag = col < row
    r2 = lax.broadcasted_iota(jnp.int32, (2 * tq, 2 * tq), 0)
    c2 = lax.broadcasted_iota(jnp.int32, (2 * tq, 2 * tq), 1)
    key = jnp.where(r2 >= tq, r2 - tq, r2)
    sums = ((c2 >= tq) | (key > c2)).astype(BF16)

    def tiles(qs, js, cs, accs, diag, lives=None):
        chains = range(n_chain)
        offs = [pl.multiple_of(js[r] * tq, tq) for r in chains]
        zs = [_dot_nt(qs[r], k_ref[pl.ds(offs[r], tq), :]) for r in chains]
        log_1m = [jnp.log(1.0 + jnp.exp2(-jnp.abs(z))) * (-LOG2E) - jnp.maximum(z, 0.0) for z in zs]
        if diag:
            log_1m = [jnp.where(before_diag, x, 0.0) for x in log_1m]
        his = [x.astype(BF16) for x in log_1m]
        mids = [(x - hi.astype(F32)).astype(BF16) for x, hi in zip(log_1m, his)]
        ts = [_dot(jnp.concatenate([hi, mid], axis=1), sums) for hi, mid in zip(his, mids)]
        ws = [jnp.exp2(zs[r] + log_1m[r] + ts[r][:, :tq] + cs[r]) for r in chains]
        tile_sums = [t[:, tq:] for t in ts]
        if diag:
            ws = [jnp.where(before_diag, w, 0.0) for w in ws]
        if lives is not None:
            ws = [jnp.where(lives[r], ws[r], 0.0) for r in chains]
            tile_sums = [jnp.where(lives[r], tile_sums[r], 0.0) for r in chains]
        accs = tuple(accs[r] + _dot(ws[r].astype(BF16), v_ref[pl.ds(offs[r], tq), :]) for r in chains)
        return tuple(cs[r] + tile_sums[r] for r in chains), accs

    def q_group(gi, carry):
        base = gi * n_chain
        chains = range(n_chain)
        qs = [q_ref[pl.ds(pl.multiple_of((base + r) * tq, tq), tq), :] for r in chains]
        zero = (jnp.zeros((tq, tq), F32),) * n_chain
        cs, accs = tiles(qs, [base + r for r in chains], zero, zero, True)

        def live_max(cs, n):
            vals = [jnp.where(base + r - n >= 0, jnp.max(cs[r]), -jnp.inf) for r in chains]
            return functools.reduce(jnp.maximum, vals)

        def cond(st):
            n, _, _, cmax = st
            return (n <= base + n_chain - 1) & (cmax > SB_STOP_LOG2)

        def body(st):
            n, cs, accs, _ = st
            js = [base + r - n for r in chains]
            cs, accs = tiles(qs, [jnp.maximum(j, 0) for j in js], cs, accs, False, [j >= 0 for j in js])
            return n + 1, cs, accs, live_max(cs, n + 1)

        _, _, accs, _ = lax.while_loop(cond, body, (1, cs, accs, live_max(cs, 1)))
        for r in chains:
            o_ref[pl.ds(pl.multiple_of((base + r) * tq, tq), tq), :] = accs[r].astype(BF16)
        return carry

    lax.fori_loop(0, nq // n_chain, q_group, 0)


def _sb_attention(qkv, bsz, seq, n_heads, tq, n_chain):
    assert tq == HEAD_DIM == LANES
    assert seq % (tq * n_chain) == 0
    T = bsz * seq
    return pl.pallas_call(
        functools.partial(_sb_body, tq, n_chain),
        grid=(bsz, n_heads),
        in_specs=[
            pl.BlockSpec((seq, HEAD_DIM), lambda b, h: (b, h)),
            pl.BlockSpec((seq, HEAD_DIM), lambda b, h: (b, n_heads + h)),
            pl.BlockSpec((seq, HEAD_DIM), lambda b, h: (b, 2 * n_heads + h)),
        ],
        out_specs=pl.BlockSpec((seq, HEAD_DIM), lambda b, h: (b, h)),
        out_shape=jax.ShapeDtypeStruct((T, n_heads * HEAD_DIM), BF16),
        compiler_params=_params("parallel", "parallel"),
        name="sb_attention",
    )(qkv, qkv, qkv)


def _compress_body(n_chunk, a_ref, w1_ref, w2_ref, pos_ref, gain_ref, o_ref):
    kv = pl.program_id(0)
    a = a_ref[0, 0, 0]
    half = a.shape[1]
    w1 = w1_ref[0]
    const = _dot(pos_ref[0], w1)[0:1, :]
    first = _dot(a, w1[:half, :])
    second = _dot(a, w1[half:, :])
    pre = first + pltpu.roll(second, n_chunk - 1, 0) + const
    hid = pre * jax.nn.sigmoid(pre)
    out = _dot(hid.astype(BF16), w2_ref[0])
    normed = _rms(out, gain_ref[...])
    o_ref[0, 0, 0] = jnp.where(kv == 0, normed, out).astype(BF16)


def _compress(a, w1, w2, pos, gain):
    _, bsz, ng, n_chunk, width = a.shape
    dh = HEAD_DIM
    return pl.pallas_call(
        functools.partial(_compress_body, n_chunk),
        grid=(2, bsz, ng),
        in_specs=[
            pl.BlockSpec((1, 1, 1, n_chunk, width), lambda s, b, g: (s, b, g, 0, 0)),
            pl.BlockSpec((1, 2 * width, dh), lambda s, b, g: (s, 0, 0)),
            pl.BlockSpec((1, dh, dh), lambda s, b, g: (s, 0, 0)),
            pl.BlockSpec((1, 8, 2 * width), lambda s, b, g: (s, 0, 0)),
            pl.BlockSpec((1, dh), lambda s, b, g: (0, 0)),
        ],
        out_specs=pl.BlockSpec((1, 1, 1, n_chunk, dh), lambda s, b, g: (s, b, g, 0, 0)),
        out_shape=jax.ShapeDtypeStruct((2, bsz, ng, n_chunk, dh), BF16),
        compiler_params=_params("parallel", "parallel", "parallel"),
        name="nsa_compress",
    )(a, w1, w2, pos, gain)


def _stack_heads(q_ref, hpg):
    return jnp.concatenate([q_ref[:, h * HEAD_DIM:(h + 1) * HEAD_DIM] for h in range(hpg)], axis=0)


def _gate_column(gates, col):
    lane = lax.broadcasted_iota(jnp.int32, gates.shape, 1)
    return jnp.sum(jnp.where(lane == col, gates, 0.0), axis=1, keepdims=True)


def _nsa_cmp_body(tq, hpg, n_sel, q_ref, kc_ref, vc_ref, gates_ref, o_ref, sel_ref, vt_scr, st_scr):
    g = pl.program_id(1)
    i = pl.program_id(2)
    qs = _stack_heads(q_ref, hpg)
    kc = kc_ref[0, 0, 0]
    vc = vc_ref[0, 0, 0]
    n_chunk = kc.shape[0]
    s = _dot_nt(qs, kc)
    t1 = i * tq + lax.broadcasted_iota(jnp.int32, (tq, n_chunk), 0)
    c1 = lax.broadcasted_iota(jnp.int32, (tq, n_chunk), 1)
    valid1 = c1 * CMP_STRIDE + (CMP_LEN - 1) <= t1
    valid = jnp.concatenate([valid1] * hpg, axis=0)
    s = jnp.where(valid, s, NEG)
    m = jnp.max(s, axis=1, keepdims=True)
    e = jnp.where(valid, jnp.exp2(s - m), 0.0)
    l = jnp.sum(e, axis=1, keepdims=True)
    p = e / jnp.where(l > 0.0, l, 1.0)
    o = _dot(p.astype(BF16), vc)
    gates = gates_ref[...]
    for h in range(hpg):
        gc = _gate_column(gates, 3 * (g * hpg + h) + 0)
        o_ref[:, h * HEAD_DIM:(h + 1) * HEAD_DIM] = (gc * o[h * tq:(h + 1) * tq, :]).astype(BF16)

    p_sum = p[0:tq, :]
    for h in range(1, hpg):
        p_sum = p_sum + p[h * tq:(h + 1) * tq, :]
    ci = lax.broadcasted_iota(jnp.int32, (n_chunk, LANES), 0)
    sj = lax.broadcasted_iota(jnp.int32, (n_chunk, LANES), 1)
    overlap = ((ci * CMP_STRIDE < (sj + 1) * SEL_LEN) & (ci * CMP_STRIDE + CMP_LEN > sj * SEL_LEN)
               & (ci < n_chunk - 1) & (sj < n_sel)).astype(BF16)
    imp = _split_dot(p_sum, overlap)
    t2 = i * tq + lax.broadcasted_iota(jnp.int32, (tq, LANES), 0)
    blk = lax.broadcasted_iota(jnp.int32, (tq, LANES), 1)
    cur = lax.shift_right_logical(t2, SEL_SHIFT)
    forced = (blk == 0) | (blk == cur) | (blk == cur - 1)
    blk_valid = blk * SEL_LEN <= t2
    score = jnp.where(forced, BIG, jnp.where(blk_valid, imp, -BIG))

    vt_scr[...] = score.T
    vt = vt_scr[0:n_sel, :]
    jrow = lax.broadcasted_iota(jnp.int32, (n_sel, tq), 0)
    cnt = jnp.zeros((n_sel, tq), F32)
    for b in range(n_sel):
        vb = vt_scr[b:b + 1, :]
        ahead = (vb > vt) | ((vb == vt) & (jrow > b))
        cnt = cnt + ahead.astype(F32)
    st_scr[...] = jnp.zeros_like(st_scr)
    st_scr[0:n_sel, :] = (cnt < float(min(SEL_TOPK, n_sel))).astype(F32)
    sel_ref[0, 0] = st_scr[...].T.astype(BF16)


def _nsa_compressed(qkv, kvc, gates, bsz, seq, ng, hpg, tq):
    T = bsz * seq
    nq = seq // tq
    n_chunk = kvc.shape[3]
    n_sel = seq // SEL_LEN
    assert n_sel <= LANES
    gw = hpg * HEAD_DIM
    return pl.pallas_call(
        functools.partial(_nsa_cmp_body, tq, hpg, n_sel),
        grid=(bsz, ng, nq),
        in_specs=[
            pl.BlockSpec((tq, gw), lambda b, g, i: (b * nq + i, g)),
            pl.BlockSpec((1, 1, 1, n_chunk, HEAD_DIM), lambda b, g, i: (0, b, g, 0, 0)),
            pl.BlockSpec((1, 1, 1, n_chunk, HEAD_DIM), lambda b, g, i: (1, b, g, 0, 0)),
            pl.BlockSpec((tq, LANES), lambda b, g, i: (b * nq + i, 0)),
        ],
        out_specs=[
            pl.BlockSpec((tq, gw), lambda b, g, i: (b * nq + i, g)),
            pl.BlockSpec((1, 1, tq, LANES), lambda b, g, i: (b, g, i, 0)),
        ],
        out_shape=[
            jax.ShapeDtypeStruct((T, ng * gw), BF16),
            jax.ShapeDtypeStruct((bsz, ng, seq, LANES), BF16),
        ],
        scratch_shapes=[pltpu.VMEM((LANES, tq), F32), pltpu.VMEM((LANES, tq), F32)],
        compiler_params=_params("parallel", "parallel", "parallel"),
        name="nsa_compressed",
    )(qkv, kvc, kvc, gates)


def _fill_values_and_ones(vo_scr, v_ref):
    vo_scr[:, :HEAD_DIM] = v_ref[...]
    vo_scr[:, HEAD_DIM:] = jnp.ones((vo_scr.shape[0], LANES), BF16)


def _store_gated(o_ref, gates_ref, out, g, branch, tq, hpg):
    gates = gates_ref[...]
    for h in range(hpg):
        gc = _gate_column(gates, 3 * (g * hpg + h) + branch)
        o_ref[:, h * HEAD_DIM:(h + 1) * HEAD_DIM] = (gc * out[h * tq:(h + 1) * tq, :]).astype(BF16)


def _nsa_sel_body(tq, tk, hpg, q_ref, k_ref, v_ref, gates_ref, sel_ref, o_ref, m_scr, acc_scr, vo_scr):
    g = pl.program_id(1)
    i = pl.program_id(2)
    dh = HEAD_DIM

    @pl.when(i == 0)
    def _():
        _fill_values_and_ones(vo_scr, v_ref)

    qs = _stack_heads(q_ref, hpg)
    m_scr[...] = jnp.full_like(m_scr, NEG)
    acc_scr[...] = jnp.zeros_like(acc_scr)
    qpos = i * tq + lax.broadcasted_iota(jnp.int32, (tq, tk), 0)
    kcol = lax.broadcasted_iota(jnp.int32, (tq, tk), 1)
    sel = sel_ref[0, 0]
    eb = lax.broadcasted_iota(jnp.int32, (LANES, tk), 0)
    ek = lax.broadcasted_iota(jnp.int32, (LANES, tk), 1)
    n_tiles = ((i + 1) * tq + tk - 1) // tk

    def step(n, carry):
        j = n_tiles - 1 - n
        off = pl.multiple_of(j * tk, tk)
        k = k_ref[pl.ds(off, tk), :]
        vo = vo_scr[pl.ds(off, tk), :]
        expand = (eb == lax.shift_right_logical(j * tk + ek, SEL_SHIFT)).astype(BF16)
        mask1 = (_dot(sel, expand) > 0.5) & (j * tk + kcol <= qpos)
        bias1 = jnp.where(mask1, 0.0, NEG)
        s = _dot_nt(qs, k) + jnp.concatenate([bias1] * hpg, axis=0)
        chunks = [s[:, c * LANES:(c + 1) * LANES] for c in range(tk // LANES)]
        mx = chunks[0]
        for ch in chunks[1:]:
            mx = jnp.maximum(mx, ch)
        m_old = m_scr[...]
        m_new = jnp.maximum(m_old, jnp.max(mx, axis=1, keepdims=True))
        alpha = jnp.exp2(m_old - m_new)
        p = jnp.concatenate([jnp.exp2(ch - m_new).astype(BF16) for ch in chunks], axis=1)
        pv = _dot(p, vo)
        acc_scr[:, :dh] = alpha * acc_scr[:, :dh] + pv[:, :dh]
        acc_scr[:, dh:] = alpha * acc_scr[:, dh:] + pv[:, dh:]
        m_scr[...] = m_new
        return carry

    lax.fori_loop(0, n_tiles, step, 0)
    _store_gated(o_ref, gates_ref, acc_scr[:, :dh] / acc_scr[:, dh:], g, 1, tq, hpg)


def _nsa_win_body(tq, hpg, q_ref, k_ref, v_ref, gates_ref, o_ref, vo_scr):
    g = pl.program_id(1)
    i = pl.program_id(2)
    dh = HEAD_DIM
    span = WINDOW + tq

    @pl.when(i == 0)
    def _():
        _fill_values_and_ones(vo_scr, v_ref)

    qs = _stack_heads(q_ref, hpg)
    off = pl.multiple_of(jnp.maximum(i * tq - WINDOW, 0), tq)
    k = k_ref[pl.ds(off, span), :]
    vo = vo_scr[pl.ds(off, span), :]
    qpos = i * tq + lax.broadcasted_iota(jnp.int32, (tq, span), 0)
    kpos = off + lax.broadcasted_iota(jnp.int32, (tq, span), 1)
    bias1 = jnp.where((kpos <= qpos) & (qpos - kpos < WINDOW), 0.0, NEG)
    s = _dot_nt(qs, k) + jnp.concatenate([bias1] * hpg, axis=0)
    p = jnp.exp2(s - jnp.max(s, axis=1, keepdims=True)).astype(BF16)
    pv = _dot(p, vo)
    _store_gated(o_ref, gates_ref, pv[:, :dh] / pv[:, dh:], g, 2, tq, hpg)


def _nsa_attend(mode, qkv, gates, sel, k_col, v_col, bsz, seq, ng, hpg, tq, tk):
    T = bsz * seq
    nq = seq // tq
    gw = hpg * HEAD_DIM
    rows = hpg * tq
    in_specs = [
        pl.BlockSpec((tq, gw), lambda b, g, i: (b * nq + i, g)),
        pl.BlockSpec((seq, HEAD_DIM), lambda b, g, i: (b, k_col + g)),
        pl.BlockSpec((seq, HEAD_DIM), lambda b, g, i: (b, v_col + g)),
        pl.BlockSpec((tq, LANES), lambda b, g, i: (b * nq + i, 0)),
    ]
    args = [qkv, qkv, qkv, gates]
    scratch = [pltpu.VMEM((seq, HEAD_DIM + LANES), BF16)]
    if mode == "sel":
        in_specs.append(pl.BlockSpec((1, 1, tq, LANES), lambda b, g, i: (b, g, i, 0)))
        args.append(sel)
        scratch = [pltpu.VMEM((rows, LANES), F32), pltpu.VMEM((rows, HEAD_DIM + LANES), F32)] + scratch
        body = functools.partial(_nsa_sel_body, tq, tk, hpg)
    else:
        assert seq >= WINDOW + tq and WINDOW % tq == 0
        body = functools.partial(_nsa_win_body, tq, hpg)
    return pl.pallas_call(
        body,
        grid=(bsz, ng, nq),
        in_specs=in_specs,
        out_specs=pl.BlockSpec((tq, gw), lambda b, g, i: (b * nq + i, g)),
        out_shape=jax.ShapeDtypeStruct((T, ng * gw), BF16),
        scratch_shapes=scratch,
        compiler_params=_params("parallel", "parallel", "arbitrary"),
        name="nsa_" + mode,
    )(*args)


def _sum3_prologue(a_ref, b_ref, c_ref):
    return (a_ref[...].astype(F32) + b_ref[...].astype(F32) + c_ref[...].astype(F32)).astype(BF16)


def _ident_prologue(a_ref):
    return a_ref[...]


TM_PROJ = 1024
TN_PROJ = 512
TM_OUT = 256
TM_MLP = 512
TF_MLP = 1024
TM_POOL = 256
TQ_SB = 128
SB_CHAINS = 8
TQ_NSA = 128
TK_SEL = 512


def _tile_gains(per_tile, tn):
    rows = [jnp.tile(g.astype(F32), tn // HEAD_DIM) if g is not None else jnp.ones((tn,), F32) for g in per_tile]
    return jnp.stack(rows).reshape(len(per_tile), 1, tn)


def _conv_layer(x, seq, norm_g, w_in, conv_w, w_out):
    n_tiles = w_in.shape[1] // TN_PROJ
    bcv = _norm_proj(x, norm_g, w_in.astype(BF16), _tile_gains([None] * n_tiles, TN_PROJ),
                     ["plain"] * n_tiles, TM_PROJ, TN_PROJ)
    return _conv_out(bcv, conv_w, w_out.astype(BF16), x, seq, TM_OUT)


def _nsa_layer(x, bsz, seq, norm_g, w_in, q_gain, k_gain, cmp_pos, cmp_w1, cmp_w2, w_out):
    T, D = x.shape
    dh, G = HEAD_DIM, NSA_KV_GROUPS
    H = D // dh
    hpg = H // G
    gw = hpg * dh
    assert gw == TN_PROJ and G * dh == TN_PROJ
    scale = LOG2E * dh ** -0.5
    width = w_in.shape[1]
    n_tiles = -(-width // TN_PROJ)
    w_pad = jnp.pad(w_in, ((0, 0), (0, n_tiles * TN_PROJ - width))).astype(BF16)
    modes = ["norm"] * G + ["plain", "plain", "norm", "plain", "norm", "plain", "gate"]
    gains = _tile_gains([q_gain * scale] * G + [None, None, k_gain[1], None, k_gain[2], None, None], TN_PROJ)
    qkv, gates = _norm_proj(x, norm_g, w_pad, gains, modes, TM_PROJ, TN_PROJ)

    n_chunk = seq // CMP_STRIDE
    kvc_in = qkv[:, H * dh:H * dh + 2 * G * dh].reshape(bsz, n_chunk, CMP_STRIDE, 2, G, dh)
    kvc_in = kvc_in.transpose(3, 0, 4, 1, 2, 5).reshape(2, bsz, G, n_chunk, CMP_STRIDE * dh)
    pos = jnp.broadcast_to(cmp_pos.reshape(2, 1, CMP_LEN * dh), (2, 8, CMP_LEN * dh)).astype(BF16)
    kvc = _compress(kvc_in, cmp_w1.astype(BF16), cmp_w2.astype(BF16), pos, k_gain[0].reshape(1, dh))

    col = lambda t: t * (TN_PROJ // dh)
    o_cmp, sel = _nsa_compressed(qkv, kvc, gates, bsz, seq, G, hpg, TQ_NSA)
    o_sel = _nsa_attend("sel", qkv, gates, sel, col(G + 2), col(G + 3), bsz, seq, G, hpg, TQ_NSA, TK_SEL)
    o_win = _nsa_attend("win", qkv, gates, None, col(G + 4), col(G + 5), bsz, seq, G, hpg, TQ_NSA, None)
    spec = pl.BlockSpec((TM_OUT, H * dh), lambda i: (i, 0))
    return _res_proj([o_cmp, o_sel, o_win], [spec] * 3, _sum3_prologue, w_out.astype(BF16), x, TM_OUT, "nsa_out")


def _sb_layer(x, bsz, seq, norm_g, w_in, q_gain, k_gain, w_out):
    T, D = x.shape
    dh = HEAD_DIM
    H = D // dh
    scale = LOG2E * dh ** -0.5
    per = (H * dh) // TN_PROJ
    modes = ["norm"] * (2 * per) + ["plain"] * per
    gains = _tile_gains([q_gain * scale] * per + [k_gain] * per + [None] * per, TN_PROJ)
    qkv = _norm_proj(x, norm_g, w_in.astype(BF16), gains, modes, TM_PROJ, TN_PROJ)
    o = _sb_attention(qkv, bsz, seq, H, TQ_SB, SB_CHAINS)
    spec = pl.BlockSpec((TM_OUT, H * dh), lambda i: (i, 0))
    return _res_proj([o], [spec], _ident_prologue, w_out.astype(BF16), x, TM_OUT, "sb_out")


def kernel(x, mix_norm, mlp_norm, mlp_w1, mlp_w2, conv_w_in, conv_w, conv_w_out, nsa_w_in, nsa_q_gain, nsa_k_gain, nsa_cmp_pos, nsa_cmp_w1, nsa_cmp_w2, nsa_w_out, pool_w, pool_scale, sb_w_in, sb_q_gain, sb_k_gain, sb_w_out):
    bsz, seq, d = x.shape
    depth = mix_norm.shape[0]
    n_mixers = 4
    xf = x.reshape(bsz * seq, d)
    for i in range(depth):
        kind, j = i % n_mixers, i // n_mixers
        if kind == 0:
            xf = _conv_layer(xf, seq, mix_norm[i], conv_w_in[j], conv_w[j], conv_w_out[j])
        elif kind == 1:
            xf = _nsa_layer(xf, bsz, seq, mix_norm[i], nsa_w_in[j], nsa_q_gain[j], nsa_k_gain[j],
                            nsa_cmp_pos[j], nsa_cmp_w1[j], nsa_cmp_w2[j], nsa_w_out[j])
        elif kind == 2:
            xf = _pool_mixer(xf, mix_norm[i], pool_w[j].astype(BF16), pool_scale[j], seq, TM_POOL)
        else:
            xf = _sb_layer(xf, bsz, seq, mix_norm[i], sb_w_in[j], sb_q_gain[j], sb_k_gain[j], sb_w_out[j])
        xf = _mlp(xf, mlp_norm[i], mlp_w1[i].astype(BF16), mlp_w2[i].astype(BF16), TM_MLP, TF_MLP)
    return xf.reshape(bsz, seq, d)
```

```python
import functools

import jax
import jax.numpy as jnp
from jax import lax
from jax.experimental import pallas as pl
from jax.experimental.pallas import tpu as pltpu

F32 = jnp.float32
BF16 = jnp.bfloat16

HEAD_DIM = 128
EPS = 1e-6
NEG = -1e30
BIG = 1e4
CONV_WIDTH = 3
NSA_KV_GROUPS = 4
CMP_LEN = 32
CMP_STRIDE = 16
SEL_LEN = 64
SEL_SHIFT = SEL_LEN.bit_length() - 1
assert 1 << SEL_SHIFT == SEL_LEN
SEL_TOPK = 16
WINDOW = 512
POOL_WINDOWS = (2, 4, 8, 16)
LANES = 128
VMEM_LIMIT = 56 * 1024 * 1024


def _params(*sem):
    return pltpu.CompilerParams(dimension_semantics=sem, vmem_limit_bytes=VMEM_LIMIT)


def _rms(xv, g):
    ms = jnp.mean(xv * xv, axis=-1, keepdims=True)
    return xv * lax.rsqrt(ms + EPS) * g


def _dot(a, b):
    return jnp.dot(a, b, preferred_element_type=F32)


def _dot_nt(a, b):
    return lax.dot_general(a, b, (((1,), (1,)), ((), ())), preferred_element_type=F32)


def _split_dot(a, b):
    hi = a.astype(BF16)
    r1 = a - hi.astype(F32)
    mid = r1.astype(BF16)
    lo = (r1 - mid.astype(F32)).astype(BF16)
    return _dot(hi, b) + _dot(mid, b) + _dot(lo, b)


def _norm_proj_body(modes, tn, has_gate, x_ref, g_ref, w_ref, gain_ref, *rest):
    if has_gate:
        o_ref, og_ref, h_scr = rest
    else:
        o_ref, h_scr = rest
    j = pl.program_id(1)

    @pl.when(j == 0)
    def _():
        h_scr[...] = _rms(x_ref[...], g_ref[...]).astype(BF16)

    acc = _dot(h_scr[...], w_ref[...])

    def tiles_of(mode):
        return [t for t, m in enumerate(modes) if m == mode]

    def any_of(tiles):
        c = j == tiles[0]
        for t in tiles[1:]:
            c = c | (j == t)
        return c

    if tiles_of("plain"):
        @pl.when(any_of(tiles_of("plain")))
        def _():
            o_ref[...] = acc.astype(BF16)

    if tiles_of("norm"):
        @pl.when(any_of(tiles_of("norm")))
        def _():
            gain = gain_ref[0]
            for h in range(tn // HEAD_DIM):
                sl = slice(h * HEAD_DIM, (h + 1) * HEAD_DIM)
                o_ref[:, sl] = _rms(acc[:, sl], gain[:, sl]).astype(BF16)

    if tiles_of("gate"):
        @pl.when(any_of(tiles_of("gate")))
        def _():
            o_ref[...] = acc.astype(BF16)
            og_ref[...] = jax.nn.sigmoid(acc[:, :LANES])


def _norm_proj(x, g, w, gains, modes, tm, tn):
    T, D = x.shape
    N = w.shape[1]
    assert N == len(modes) * tn and T % tm == 0
    has_gate = "gate" in modes
    out_shape = [jax.ShapeDtypeStruct((T, N), BF16)]
    out_specs = [pl.BlockSpec((tm, tn), lambda i, j: (i, j))]
    if has_gate:
        out_shape.append(jax.ShapeDtypeStruct((T, LANES), F32))
        out_specs.append(pl.BlockSpec((tm, LANES), lambda i, j: (i, 0)))
    res = pl.pallas_call(
        functools.partial(_norm_proj_body, tuple(modes), tn, has_gate),
        grid=(T // tm, N // tn),
        in_specs=[
            pl.BlockSpec((tm, D), lambda i, j: (i, 0)),
            pl.BlockSpec((1, D), lambda i, j: (0, 0)),
            pl.BlockSpec((D, tn), lambda i, j: (0, j)),
            pl.BlockSpec((1, 1, tn), lambda i, j: (j, 0, 0)),
        ],
        out_specs=out_specs,
        out_shape=out_shape,
        scratch_shapes=[pltpu.VMEM((tm, D), BF16)],
        compiler_params=_params("parallel", "arbitrary"),
        name="norm_proj",
    )(x, g.reshape(1, D), w, gains)
    return res if has_gate else res[0]


def _res_proj_body(prologue, n_rows, *refs):
    row_refs = refs[:n_rows]
    w_ref, x_ref, o_ref = refs[n_rows:n_rows + 3]
    a = prologue(*row_refs)
    o_ref[...] = x_ref[...] + _dot(a, w_ref[...])


def _res_proj(rows, row_specs, prologue, w, x, tm, name):
    T, D = x.shape
    K = w.shape[0]
    return pl.pallas_call(
        functools.partial(_res_proj_body, prologue, len(rows)),
        grid=(T // tm,),
        in_specs=list(row_specs) + [
            pl.BlockSpec((K, D), lambda i: (0, 0)),
            pl.BlockSpec((tm, D), lambda i: (i, 0)),
        ],
        out_specs=pl.BlockSpec((tm, D), lambda i: (i, 0)),
        out_shape=jax.ShapeDtypeStruct((T, D), F32),
        compiler_params=_params("parallel"),
        name=name,
    )(*rows, w, x)


def _mlp_body(x_ref, g_ref, w1_ref, w2_ref, o_ref, h_scr):
    f = pl.program_id(1)

    @pl.when(f == 0)
    def _():
        xv = x_ref[...]
        h_scr[...] = _rms(xv, g_ref[...]).astype(BF16)
        o_ref[...] = xv

    a = jnp.maximum(_dot(h_scr[...], w1_ref[...]), 0.0)
    o_ref[...] += _dot((a * a).astype(BF16), w2_ref[...])


def _mlp(x, g, w1, w2, layer, tm, tf):
    T, D = x.shape
    FF = w1.shape[2]
    return pl.pallas_call(
        _mlp_body,
        grid=(T // tm, FF // tf),
        in_specs=[
            pl.BlockSpec((tm, D), lambda i, f: (i, 0)),
            pl.BlockSpec((1, D), lambda i, f: (0, 0)),
            pl.BlockSpec((None, D, tf), lambda i, f: (layer, 0, f)),
            pl.BlockSpec((None, tf, D), lambda i, f: (layer, f, 0)),
        ],
        out_specs=pl.BlockSpec((tm, D), lambda i, f: (i, 0)),
        out_shape=jax.ShapeDtypeStruct((T, D), F32),
        scratch_shapes=[pltpu.VMEM((tm, D), BF16)],
        compiler_params=_params("parallel", "arbitrary"),
        name="mlp",
    )(x, g.reshape(1, D), w1, w2)


HALO = 16


def _conv_prologue(tm, tiles_per_seq, b_ref, c_ref, v_ref, cp_ref, vp_ref, cw_ref):
    i = pl.program_id(0)
    first = (i % tiles_per_seq) == 0
    u = c_ref[...].astype(F32) * v_ref[...].astype(F32)
    up = cp_ref[...].astype(F32) * vp_ref[...].astype(F32)
    up = jnp.where(first, 0.0, up)
    row = lax.broadcasted_iota(jnp.int32, u.shape, 0)
    r1 = jnp.where(row == 0, up[HALO - 1:HALO, :], pltpu.roll(u, 1, 0))
    r2 = pltpu.roll(u, 2, 0)
    r2 = jnp.where(row == 0, up[HALO - 2:HALO - 1, :], jnp.where(row == 1, up[HALO - 1:HALO, :], r2))
    cw = cw_ref[...]
    y = cw[0:1, :] * r2 + cw[1:2, :] * r1 + cw[2:3, :] * u
    return (b_ref[...].astype(F32) * y).astype(BF16)


def _conv_out(bcv, conv_w, w_out, x, seq, tm):
    T, D = x.shape
    hb = tm // HALO
    prev = lambda col: (lambda i: (jnp.maximum(i * hb - 1, 0), col))
    specs = [
        pl.BlockSpec((tm, D), lambda i: (i, 0)),
        pl.BlockSpec((tm, D), lambda i: (i, 1)),
        pl.BlockSpec((tm, D), lambda i: (i, 2)),
        pl.BlockSpec((HALO, D), prev(1)),
        pl.BlockSpec((HALO, D), prev(2)),
        pl.BlockSpec((CONV_WIDTH, D), lambda i: (0, 0)),
    ]
    prologue = functools.partial(_conv_prologue, tm, seq // tm)
    return _res_proj([bcv, bcv, bcv, bcv, bcv, conv_w], specs, prologue, w_out, x, tm, "conv_out")


def _pool_body(tm, tiles_per_seq, x_ref, xp_ref, g_ref, w_ref, sc_ref, o_ref):
    i = pl.program_id(0)
    first = (i % tiles_per_seq) == 0
    xv = x_ref[...]
    g = g_ref[...]
    h = _rms(xv, g)
    hp = jnp.where(first, 0.0, _rms(xp_ref[...], g))
    pos = (i % tiles_per_seq) * tm + lax.broadcasted_iota(jnp.int32, (tm, 1), 0)
    cg = h.shape[1] // len(POOL_WINDOWS)
    for gi, win in enumerate(POOL_WINDOWS):
        sl = slice(gi * cg, (gi + 1) * cg)
        hg = h[:, sl]
        s = jnp.concatenate([hp[:, sl], hg], axis=0)
        k = 1
        while k < win:
            s = s + pltpu.roll(s, k, 0)
            k *= 2
        cnt = jnp.minimum(pos + 1, win).astype(F32)
        pooled = s[HALO:, :] / cnt - hg
        y = _dot(pooled.astype(BF16), w_ref[gi])
        o_ref[:, sl] = xv[:, sl] + y * sc_ref[:, sl]


def _pool_mixer(x, g, w, scale, seq, tm):
    T, D = x.shape
    ng, cg, _ = w.shape
    hb = tm // HALO
    return pl.pallas_call(
        functools.partial(_pool_body, tm, seq // tm),
        grid=(T // tm,),
        in_specs=[
            pl.BlockSpec((tm, D), lambda i: (i, 0)),
            pl.BlockSpec((HALO, D), lambda i: (jnp.maximum(i * hb - 1, 0), 0)),
            pl.BlockSpec((1, D), lambda i: (0, 0)),
            pl.BlockSpec((ng, cg, cg), lambda i: (0, 0, 0)),
            pl.BlockSpec((1, D), lambda i: (0, 0)),
        ],
        out_specs=pl.BlockSpec((tm, D), lambda i: (i, 0)),
        out_shape=jax.ShapeDtypeStruct((T, D), F32),
        compiler_params=_params("parallel"),
        name="pool_mixer",
    )(x, x, g.reshape(1, D), w, scale.reshape(1, D))


LOG2E = 1.4426950408889634
SB_STOP_LOG2 = -151.0


def _sb_body(tq, n_chain, q_ref, k_ref, v_ref, o_ref):
    nq = q_ref.shape[0] // tq
    row = lax.broadcasted_iota(jnp.int32, (tq, tq), 0)
    col = lax.broadcasted_iota(jnp.int32, (tq, tq), 1)
    before_diag = col < row
    r2 = lax.broadcasted_iota(jnp.int32, (2 * tq, 2 * tq), 0)
    c2 = lax.broadcasted_iota(jnp.int32, (2 * tq, 2 * tq), 1)
    key = jnp.where(r2 >= tq, r2 - tq, r2)
    sums = ((c2 >= tq) | (key > c2)).astype(BF16)

    def tiles(qs, js, cs, accs, diag, lives=None):
        chains = range(n_chain)
        offs = [pl.multiple_of(js[r] * tq, tq) for r in chains]
        zs = [_dot_nt(qs[r], k_ref[pl.ds(offs[r], tq), :]) for r in chains]
        log_1m = [jnp.log(1.0 + jnp.exp2(-jnp.abs(z))) * (-LOG2E) - jnp.maximum(z, 0.0) for z in zs]
        if diag:
            log_1m = [jnp.where(before_diag, x, 0.0) for x in log_1m]
        his = [x.astype(BF16) for x in log_1m]
        mids = [(x - hi.astype(F32)).astype(BF16) for x, hi in zip(log_1m, his)]
        ts = [_dot(jnp.concatenate([hi, mid], axis=1), sums) for hi, mid in zip(his, mids)]
        ws = [jnp.exp2(zs[r] + log_1m[r] + ts[r][:, :tq] + cs[r]) for r in chains]
        tile_sums = [t[:, tq:] for t in ts]
        if diag:
            ws = [jnp.where(before_diag, w, 0.0) for w in ws]
        if lives is not None:
            ws = [jnp.where(lives[r], ws[r], 0.0) for r in chains]
            tile_sums = [jnp.where(lives[r], tile_sums[r], 0.0) for r in chains]
        accs = tuple(accs[r] + _dot(ws[r].astype(BF16), v_ref[pl.ds(offs[r], tq), :]) for r in chains)
        return tuple(cs[r] + tile_sums[r] for r in chains), accs

    def q_group(gi, carry):
        base = gi * n_chain
        chains = range(n_chain)
        qs = [q_ref[pl.ds(pl.multiple_of((base + r) * tq, tq), tq), :] for r in chains]
        zero = (jnp.zeros((tq, tq), F32),) * n_chain
        cs, accs = tiles(qs, [base + r for r in chains], zero, zero, True)

        def live_max(cs, n):
            vals = [jnp.where(base + r - n >= 0, jnp.max(cs[r]), -jnp.inf) for r in chains]
            return functools.reduce(jnp.maximum, vals)

        def cond(st):
            n, _, _, cmax = st
            return (n <= base + n_chain - 1) & (cmax > SB_STOP_LOG2)

        def body(st):
            n, cs, accs, _ = st
            js = [base + r - n for r in chains]
            cs, accs = tiles(qs, [jnp.maximum(j, 0) for j in js], cs, accs, False, [j >= 0 for j in js])
            return n + 1, cs, accs, live_max(cs, n + 1)

        _, _, accs, _ = lax.while_loop(cond, body, (1, cs, accs, live_max(cs, 1)))
        for r in chains:
            o_ref[pl.ds(pl.multiple_of((base + r) * tq, tq), tq), :] = accs[r].astype(BF16)
        return carry

    lax.fori_loop(0, nq // n_chain, q_group, 0)


def _sb_attention(qkv, bsz, seq, n_heads, tq, n_chain):
    assert tq == HEAD_DIM == LANES
    assert seq % (tq * n_chain) == 0
    T = bsz * seq
    return pl.pallas_call(
        functools.partial(_sb_body, tq, n_chain),
        grid=(bsz, n_heads),
        in_specs=[
            pl.BlockSpec((seq, HEAD_DIM), lambda b, h: (b, h)),
            pl.BlockSpec((seq, HEAD_DIM), lambda b, h: (b, n_heads + h)),
            pl.BlockSpec((seq, HEAD_DIM), lambda b, h: (b, 2 * n_heads + h)),
        ],
        out_specs=pl.BlockSpec((seq, HEAD_DIM), lambda b, h: (b, h)),
        out_shape=jax.ShapeDtypeStruct((T, n_heads * HEAD_DIM), BF16),
        compiler_params=_params("parallel", "parallel"),
        name="sb_attention",
    )(qkv, qkv, qkv)


def _compress_body(n_chunk, a_ref, w1_ref, w2_ref, pos_ref, gain_ref, o_ref):
    kv = pl.program_id(0)
    a = a_ref[0, 0, 0]
    half = a.shape[1]
    w1 = w1_ref[0]
    const = _dot(pos_ref[0], w1)[0:1, :]
    first = _dot(a, w1[:half, :])
    second = _dot(a, w1[half:, :])
    pre = first + pltpu.roll(second, n_chunk - 1, 0) + const
    hid = pre * jax.nn.sigmoid(pre)
    out = _dot(hid.astype(BF16), w2_ref[0])
    normed = _rms(out, gain_ref[...])
    o_ref[0, 0, 0] = jnp.where(kv == 0, normed, out).astype(BF16)


def _compress(a, w1, w2, pos, gain):
    _, bsz, ng, n_chunk, width = a.shape
    dh = HEAD_DIM
    return pl.pallas_call(
        functools.partial(_compress_body, n_chunk),
        grid=(2, bsz, ng),
        in_specs=[
            pl.BlockSpec((1, 1, 1, n_chunk, width), lambda s, b, g: (s, b, g, 0, 0)),
            pl.BlockSpec((1, 2 * width, dh), lambda s, b, g: (s, 0, 0)),
            pl.BlockSpec((1, dh, dh), lambda s, b, g: (s, 0, 0)),
            pl.BlockSpec((1, 8, 2 * width), lambda s, b, g: (s, 0, 0)),
            pl.BlockSpec((1, dh), lambda s, b, g: (0, 0)),
        ],
        out_specs=pl.BlockSpec((1, 1, 1, n_chunk, dh), lambda s, b, g: (s, b, g, 0, 0)),
        out_shape=jax.ShapeDtypeStruct((2, bsz, ng, n_chunk, dh), BF16),
        compiler_params=_params("parallel", "parallel", "parallel"),
        name="nsa_compress",
    )(a, w1, w2, pos, gain)


def _stack_heads(q_ref, hpg):
    return jnp.concatenate([q_ref[:, h * HEAD_DIM:(h + 1) * HEAD_DIM] for h in range(hpg)], axis=0)


def _gate_column(gates, col):
    lane = lax.broadcasted_iota(jnp.int32, gates.shape, 1)
    return jnp.sum(jnp.where(lane == col, gates, 0.0), axis=1, keepdims=True)


def _nsa_cmp_body(tq, hpg, n_sel, q_ref, kc_ref, vc_ref, gates_ref, o_ref, sel_ref, vt_scr, st_scr):
    g = pl.program_id(1)
    i = pl.program_id(2)
    qs = _stack_heads(q_ref, hpg)
    kc = kc_ref[0, 0, 0]
    vc = vc_ref[0, 0, 0]
    n_chunk = kc.shape[0]
    s = _dot_nt(qs, kc)
    t1 = i * tq + lax.broadcasted_iota(jnp.int32, (tq, n_chunk), 0)
    c1 = lax.broadcasted_iota(jnp.int32, (tq, n_chunk), 1)
    valid1 = c1 * CMP_STRIDE + (CMP_LEN - 1) <= t1
    valid = jnp.concatenate([valid1] * hpg, axis=0)
    s = jnp.where(valid, s, NEG)
    m = jnp.max(s, axis=1, keepdims=True)
    e = jnp.where(valid, jnp.exp2(s - m), 0.0)
    l = jnp.sum(e, axis=1, keepdims=True)
    p = e / jnp.where(l > 0.0, l, 1.0)
    o = _dot(p.astype(BF16), vc)
    gates = gates_ref[...]
    for h in range(hpg):
        gc = _gate_column(gates, 3 * (g * hpg + h) + 0)
        o_ref[:, h * HEAD_DIM:(h + 1) * HEAD_DIM] = (gc * o[h * tq:(h + 1) * tq, :]).astype(BF16)

    p_sum = p[0:tq, :]
    for h in range(1, hpg):
        p_sum = p_sum + p[h * tq:(h + 1) * tq, :]
    ci = lax.broadcasted_iota(jnp.int32, (n_chunk, LANES), 0)
    sj = lax.broadcasted_iota(jnp.int32, (n_chunk, LANES), 1)
    overlap = ((ci * CMP_STRIDE < (sj + 1) * SEL_LEN) & (ci * CMP_STRIDE + CMP_LEN > sj * SEL_LEN)
               & (ci < n_chunk - 1) & (sj < n_sel)).astype(BF16)
    imp = _split_dot(p_sum, overlap)
    t2 = i * tq + lax.broadcasted_iota(jnp.int32, (tq, LANES), 0)
    blk = lax.broadcasted_iota(jnp.int32, (tq, LANES), 1)
    cur = lax.shift_right_logical(t2, SEL_SHIFT)
    forced = (blk == 0) | (blk == cur) | (blk == cur - 1)
    blk_valid = blk * SEL_LEN <= t2
    score = jnp.where(forced, BIG, jnp.where(blk_valid, imp, -BIG))

    vt_scr[...] = score.T
    vt = vt_scr[0:n_sel, :]
    jrow = lax.broadcasted_iota(jnp.int32, (n_sel, tq), 0)
    cnt = jnp.zeros((n_sel, tq), F32)
    for b in range(n_sel):
        vb = vt_scr[b:b + 1, :]
        ahead = (vb > vt) | ((vb == vt) & (jrow > b))
        cnt = cnt + ahead.astype(F32)
    st_scr[...] = jnp.zeros_like(st_scr)
    st_scr[0:n_sel, :] = (cnt < float(min(SEL_TOPK, n_sel))).astype(F32)
    sel_ref[0, 0] = st_scr[...].T.astype(BF16)


def _nsa_compressed(qkv, kvc, gates, bsz, seq, ng, hpg, tq):
    T = bsz * seq
    nq = seq // tq
    n_chunk = kvc.shape[3]
    n_sel = seq // SEL_LEN
    assert n_sel <= LANES
    gw = hpg * HEAD_DIM
    return pl.pallas_call(
        functools.partial(_nsa_cmp_body, tq, hpg, n_sel),
        grid=(bsz, ng, nq),
        in_specs=[
            pl.BlockSpec((tq, gw), lambda b, g, i: (b * nq + i, g)),
            pl.BlockSpec((1, 1, 1, n_chunk, HEAD_DIM), lambda b, g, i: (0, b, g, 0, 0)),
            pl.BlockSpec((1, 1, 1, n_chunk, HEAD_DIM), lambda b, g, i: (1, b, g, 0, 0)),
            pl.BlockSpec((tq, LANES), lambda b, g, i: (b * nq + i, 0)),
        ],
        out_specs=[
            pl.BlockSpec((tq, gw), lambda b, g, i: (b * nq + i, g)),
            pl.BlockSpec((1, 1, tq, LANES), lambda b, g, i: (b, g, i, 0)),
        ],
        out_shape=[
            jax.ShapeDtypeStruct((T, ng * gw), BF16),
            jax.ShapeDtypeStruct((bsz, ng, seq, LANES), BF16),
        ],
        scratch_shapes=[pltpu.VMEM((LANES, tq), F32), pltpu.VMEM((LANES, tq), F32)],
        compiler_params=_params("parallel", "parallel", "parallel"),
        name="nsa_compressed",
    )(qkv, kvc, kvc, gates)


def _fill_values_and_ones(vo_scr, v_ref):
    vo_scr[:, :HEAD_DIM] = v_ref[...]
    vo_scr[:, HEAD_DIM:] = jnp.ones((vo_scr.shape[0], LANES), BF16)


def _store_gated(o_ref, gates_ref, out, g, branch, tq, hpg):
    gates = gates_ref[...]
    for h in range(hpg):
        gc = _gate_column(gates, 3 * (g * hpg + h) + branch)
        o_ref[:, h * HEAD_DIM:(h + 1) * HEAD_DIM] = (gc * out[h * tq:(h + 1) * tq, :]).astype(BF16)


def _nsa_sel_body(tq, tk, hpg, q_ref, k_ref, v_ref, gates_ref, sel_ref, o_ref, m_scr, acc_scr, vo_scr):
    g = pl.program_id(1)
    i = pl.program_id(2)
    dh = HEAD_DIM

    @pl.when(i == 0)
    def _():
        _fill_values_and_ones(vo_scr, v_ref)

    qs = _stack_heads(q_ref, hpg)
    m_scr[...] = jnp.full_like(m_scr, NEG)
    acc_scr[...] = jnp.zeros_like(acc_scr)
    qpos = i * tq + lax.broadcasted_iota(jnp.int32, (tq, tk), 0)
    kcol = lax.broadcasted_iota(jnp.int32, (tq, tk), 1)
    sel = sel_ref[0, 0]
    eb = lax.broadcasted_iota(jnp.int32, (LANES, tk), 0)
    ek = lax.broadcasted_iota(jnp.int32, (LANES, tk), 1)
    n_tiles = ((i + 1) * tq + tk - 1) // tk

    def scores(j):
        k = k_ref[pl.ds(pl.multiple_of(j * tk, tk), tk), :]
        expand = (eb == lax.shift_right_logical(j * tk + ek, SEL_SHIFT)).astype(BF16)
        mask1 = (_dot(sel, expand) > 0.5) & (j * tk + kcol <= qpos)
        bias1 = jnp.where(mask1, 0.0, NEG)
        return _dot_nt(qs, k) + jnp.concatenate([bias1] * hpg, axis=0)

    def step(n, s):
        j = n_tiles - 1 - n
        s_next = scores(jnp.maximum(j - 1, 0))
        vo = vo_scr[pl.ds(pl.multiple_of(j * tk, tk), tk), :]
        chunks = [s[:, c * LANES:(c + 1) * LANES] for c in range(tk // LANES)]
        mx = chunks[0]
        for ch in chunks[1:]:
            mx = jnp.maximum(mx, ch)
        m_old = m_scr[...]
        m_new = jnp.maximum(m_old, jnp.max(mx, axis=1, keepdims=True))
        alpha = jnp.exp2(m_old - m_new)
        p = jnp.concatenate([jnp.exp2(ch - m_new).astype(BF16) for ch in chunks], axis=1)
        pv = _dot(p, vo)
        acc_scr[:, :dh] = alpha * acc_scr[:, :dh] + pv[:, :dh]
        acc_scr[:, dh:] = alpha * acc_scr[:, dh:] + pv[:, dh:]
        m_scr[...] = m_new
        return s_next

    lax.fori_loop(0, n_tiles, step, scores(n_tiles - 1))
    _store_gated(o_ref, gates_ref, acc_scr[:, :dh] / acc_scr[:, dh:], g, 1, tq, hpg)


def _nsa_win_body(tq, hpg, q_ref, k_ref, v_ref, gates_ref, o_ref, vo_scr):
    g = pl.program_id(1)
    i = pl.program_id(2)
    dh = HEAD_DIM
    span = WINDOW + tq

    @pl.when(i == 0)
    def _():
        _fill_values_and_ones(vo_scr, v_ref)

    qs = _stack_heads(q_ref, hpg)
    off = pl.multiple_of(jnp.maximum(i * tq - WINDOW, 0), tq)
    k = k_ref[pl.ds(off, span), :]
    vo = vo_scr[pl.ds(off, span), :]
    qpos = i * tq + lax.broadcasted_iota(jnp.int32, (tq, span), 0)
    kpos = off + lax.broadcasted_iota(jnp.int32, (tq, span), 1)
    bias1 = jnp.where((kpos <= qpos) & (qpos - kpos < WINDOW), 0.0, NEG)
    s = _dot_nt(qs, k) + jnp.concatenate([bias1] * hpg, axis=0)
    p = jnp.exp2(s - jnp.max(s, axis=1, keepdims=True)).astype(BF16)
    pv = _dot(p, vo)
    _store_gated(o_ref, gates_ref, pv[:, :dh] / pv[:, dh:], g, 2, tq, hpg)


def _nsa_attend(mode, qkv, gates, sel, k_col, v_col, bsz, seq, ng, hpg, tq, tk):
    T = bsz * seq
    nq = seq // tq
    gw = hpg * HEAD_DIM
    rows = hpg * tq
    in_specs = [
        pl.BlockSpec((tq, gw), lambda b, g, i: (b * nq + i, g)),
        pl.BlockSpec((seq, HEAD_DIM), lambda b, g, i: (b, k_col + g)),
        pl.BlockSpec((seq, HEAD_DIM), lambda b, g, i: (b, v_col + g)),
        pl.BlockSpec((tq, LANES), lambda b, g, i: (b * nq + i, 0)),
    ]
    args = [qkv, qkv, qkv, gates]
    scratch = [pltpu.VMEM((seq, HEAD_DIM + LANES), BF16)]
    if mode == "sel":
        in_specs.append(pl.BlockSpec((1, 1, tq, LANES), lambda b, g, i: (b, g, i, 0)))
        args.append(sel)
        scratch = [pltpu.VMEM((rows, LANES), F32), pltpu.VMEM((rows, HEAD_DIM + LANES), F32)] + scratch
        body = functools.partial(_nsa_sel_body, tq, tk, hpg)
    else:
        assert seq >= WINDOW + tq and WINDOW % tq == 0
        body = functools.partial(_nsa_win_body, tq, hpg)
    return pl.pallas_call(
        body,
        grid=(bsz, ng, nq),
        in_specs=in_specs,
        out_specs=pl.BlockSpec((tq, gw), lambda b, g, i: (b * nq + i, g)),
        out_shape=jax.ShapeDtypeStruct((T, ng * gw), BF16),
        scratch_shapes=scratch,
        compiler_params=_params("parallel", "parallel", "arbitrary"),
        name="nsa_" + mode,
    )(*args)


def _sum3_prologue(a_ref, b_ref, c_ref):
    return (a_ref[...].astype(F32) + b_ref[...].astype(F32) + c_ref[...].astype(F32)).astype(BF16)


def _ident_prologue(a_ref):
    return a_ref[...]


TM_PROJ = 1024
TN_PROJ = 512
TM_OUT = 256
TM_MLP = 1024
TF_MLP = 512
TM_POOL = 256
TQ_SB = 128
SB_CHAINS = 8
TQ_NSA = 128
TK_SEL = 512


def _tile_gains(per_tile, tn):
    rows = [jnp.tile(g.astype(F32), tn // HEAD_DIM) if g is not None else jnp.ones((tn,), F32) for g in per_tile]
    return jnp.stack(rows).reshape(len(per_tile), 1, tn)


def _conv_layer(x, seq, norm_g, w_in, conv_w, w_out):
    n_tiles = w_in.shape[1] // TN_PROJ
    bcv = _norm_proj(x, norm_g, w_in.astype(BF16), _tile_gains([None] * n_tiles, TN_PROJ),
                     ["plain"] * n_tiles, TM_PROJ, TN_PROJ)
    return _conv_out(bcv, conv_w, w_out.astype(BF16), x, seq, TM_OUT)


def _nsa_layer(x, bsz, seq, norm_g, w_in, q_gain, k_gain, cmp_pos, cmp_w1, cmp_w2, w_out):
    T, D = x.shape
    dh, G = HEAD_DIM, NSA_KV_GROUPS
    H = D // dh
    hpg = H // G
    gw = hpg * dh
    assert gw == TN_PROJ and G * dh == TN_PROJ
    scale = LOG2E * dh ** -0.5
    width = w_in.shape[1]
    n_tiles = -(-width // TN_PROJ)
    w_pad = jnp.pad(w_in, ((0, 0), (0, n_tiles * TN_PROJ - width))).astype(BF16)
    modes = ["norm"] * G + ["plain", "plain", "norm", "plain", "norm", "plain", "gate"]
    gains = _tile_gains([q_gain * scale] * G + [None, None, k_gain[1], None, k_gain[2], None, None], TN_PROJ)
    qkv, gates = _norm_proj(x, norm_g, w_pad, gains, modes, TM_PROJ, TN_PROJ)

    n_chunk = seq // CMP_STRIDE
    kvc_in = qkv[:, H * dh:H * dh + 2 * G * dh].reshape(bsz, n_chunk, CMP_STRIDE, 2, G, dh)
    kvc_in = kvc_in.transpose(3, 0, 4, 1, 2, 5).reshape(2, bsz, G, n_chunk, CMP_STRIDE * dh)
    pos = jnp.broadcast_to(cmp_pos.reshape(2, 1, CMP_LEN * dh), (2, 8, CMP_LEN * dh)).astype(BF16)
    kvc = _compress(kvc_in, cmp_w1.astype(BF16), cmp_w2.astype(BF16), pos, k_gain[0].reshape(1, dh))

    col = lambda t: t * (TN_PROJ // dh)
    o_cmp, sel = _nsa_compressed(qkv, kvc, gates, bsz, seq, G, hpg, TQ_NSA)
    o_sel = _nsa_attend("sel", qkv, gates, sel, col(G + 2), col(G + 3), bsz, seq, G, hpg, TQ_NSA, TK_SEL)
    o_win = _nsa_attend("win", qkv, gates, None, col(G + 4), col(G + 5), bsz, seq, G, hpg, TQ_NSA, None)
    spec = pl.BlockSpec((TM_OUT, H * dh), lambda i: (i, 0))
    return _res_proj([o_cmp, o_sel, o_win], [spec] * 3, _sum3_prologue, w_out.astype(BF16), x, TM_OUT, "nsa_out")


def _sb_layer(x, bsz, seq, norm_g, w_in, q_gain, k_gain, w_out):
    T, D = x.shape
    dh = HEAD_DIM
    H = D // dh
    scale = LOG2E * dh ** -0.5
    per = (H * dh) // TN_PROJ
    modes = ["norm"] * (2 * per) + ["plain"] * per
    gains = _tile_gains([q_gain * scale] * per + [k_gain] * per + [None] * per, TN_PROJ)
    qkv = _norm_proj(x, norm_g, w_in.astype(BF16), gains, modes, TM_PROJ, TN_PROJ)
    o = _sb_attention(qkv, bsz, seq, H, TQ_SB, SB_CHAINS)
    spec = pl.BlockSpec((TM_OUT, H * dh), lambda i: (i, 0))
    return _res_proj([o], [spec], _ident_prologue, w_out.astype(BF16), x, TM_OUT, "sb_out")


def kernel(x, mix_norm, mlp_norm, mlp_w1, mlp_w2, conv_w_in, conv_w, conv_w_out, nsa_w_in, nsa_q_gain, nsa_k_gain, nsa_cmp_pos, nsa_cmp_w1, nsa_cmp_w2, nsa_w_out, pool_w, pool_scale, sb_w_in, sb_q_gain, sb_k_gain, sb_w_out):
    bsz, seq, d = x.shape
    depth = mix_norm.shape[0]
    n_mixers = 4
    xf = x.reshape(bsz * seq, d)
    mlp_w1 = mlp_w1.astype(BF16)
    mlp_w2 = mlp_w2.astype(BF16)
    for i in range(depth):
        kind, j = i % n_mixers, i // n_mixers
        if kind == 0:
            xf = _conv_layer(xf, seq, mix_norm[i], conv_w_in[j], conv_w[j], conv_w_out[j])
        elif kind == 1:
            xf = _nsa_layer(xf, bsz, seq, mix_norm[i], nsa_w_in[j], nsa_q_gain[j], nsa_k_gain[j],
                            nsa_cmp_pos[j], nsa_cmp_w1[j], nsa_cmp_w2[j], nsa_w_out[j])
        elif kind == 2:
            xf = _pool_mixer(xf, mix_norm[i], pool_w[j].astype(BF16), pool_scale[j], seq, TM_POOL)
        else:
            xf = _sb_layer(xf, bsz, seq, mix_norm[i], sb_w_in[j], sb_q_gain[j], sb_k_gain[j], sb_w_out[j])
        xf = _mlp(xf, mlp_norm[i], mlp_w1, mlp_w2, i, TM_MLP, TF_MLP)
    return xf.reshape(bsz, seq, d)
```

```python
import functools

import jax
import jax.numpy as jnp
from jax import lax
from jax.experimental import pallas as pl
from jax.experimental.pallas import tpu as pltpu

F32 = jnp.float32
BF16 = jnp.bfloat16

HEAD_DIM = 128
EPS = 1e-6
NEG = -1e30
BIG = 1e4
CONV_WIDTH = 3
NSA_KV_GROUPS = 4
CMP_LEN = 32
CMP_STRIDE = 16
SEL_LEN = 64
SEL_SHIFT = SEL_LEN.bit_length() - 1
assert 1 << SEL_SHIFT == SEL_LEN
SEL_TOPK = 16
WINDOW = 512
POOL_WINDOWS = (2, 4, 8, 16)
LANES = 128
VMEM_LIMIT = 56 * 1024 * 1024


def _params(*sem):
    return pltpu.CompilerParams(dimension_semantics=sem, vmem_limit_bytes=VMEM_LIMIT)


def _rms(xv, g):
    ms = jnp.mean(xv * xv, axis=-1, keepdims=True)
    return xv * lax.rsqrt(ms + EPS) * g


def _dot(a, b):
    return jnp.dot(a, b, preferred_element_type=F32)


def _dot_nt(a, b):
    return lax.dot_general(a, b, (((1,), (1,)), ((), ())), preferred_element_type=F32)


def _split_dot(a, b):
    hi = a.astype(BF16)
    r1 = a - hi.astype(F32)
    mid = r1.astype(BF16)
    lo = (r1 - mid.astype(F32)).astype(BF16)
    return _dot(hi, b) + _dot(mid, b) + _dot(lo, b)


def _norm_proj_body(modes, tn, has_gate, x_ref, g_ref, w_ref, gain_ref, *rest):
    if has_gate:
        o_ref, og_ref, h_scr = rest
    else:
        o_ref, h_scr = rest
    j = pl.program_id(1)

    @pl.when(j == 0)
    def _():
        h_scr[...] = _rms(x_ref[...], g_ref[...]).astype(BF16)

    acc = _dot(h_scr[...], w_ref[...])

    def tiles_of(mode):
        return [t for t, m in enumerate(modes) if m == mode]

    def any_of(tiles):
        c = j == tiles[0]
        for t in tiles[1:]:
            c = c | (j == t)
        return c

    if tiles_of("plain"):
        @pl.when(any_of(tiles_of("plain")))
        def _():
            o_ref[...] = acc.astype(BF16)

    if tiles_of("norm"):
        @pl.when(any_of(tiles_of("norm")))
        def _():
            gain = gain_ref[0]
            for h in range(tn // HEAD_DIM):
                sl = slice(h * HEAD_DIM, (h + 1) * HEAD_DIM)
                o_ref[:, sl] = _rms(acc[:, sl], gain[:, sl]).astype(BF16)

    if tiles_of("gate"):
        @pl.when(any_of(tiles_of("gate")))
        def _():
            o_ref[...] = acc.astype(BF16)
            og_ref[...] = jax.nn.sigmoid(acc[:, :LANES])


def _norm_proj(x, g, w, gains, modes, tm, tn):
    T, D = x.shape
    N = w.shape[1]
    assert N == len(modes) * tn and T % tm == 0
    has_gate = "gate" in modes
    out_shape = [jax.ShapeDtypeStruct((T, N), BF16)]
    out_specs = [pl.BlockSpec((tm, tn), lambda i, j: (i, j))]
    if has_gate:
        out_shape.append(jax.ShapeDtypeStruct((T, LANES), F32))
        out_specs.append(pl.BlockSpec((tm, LANES), lambda i, j: (i, 0)))
    res = pl.pallas_call(
        functools.partial(_norm_proj_body, tuple(modes), tn, has_gate),
        grid=(T // tm, N // tn),
        in_specs=[
            pl.BlockSpec((tm, D), lambda i, j: (i, 0)),
            pl.BlockSpec((1, D), lambda i, j: (0, 0)),
            pl.BlockSpec((D, tn), lambda i, j: (0, j)),
            pl.BlockSpec((1, 1, tn), lambda i, j: (j, 0, 0)),
        ],
        out_specs=out_specs,
        out_shape=out_shape,
        scratch_shapes=[pltpu.VMEM((tm, D), BF16)],
        compiler_params=_params("parallel", "arbitrary"),
        name="norm_proj",
    )(x, g.reshape(1, D), w, gains)
    return res if has_gate else res[0]


def _res_proj_body(prologue, n_rows, *refs):
    row_refs = refs[:n_rows]
    w_ref, x_ref, o_ref = refs[n_rows:n_rows + 3]
    a = prologue(*row_refs)
    o_ref[...] = x_ref[...] + _dot(a, w_ref[...])


def _res_proj(rows, row_specs, prologue, w, x, tm, name):
    T, D = x.shape
    K = w.shape[0]
    return pl.pallas_call(
        functools.partial(_res_proj_body, prologue, len(rows)),
        grid=(T // tm,),
        in_specs=list(row_specs) + [
            pl.BlockSpec((K, D), lambda i: (0, 0)),
            pl.BlockSpec((tm, D), lambda i: (i, 0)),
        ],
        out_specs=pl.BlockSpec((tm, D), lambda i: (i, 0)),
        out_shape=jax.ShapeDtypeStruct((T, D), F32),
        compiler_params=_params("parallel"),
        name=name,
    )(*rows, w, x)


def _mlp_body(x_ref, g_ref, w1_ref, w2_ref, o_ref, h_scr):
    f = pl.program_id(1)

    @pl.when(f == 0)
    def _():
        xv = x_ref[...]
        h_scr[...] = _rms(xv, g_ref[...]).astype(BF16)
        o_ref[...] = xv

    a = jnp.maximum(_dot(h_scr[...], w1_ref[...]), 0.0)
    o_ref[...] += _dot((a * a).astype(BF16), w2_ref[...])


def _mlp(x, g, w1, w2, layer, tm, tf):
    T, D = x.shape
    FF = w1.shape[2]
    return pl.pallas_call(
        _mlp_body,
        grid=(T // tm, FF // tf),
        in_specs=[
            pl.BlockSpec((tm, D), lambda i, f: (i, 0)),
            pl.BlockSpec((1, D), lambda i, f: (0, 0)),
            pl.BlockSpec((None, D, tf), lambda i, f: (layer, 0, f)),
            pl.BlockSpec((None, tf, D), lambda i, f: (layer, f, 0)),
        ],
        out_specs=pl.BlockSpec((tm, D), lambda i, f: (i, 0)),
        out_shape=jax.ShapeDtypeStruct((T, D), F32),
        scratch_shapes=[pltpu.VMEM((tm, D), BF16)],
        compiler_params=_params("parallel", "arbitrary"),
        name="mlp",
    )(x, g.reshape(1, D), w1, w2)


HALO = 16


def _conv_prologue(tm, tiles_per_seq, b_ref, c_ref, v_ref, cp_ref, vp_ref, cw_ref):
    i = pl.program_id(0)
    first = (i % tiles_per_seq) == 0
    u = c_ref[...].astype(F32) * v_ref[...].astype(F32)
    up = cp_ref[...].astype(F32) * vp_ref[...].astype(F32)
    up = jnp.where(first, 0.0, up)
    row = lax.broadcasted_iota(jnp.int32, u.shape, 0)
    r1 = jnp.where(row == 0, up[HALO - 1:HALO, :], pltpu.roll(u, 1, 0))
    r2 = pltpu.roll(u, 2, 0)
    r2 = jnp.where(row == 0, up[HALO - 2:HALO - 1, :], jnp.where(row == 1, up[HALO - 1:HALO, :], r2))
    cw = cw_ref[...]
    y = cw[0:1, :] * r2 + cw[1:2, :] * r1 + cw[2:3, :] * u
    return (b_ref[...].astype(F32) * y).astype(BF16)


def _conv_out(bcv, conv_w, w_out, x, seq, tm):
    T, D = x.shape
    hb = tm // HALO
    prev = lambda col: (lambda i: (jnp.maximum(i * hb - 1, 0), col))
    specs = [
        pl.BlockSpec((tm, D), lambda i: (i, 0)),
        pl.BlockSpec((tm, D), lambda i: (i, 1)),
        pl.BlockSpec((tm, D), lambda i: (i, 2)),
        pl.BlockSpec((HALO, D), prev(1)),
        pl.BlockSpec((HALO, D), prev(2)),
        pl.BlockSpec((CONV_WIDTH, D), lambda i: (0, 0)),
    ]
    prologue = functools.partial(_conv_prologue, tm, seq // tm)
    return _res_proj([bcv, bcv, bcv, bcv, bcv, conv_w], specs, prologue, w_out, x, tm, "conv_out")


def _pool_body(tm, tiles_per_seq, x_ref, xp_ref, g_ref, w_ref, sc_ref, o_ref):
    i = pl.program_id(0)
    first = (i % tiles_per_seq) == 0
    xv = x_ref[...]
    g = g_ref[...]
    h = _rms(xv, g)
    hp = jnp.where(first, 0.0, _rms(xp_ref[...], g))
    pos = (i % tiles_per_seq) * tm + lax.broadcasted_iota(jnp.int32, (tm, 1), 0)
    cg = h.shape[1] // len(POOL_WINDOWS)
    for gi, win in enumerate(POOL_WINDOWS):
        sl = slice(gi * cg, (gi + 1) * cg)
        hg = h[:, sl]
        s = jnp.concatenate([hp[:, sl], hg], axis=0)
        k = 1
        while k < win:
            s = s + pltpu.roll(s, k, 0)
            k *= 2
        cnt = jnp.minimum(pos + 1, win).astype(F32)
        pooled = s[HALO:, :] / cnt - hg
        y = _dot(pooled.astype(BF16), w_ref[gi])
        o_ref[:, sl] = xv[:, sl] + y * sc_ref[:, sl]


def _pool_mixer(x, g, w, scale, seq, tm):
    T, D = x.shape
    ng, cg, _ = w.shape
    hb = tm // HALO
    return pl.pallas_call(
        functools.partial(_pool_body, tm, seq // tm),
        grid=(T // tm,),
        in_specs=[
            pl.BlockSpec((tm, D), lambda i: (i, 0)),
            pl.BlockSpec((HALO, D), lambda i: (jnp.maximum(i * hb - 1, 0), 0)),
            pl.BlockSpec((1, D), lambda i: (0, 0)),
            pl.BlockSpec((ng, cg, cg), lambda i: (0, 0, 0)),
            pl.BlockSpec((1, D), lambda i: (0, 0)),
        ],
        out_specs=pl.BlockSpec((tm, D), lambda i: (i, 0)),
        out_shape=jax.ShapeDtypeStruct((T, D), F32),
        compiler_params=_params("parallel"),
        name="pool_mixer",
    )(x, x, g.reshape(1, D), w, scale.reshape(1, D))


LOG2E = 1.4426950408889634
SB_STOP_LOG2 = -151.0


def _sb_body(tq, n_chain, q_ref, k_ref, v_ref, o_ref):
    nq = q_ref.shape[0] // tq
    row = lax.broadcasted_iota(jnp.int32, (tq, tq), 0)
    col = lax.broadcasted_iota(jnp.int32, (tq, tq), 1)
    before_diag = col < row
    r2 = lax.broadcasted_iota(jnp.int32, (2 * tq, 2 * tq), 0)
    c2 = lax.broadcasted_iota(jnp.int32, (2 * tq, 2 * tq), 1)
    key = jnp.where(r2 >= tq, r2 - tq, r2)
    sums = ((c2 >= tq) | (key > c2)).astype(BF16)

    def tiles(qs, js, cs, accs, diag, lives=None):
        chains = range(n_chain)
        offs = [pl.multiple_of(js[r] * tq, tq) for r in chains]
        zs = [_dot_nt(qs[r], k_ref[pl.ds(offs[r], tq), :]) for r in chains]
        log_1m = [jnp.log(1.0 + jnp.exp2(-jnp.abs(z))) * (-LOG2E) - jnp.maximum(z, 0.0) for z in zs]
        if diag:
            log_1m = [jnp.where(before_diag, x, 0.0) for x in log_1m]
        his = [x.astype(BF16) for x in log_1m]
        mids = [(x - hi.astype(F32)).astype(BF16) for x, hi in zip(log_1m, his)]
        ts = [_dot(jnp.concatenate([hi, mid], axis=1), sums) for hi, mid in zip(his, mids)]
        ws = [jnp.exp2(zs[r] + log_1m[r] + ts[r][:, :tq] + cs[r]) for r in chains]
        tile_sums = [t[:, tq:] for t in ts]
        if diag:
            ws = [jnp.where(before_diag, w, 0.0) for w in ws]
        if lives is not None:
            ws = [jnp.where(lives[r], ws[r], 0.0) for r in chains]
            tile_sums = [jnp.where(lives[r], tile_sums[r], 0.0) for r in chains]
        accs = tuple(accs[r] + _dot(ws[r].astype(BF16), v_ref[pl.ds(offs[r], tq), :]) for r in chains)
        return tuple(cs[r] + tile_sums[r] for r in chains), accs

    def q_group(gi, carry):
        base = gi * n_chain
        chains = range(n_chain)
        qs = [q_ref[pl.ds(pl.multiple_of((base + r) * tq, tq), tq), :] for r in chains]
        zero = (jnp.zeros((tq, tq), F32),) * n_chain
        cs, accs = tiles(qs, [base + r for r in chains], zero, zero, True)

        def live_max(cs, n):
            vals = [jnp.where(base + r - n >= 0, jnp.max(cs[r]), -jnp.inf) for r in chains]
            return functools.reduce(jnp.maximum, vals)

        def cond(st):
            n, _, _, cmax = st
            return (n <= base + n_chain - 1) & (cmax > SB_STOP_LOG2)

        def body(st):
            n, cs, accs, _ = st
            js = [base + r - n for r in chains]
            cs, accs = tiles(qs, [jnp.maximum(j, 0) for j in js], cs, accs, False, [j >= 0 for j in js])
            return n + 1, cs, accs, live_max(cs, n + 1)

        _, _, accs, _ = lax.while_loop(cond, body, (1, cs, accs, live_max(cs, 1)))
        for r in chains:
            o_ref[pl.ds(pl.multiple_of((base + r) * tq, tq), tq), :] = accs[r].astype(BF16)
        return carry

    lax.fori_loop(0, nq // n_chain, q_group, 0)


def _sb_attention(qkv, bsz, seq, n_heads, tq, n_chain):
    assert tq == HEAD_DIM == LANES
    assert seq % (tq * n_chain) == 0
    T = bsz * seq
    return pl.pallas_call(
        functools.partial(_sb_body, tq, n_chain),
        grid=(bsz, n_heads),
        in_specs=[
            pl.BlockSpec((seq, HEAD_DIM), lambda b, h: (b, h)),
            pl.BlockSpec((seq, HEAD_DIM), lambda b, h: (b, n_heads + h)),
            pl.BlockSpec((seq, HEAD_DIM), lambda b, h: (b, 2 * n_heads + h)),
        ],
        out_specs=pl.BlockSpec((seq, HEAD_DIM), lambda b, h: (b, h)),
        out_shape=jax.ShapeDtypeStruct((T, n_heads * HEAD_DIM), BF16),
        compiler_params=_params("parallel", "parallel"),
        name="sb_attention",
    )(qkv, qkv, qkv)


def _compress_body(n_chunk, a_ref, w1_ref, w2_ref, pos_ref, gain_ref, o_ref):
    kv = pl.program_id(0)
    a = a_ref[0, 0, 0]
    half = a.shape[1]
    w1 = w1_ref[0]
    const = _dot(pos_ref[0], w1)[0:1, :]
    first = _dot(a, w1[:half, :])
    second = _dot(a, w1[half:, :])
    pre = first + pltpu.roll(second, n_chunk - 1, 0) + const
    hid = pre * jax.nn.sigmoid(pre)
    out = _dot(hid.astype(BF16), w2_ref[0])
    normed = _rms(out, gain_ref[...])
    o_ref[0, 0, 0] = jnp.where(kv == 0, normed, out).astype(BF16)


def _compress(a, w1, w2, pos, gain):
    _, bsz, ng, n_chunk, width = a.shape
    dh = HEAD_DIM
    return pl.pallas_call(
        functools.partial(_compress_body, n_chunk),
        grid=(2, bsz, ng),
        in_specs=[
            pl.BlockSpec((1, 1, 1, n_chunk, width), lambda s, b, g: (s, b, g, 0, 0)),
            pl.BlockSpec((1, 2 * width, dh), lambda s, b, g: (s, 0, 0)),
            pl.BlockSpec((1, dh, dh), lambda s, b, g: (s, 0, 0)),
            pl.BlockSpec((1, 8, 2 * width), lambda s, b, g: (s, 0, 0)),
            pl.BlockSpec((1, dh), lambda s, b, g: (0, 0)),
        ],
        out_specs=pl.BlockSpec((1, 1, 1, n_chunk, dh), lambda s, b, g: (s, b, g, 0, 0)),
        out_shape=jax.ShapeDtypeStruct((2, bsz, ng, n_chunk, dh), BF16),
        compiler_params=_params("parallel", "parallel", "parallel"),
        name="nsa_compress",
    )(a, w1, w2, pos, gain)


def _stack_heads(q_ref, hpg):
    return jnp.concatenate([q_ref[:, h * HEAD_DIM:(h + 1) * HEAD_DIM] for h in range(hpg)], axis=0)


def _gate_column(gates, col):
    lane = lax.broadcasted_iota(jnp.int32, gates.shape, 1)
    return jnp.sum(jnp.where(lane == col, gates, 0.0), axis=1, keepdims=True)


def _cmp_branch(qs, kc, vc, i, tq, hpg, n_sel, vt_scr, st_scr):
    n_chunk = kc.shape[0]
    s = _dot_nt(qs, kc)
    t1 = i * tq + lax.broadcasted_iota(jnp.int32, (tq, n_chunk), 0)
    c1 = lax.broadcasted_iota(jnp.int32, (tq, n_chunk), 1)
    valid1 = c1 * CMP_STRIDE + (CMP_LEN - 1) <= t1
    valid = jnp.concatenate([valid1] * hpg, axis=0)
    s = jnp.where(valid, s, NEG)
    m = jnp.max(s, axis=1, keepdims=True)
    e = jnp.where(valid, jnp.exp2(s - m), 0.0)
    l = jnp.sum(e, axis=1, keepdims=True)
    p = e / jnp.where(l > 0.0, l, 1.0)
    o = _dot(p.astype(BF16), vc)

    p_sum = p[0:tq, :]
    for h in range(1, hpg):
        p_sum = p_sum + p[h * tq:(h + 1) * tq, :]
    ci = lax.broadcasted_iota(jnp.int32, (n_chunk, LANES), 0)
    sj = lax.broadcasted_iota(jnp.int32, (n_chunk, LANES), 1)
    overlap = ((ci * CMP_STRIDE < (sj + 1) * SEL_LEN) & (ci * CMP_STRIDE + CMP_LEN > sj * SEL_LEN)
               & (ci < n_chunk - 1) & (sj < n_sel)).astype(BF16)
    imp = _split_dot(p_sum, overlap)
    t2 = i * tq + lax.broadcasted_iota(jnp.int32, (tq, LANES), 0)
    blk = lax.broadcasted_iota(jnp.int32, (tq, LANES), 1)
    cur = lax.shift_right_logical(t2, SEL_SHIFT)
    forced = (blk == 0) | (blk == cur) | (blk == cur - 1)
    blk_valid = blk * SEL_LEN <= t2
    score = jnp.where(forced, BIG, jnp.where(blk_valid, imp, -BIG))

    vt_scr[...] = score.T
    SUB = 8
    vts = [vt_scr[a * SUB:(a + 1) * SUB, :] for a in range(n_sel // SUB)]
    cnts = [jnp.zeros((SUB, tq), F32) for _ in vts]
    jrow = lax.broadcasted_iota(jnp.int32, (SUB, tq), 0)
    for b in range(n_sel):
        vb = vt_scr[b:b + 1, :]
        for a, vt in enumerate(vts):
            if b < a * SUB:
                ahead = vb >= vt
            elif b >= (a + 1) * SUB:
                ahead = vb > vt
            else:
                ahead = (vb > vt) | ((vb == vt) & (jrow > b - a * SUB))
            cnts[a] = cnts[a] + ahead.astype(F32)
    st_scr[...] = jnp.zeros_like(st_scr)
    for a, cnt in enumerate(cnts):
        st_scr[a * SUB:(a + 1) * SUB, :] = (cnt < float(min(SEL_TOPK, n_sel))).astype(F32)
    return o, st_scr[...].T.astype(BF16)


def _fill_values_and_ones(vo_scr, v_ref):
    vo_scr[:, :HEAD_DIM] = v_ref[...]
    vo_scr[:, HEAD_DIM:] = jnp.ones((vo_scr.shape[0], LANES), BF16)


def _sel_branch(qs, k_ref, vo_scr, sel, i, tq, tk, hpg, m_scr, acc_scr):
    dh = HEAD_DIM
    m_scr[...] = jnp.full_like(m_scr, NEG)
    acc_scr[...] = jnp.zeros_like(acc_scr)
    qpos = i * tq + lax.broadcasted_iota(jnp.int32, (tq, tk), 0)
    kcol = lax.broadcasted_iota(jnp.int32, (tq, tk), 1)
    eb = lax.broadcasted_iota(jnp.int32, (LANES, tk), 0)
    ek = lax.broadcasted_iota(jnp.int32, (LANES, tk), 1)
    n_tiles = ((i + 1) * tq + tk - 1) // tk

    def scores(j):
        k = k_ref[pl.ds(pl.multiple_of(j * tk, tk), tk), :]
        expand = (eb == lax.shift_right_logical(j * tk + ek, SEL_SHIFT)).astype(BF16)
        mask1 = (_dot(sel, expand) > 0.5) & (j * tk + kcol <= qpos)
        bias1 = jnp.where(mask1, 0.0, NEG)
        return _dot_nt(qs, k) + jnp.concatenate([bias1] * hpg, axis=0)

    def step(n, s):
        j = n_tiles - 1 - n
        s_next = scores(jnp.maximum(j - 1, 0))
        vo = vo_scr[pl.ds(pl.multiple_of(j * tk, tk), tk), :]
        chunks = [s[:, c * LANES:(c + 1) * LANES] for c in range(tk // LANES)]
        mx = chunks[0]
        for ch in chunks[1:]:
            mx = jnp.maximum(mx, ch)
        m_old = m_scr[...]
        m_new = jnp.maximum(m_old, jnp.max(mx, axis=1, keepdims=True))
        alpha = jnp.exp2(m_old - m_new)
        p = jnp.concatenate([jnp.exp2(ch - m_new).astype(BF16) for ch in chunks], axis=1)
        pv = _dot(p, vo)
        acc_scr[:, :dh] = alpha * acc_scr[:, :dh] + pv[:, :dh]
        acc_scr[:, dh:] = alpha * acc_scr[:, dh:] + pv[:, dh:]
        m_scr[...] = m_new
        return s_next

    lax.fori_loop(0, n_tiles, step, scores(n_tiles - 1))
    return acc_scr[:, :dh] / acc_scr[:, dh:]


def _win_branch(qs, k_ref, vo_scr, i, tq, hpg):
    dh = HEAD_DIM
    span = WINDOW + tq
    off = pl.multiple_of(jnp.maximum(i * tq - WINDOW, 0), tq)
    k = k_ref[pl.ds(off, span), :]
    vo = vo_scr[pl.ds(off, span), :]
    qpos = i * tq + lax.broadcasted_iota(jnp.int32, (tq, span), 0)
    kpos = off + lax.broadcasted_iota(jnp.int32, (tq, span), 1)
    bias1 = jnp.where((kpos <= qpos) & (qpos - kpos < WINDOW), 0.0, NEG)
    s = _dot_nt(qs, k) + jnp.concatenate([bias1] * hpg, axis=0)
    p = jnp.exp2(s - jnp.max(s, axis=1, keepdims=True)).astype(BF16)
    pv = _dot(p, vo)
    return pv[:, :dh] / pv[:, dh:]


def _nsa_attn_body(tq, tk, hpg, n_sel, q_ref, kc_ref, vc_ref, ks_ref, vs_ref, kw_ref, vw_ref, gates_ref, o_ref,
                   vos_scr, vow_scr, vt_scr, st_scr, m_scr, acc_scr):
    g = pl.program_id(1)
    i = pl.program_id(2)

    @pl.when(i == 0)
    def _():
        _fill_values_and_ones(vos_scr, vs_ref)
        _fill_values_and_ones(vow_scr, vw_ref)

    qs = _stack_heads(q_ref, hpg)
    o_cmp, sel = _cmp_branch(qs, kc_ref[0, 0, 0], vc_ref[0, 0, 0], i, tq, hpg, n_sel, vt_scr, st_scr)
    o_win = _win_branch(qs, kw_ref, vow_scr, i, tq, hpg)
    o_sel = _sel_branch(qs, ks_ref, vos_scr, sel, i, tq, tk, hpg, m_scr, acc_scr)
    gates = gates_ref[...]
    for h in range(hpg):
        rows = slice(h * tq, (h + 1) * tq)
        col = 3 * (g * hpg + h)
        o = (_gate_column(gates, col) * o_cmp[rows, :] + _gate_column(gates, col + 1) * o_sel[rows, :]
             + _gate_column(gates, col + 2) * o_win[rows, :])
        o_ref[:, h * HEAD_DIM:(h + 1) * HEAD_DIM] = o.astype(BF16)


def _nsa_attention(qkv, kvc, gates, col, bsz, seq, ng, hpg, tq, tk):
    T = bsz * seq
    nq = seq // tq
    gw = hpg * HEAD_DIM
    rows = hpg * tq
    n_chunk = kvc.shape[3]
    n_sel = seq // SEL_LEN
    assert n_sel <= LANES and n_sel % 8 == 0
    assert seq >= WINDOW + tq and WINDOW % tq == 0
    resident = lambda t: pl.BlockSpec((seq, HEAD_DIM), lambda b, g, i: (b, col(t) + g))
    compressed = lambda s: pl.BlockSpec((1, 1, 1, n_chunk, HEAD_DIM), lambda b, g, i: (s, b, g, 0, 0))
    return pl.pallas_call(
        functools.partial(_nsa_attn_body, tq, tk, hpg, n_sel),
        grid=(bsz, ng, nq),
        in_specs=[
            pl.BlockSpec((tq, gw), lambda b, g, i: (b * nq + i, g)),
            compressed(0), compressed(1),
            resident(ng + 2), resident(ng + 3), resident(ng + 4), resident(ng + 5),
            pl.BlockSpec((tq, LANES), lambda b, g, i: (b * nq + i, 0)),
        ],
        out_specs=pl.BlockSpec((tq, gw), lambda b, g, i: (b * nq + i, g)),
        out_shape=jax.ShapeDtypeStruct((T, ng * gw), BF16),
        scratch_shapes=[
            pltpu.VMEM((seq, HEAD_DIM + LANES), BF16), pltpu.VMEM((seq, HEAD_DIM + LANES), BF16),
            pltpu.VMEM((LANES, tq), F32), pltpu.VMEM((LANES, tq), F32),
            pltpu.VMEM((rows, LANES), F32), pltpu.VMEM((rows, HEAD_DIM + LANES), F32),
        ],
        compiler_params=_params("parallel", "parallel", "arbitrary"),
        name="nsa_attention",
    )(qkv, kvc, kvc, qkv, qkv, qkv, qkv, gates)


def _ident_prologue(a_ref):
    return a_ref[...]


TM_PROJ = 1024
TN_PROJ = 512
TM_OUT = 256
TM_MLP = 1024
TF_MLP = 512
TM_POOL = 256
TQ_SB = 128
SB_CHAINS = 8
TQ_NSA = 128
TK_SEL = 512


def _tile_gains(per_tile, tn):
    rows = [jnp.tile(g.astype(F32), tn // HEAD_DIM) if g is not None else jnp.ones((tn,), F32) for g in per_tile]
    return jnp.stack(rows).reshape(len(per_tile), 1, tn)


def _conv_layer(x, seq, norm_g, w_in, conv_w, w_out):
    n_tiles = w_in.shape[1] // TN_PROJ
    bcv = _norm_proj(x, norm_g, w_in.astype(BF16), _tile_gains([None] * n_tiles, TN_PROJ),
                     ["plain"] * n_tiles, TM_PROJ, TN_PROJ)
    return _conv_out(bcv, conv_w, w_out.astype(BF16), x, seq, TM_OUT)


def _nsa_layer(x, bsz, seq, norm_g, w_in, q_gain, k_gain, cmp_pos, cmp_w1, cmp_w2, w_out):
    T, D = x.shape
    dh, G = HEAD_DIM, NSA_KV_GROUPS
    H = D // dh
    hpg = H // G
    gw = hpg * dh
    assert gw == TN_PROJ and G * dh == TN_PROJ
    scale = LOG2E * dh ** -0.5
    width = w_in.shape[1]
    n_tiles = -(-width // TN_PROJ)
    w_pad = jnp.pad(w_in, ((0, 0), (0, n_tiles * TN_PROJ - width))).astype(BF16)
    modes = ["norm"] * G + ["plain", "plain", "norm", "plain", "norm", "plain", "gate"]
    gains = _tile_gains([q_gain * scale] * G + [None, None, k_gain[1], None, k_gain[2], None, None], TN_PROJ)
    qkv, gates = _norm_proj(x, norm_g, w_pad, gains, modes, TM_PROJ, TN_PROJ)

    n_chunk = seq // CMP_STRIDE
    kvc_in = qkv[:, H * dh:H * dh + 2 * G * dh].reshape(bsz, n_chunk, CMP_STRIDE, 2, G, dh)
    kvc_in = kvc_in.transpose(3, 0, 4, 1, 2, 5).reshape(2, bsz, G, n_chunk, CMP_STRIDE * dh)
    pos = jnp.broadcast_to(cmp_pos.reshape(2, 1, CMP_LEN * dh), (2, 8, CMP_LEN * dh)).astype(BF16)
    kvc = _compress(kvc_in, cmp_w1.astype(BF16), cmp_w2.astype(BF16), pos, k_gain[0].reshape(1, dh))

    col = lambda t: t * (TN_PROJ // dh)
    o = _nsa_attention(qkv, kvc, gates, col, bsz, seq, G, hpg, TQ_NSA, TK_SEL)
    spec = pl.BlockSpec((TM_OUT, H * dh), lambda i: (i, 0))
    return _res_proj([o], [spec], _ident_prologue, w_out.astype(BF16), x, TM_OUT, "nsa_out")


def _sb_layer(x, bsz, seq, norm_g, w_in, q_gain, k_gain, w_out):
    T, D = x.shape
    dh = HEAD_DIM
    H = D // dh
    scale = LOG2E * dh ** -0.5
    per = (H * dh) // TN_PROJ
    modes = ["norm"] * (2 * per) + ["plain"] * per
    gains = _tile_gains([q_gain * scale] * per + [k_gain] * per + [None] * per, TN_PROJ)
    qkv = _norm_proj(x, norm_g, w_in.astype(BF16), gains, modes, TM_PROJ, TN_PROJ)
    o = _sb_attention(qkv, bsz, seq, H, TQ_SB, SB_CHAINS)
    spec = pl.BlockSpec((TM_OUT, H * dh), lambda i: (i, 0))
    return _res_proj([o], [spec], _ident_prologue, w_out.astype(BF16), x, TM_OUT, "sb_out")


def kernel(x, mix_norm, mlp_norm, mlp_w1, mlp_w2, conv_w_in, conv_w, conv_w_out, nsa_w_in, nsa_q_gain, nsa_k_gain, nsa_cmp_pos, nsa_cmp_w1, nsa_cmp_w2, nsa_w_out, pool_w, pool_scale, sb_w_in, sb_q_gain, sb_k_gain, sb_w_out):
    bsz, seq, d = x.shape
    depth = mix_norm.shape[0]
    n_mixers = 4
    xf = x.reshape(bsz * seq, d)
    mlp_w1 = mlp_w1.astype(BF16)
    mlp_w2 = mlp_w2.astype(BF16)
    for i in range(depth):
        kind, j = i % n_mixers, i // n_mixers
        if kind == 0:
            xf = _conv_layer(xf, seq, mix_norm[i], conv_w_in[j], conv_w[j], conv_w_out[j])
        elif kind == 1:
            xf = _nsa_layer(xf, bsz, seq, mix_norm[i], nsa_w_in[j], nsa_q_gain[j], nsa_k_gain[j],
                            nsa_cmp_pos[j], nsa_cmp_w1[j], nsa_cmp_w2[j], nsa_w_out[j])
        elif kind == 2:
            xf = _pool_mixer(xf, mix_norm[i], pool_w[j].astype(BF16), pool_scale[j], seq, TM_POOL)
        else:
            xf = _sb_layer(xf, bsz, seq, mix_norm[i], sb_w_in[j], sb_q_gain[j], sb_k_gain[j], sb_w_out[j])
        xf = _mlp(xf, mlp_norm[i], mlp_w1, mlp_w2, i, TM_MLP, TF_MLP)
    return xf.reshape(bsz, seq, d)
```

```python
import functools

import jax
import jax.numpy as jnp
from jax import lax
from jax.experimental import pallas as pl
from jax.experimental.pallas import tpu as pltpu

F32 = jnp.float32
BF16 = jnp.bfloat16

HEAD_DIM = 128
EPS = 1e-6
NEG = -1e30
BIG = 1e4
CONV_WIDTH = 3
NSA_KV_GROUPS = 4
CMP_LEN = 32
CMP_STRIDE = 16
SEL_LEN = 64
SEL_SHIFT = SEL_LEN.bit_length() - 1
assert 1 << SEL_SHIFT == SEL_LEN
SEL_TOPK = 16
WINDOW = 512
POOL_WINDOWS = (2, 4, 8, 16)
LANES = 128
VMEM_LIMIT = 56 * 1024 * 1024


def _params(*sem):
    return pltpu.CompilerParams(dimension_semantics=sem, vmem_limit_bytes=VMEM_LIMIT)


def _rms(xv, g):
    ms = jnp.mean(xv * xv, axis=-1, keepdims=True)
    return xv * lax.rsqrt(ms + EPS) * g


def _dot(a, b):
    return jnp.dot(a, b, preferred_element_type=F32)


def _dot_nt(a, b):
    return lax.dot_general(a, b, (((1,), (1,)), ((), ())), preferred_element_type=F32)


def _split_dot(a, b):
    hi = a.astype(BF16)
    r1 = a - hi.astype(F32)
    mid = r1.astype(BF16)
    lo = (r1 - mid.astype(F32)).astype(BF16)
    return _dot(hi, b) + _dot(mid, b) + _dot(lo, b)


def _norm_proj_body(modes, tn, has_gate, x_ref, g_ref, w_ref, gain_ref, *rest):
    if has_gate:
        o_ref, og_ref, h_scr = rest
    else:
        o_ref, h_scr = rest
    j = pl.program_id(1)

    @pl.when(j == 0)
    def _():
        h_scr[...] = _rms(x_ref[...], g_ref[...]).astype(BF16)

    acc = _dot(h_scr[...], w_ref[...])

    def tiles_of(mode):
        return [t for t, m in enumerate(modes) if m == mode]

    def any_of(tiles):
        c = j == tiles[0]
        for t in tiles[1:]:
            c = c | (j == t)
        return c

    if tiles_of("plain"):
        @pl.when(any_of(tiles_of("plain")))
        def _():
            o_ref[...] = acc.astype(BF16)

    if tiles_of("norm"):
        @pl.when(any_of(tiles_of("norm")))
        def _():
            gain = gain_ref[0]
            for h in range(tn // HEAD_DIM):
                sl = slice(h * HEAD_DIM, (h + 1) * HEAD_DIM)
                o_ref[:, sl] = _rms(acc[:, sl], gain[:, sl]).astype(BF16)

    if tiles_of("gate"):
        @pl.when(any_of(tiles_of("gate")))
        def _():
            o_ref[...] = acc.astype(BF16)
            og_ref[...] = jax.nn.sigmoid(acc[:, :LANES])


def _norm_proj(x, g, w, gains, modes, tm, tn):
    T, D = x.shape
    N = w.shape[1]
    assert N == len(modes) * tn and T % tm == 0
    has_gate = "gate" in modes
    out_shape = [jax.ShapeDtypeStruct((T, N), BF16)]
    out_specs = [pl.BlockSpec((tm, tn), lambda i, j: (i, j))]
    if has_gate:
        out_shape.append(jax.ShapeDtypeStruct((T, LANES), F32))
        out_specs.append(pl.BlockSpec((tm, LANES), lambda i, j: (i, 0)))
    res = pl.pallas_call(
        functools.partial(_norm_proj_body, tuple(modes), tn, has_gate),
        grid=(T // tm, N // tn),
        in_specs=[
            pl.BlockSpec((tm, D), lambda i, j: (i, 0)),
            pl.BlockSpec((1, D), lambda i, j: (0, 0)),
            pl.BlockSpec((D, tn), lambda i, j: (0, j)),
            pl.BlockSpec((1, 1, tn), lambda i, j: (j, 0, 0)),
        ],
        out_specs=out_specs,
        out_shape=out_shape,
        scratch_shapes=[pltpu.VMEM((tm, D), BF16)],
        compiler_params=_params("parallel", "arbitrary"),
        name="norm_proj",
    )(x, g.reshape(1, D), w, gains)
    return res if has_gate else res[0]


def _res_proj_body(prologue, n_rows, *refs):
    row_refs = refs[:n_rows]
    w_ref, x_ref, o_ref = refs[n_rows:n_rows + 3]
    a = prologue(*row_refs)
    o_ref[...] = x_ref[...] + _dot(a, w_ref[...])


def _res_proj(rows, row_specs, prologue, w, x, tm, name):
    T, D = x.shape
    K = w.shape[0]
    return pl.pallas_call(
        functools.partial(_res_proj_body, prologue, len(rows)),
        grid=(T // tm,),
        in_specs=list(row_specs) + [
            pl.BlockSpec((K, D), lambda i: (0, 0)),
            pl.BlockSpec((tm, D), lambda i: (i, 0)),
        ],
        out_specs=pl.BlockSpec((tm, D), lambda i: (i, 0)),
        out_shape=jax.ShapeDtypeStruct((T, D), F32),
        compiler_params=_params("parallel"),
        name=name,
    )(*rows, w, x)


def _mlp_body(x_ref, g_ref, w1_ref, w2_ref, o_ref, h_scr):
    f = pl.program_id(1)

    @pl.when(f == 0)
    def _():
        xv = x_ref[...]
        h_scr[...] = _rms(xv, g_ref[...]).astype(BF16)
        o_ref[...] = xv

    a = jnp.maximum(_dot(h_scr[...], w1_ref[...].astype(BF16)), 0.0)
    o_ref[...] += _dot((a * a).astype(BF16), w2_ref[...].astype(BF16))


def _mlp(x, g, w1, w2, layer, tm, tf):
    T, D = x.shape
    FF = w1.shape[2]
    return pl.pallas_call(
        _mlp_body,
        grid=(T // tm, FF // tf),
        in_specs=[
            pl.BlockSpec((tm, D), lambda i, f: (i, 0), pipeline_mode=pl.Buffered(1)),
            pl.BlockSpec((1, D), lambda i, f: (0, 0)),
            pl.BlockSpec((None, D, tf), lambda i, f: (layer, 0, f)),
            pl.BlockSpec((None, tf, D), lambda i, f: (layer, f, 0)),
        ],
        out_specs=pl.BlockSpec((tm, D), lambda i, f: (i, 0)),
        out_shape=jax.ShapeDtypeStruct((T, D), F32),
        scratch_shapes=[pltpu.VMEM((tm, D), BF16)],
        compiler_params=_params("parallel", "arbitrary"),
        name="mlp",
    )(x, g.reshape(1, D), w1, w2)


HALO = 16


def _conv_prologue(tm, tiles_per_seq, b_ref, c_ref, v_ref, cp_ref, vp_ref, cw_ref):
    i = pl.program_id(0)
    first = (i % tiles_per_seq) == 0
    u = c_ref[...].astype(F32) * v_ref[...].astype(F32)
    up = cp_ref[...].astype(F32) * vp_ref[...].astype(F32)
    up = jnp.where(first, 0.0, up)
    row = lax.broadcasted_iota(jnp.int32, u.shape, 0)
    r1 = jnp.where(row == 0, up[HALO - 1:HALO, :], pltpu.roll(u, 1, 0))
    r2 = pltpu.roll(u, 2, 0)
    r2 = jnp.where(row == 0, up[HALO - 2:HALO - 1, :], jnp.where(row == 1, up[HALO - 1:HALO, :], r2))
    cw = cw_ref[...]
    y = cw[0:1, :] * r2 + cw[1:2, :] * r1 + cw[2:3, :] * u
    return (b_ref[...].astype(F32) * y).astype(BF16)


def _conv_out(bcv, conv_w, w_out, x, seq, tm):
    T, D = x.shape
    hb = tm // HALO
    prev = lambda col: (lambda i: (jnp.maximum(i * hb - 1, 0), col))
    specs = [
        pl.BlockSpec((tm, D), lambda i: (i, 0)),
        pl.BlockSpec((tm, D), lambda i: (i, 1)),
        pl.BlockSpec((tm, D), lambda i: (i, 2)),
        pl.BlockSpec((HALO, D), prev(1)),
        pl.BlockSpec((HALO, D), prev(2)),
        pl.BlockSpec((CONV_WIDTH, D), lambda i: (0, 0)),
    ]
    prologue = functools.partial(_conv_prologue, tm, seq // tm)
    return _res_proj([bcv, bcv, bcv, bcv, bcv, conv_w], specs, prologue, w_out, x, tm, "conv_out")


def _pool_body(tm, tiles_per_seq, x_ref, xp_ref, g_ref, w_ref, sc_ref, o_ref):
    i = pl.program_id(0)
    first = (i % tiles_per_seq) == 0
    xv = x_ref[...]
    g = g_ref[...]
    h = _rms(xv, g)
    hp = jnp.where(first, 0.0, _rms(xp_ref[...], g))
    pos = (i % tiles_per_seq) * tm + lax.broadcasted_iota(jnp.int32, (tm, 1), 0)
    cg = h.shape[1] // len(POOL_WINDOWS)
    for gi, win in enumerate(POOL_WINDOWS):
        sl = slice(gi * cg, (gi + 1) * cg)
        hg = h[:, sl]
        s = jnp.concatenate([hp[:, sl], hg], axis=0)
        k = 1
        while k < win:
            s = s + pltpu.roll(s, k, 0)
            k *= 2
        cnt = jnp.minimum(pos + 1, win).astype(F32)
        pooled = s[HALO:, :] / cnt - hg
        y = _dot(pooled.astype(BF16), w_ref[gi])
        o_ref[:, sl] = xv[:, sl] + y * sc_ref[:, sl]


def _pool_mixer(x, g, w, scale, seq, tm):
    T, D = x.shape
    ng, cg, _ = w.shape
    hb = tm // HALO
    return pl.pallas_call(
        functools.partial(_pool_body, tm, seq // tm),
        grid=(T // tm,),
        in_specs=[
            pl.BlockSpec((tm, D), lambda i: (i, 0)),
            pl.BlockSpec((HALO, D), lambda i: (jnp.maximum(i * hb - 1, 0), 0)),
            pl.BlockSpec((1, D), lambda i: (0, 0)),
            pl.BlockSpec((ng, cg, cg), lambda i: (0, 0, 0)),
            pl.BlockSpec((1, D), lambda i: (0, 0)),
        ],
        out_specs=pl.BlockSpec((tm, D), lambda i: (i, 0)),
        out_shape=jax.ShapeDtypeStruct((T, D), F32),
        compiler_params=_params("parallel"),
        name="pool_mixer",
    )(x, x, g.reshape(1, D), w, scale.reshape(1, D))


LOG2E = 1.4426950408889634
SB_STOP_LOG2 = -151.0


def _sb_body(tq, n_chain, q_ref, k_ref, v_ref, o_ref):
    nq = q_ref.shape[0] // tq
    row = lax.broadcasted_iota(jnp.int32, (tq, tq), 0)
    col = lax.broadcasted_iota(jnp.int32, (tq, tq), 1)
    before_diag = col < row
    r2 = lax.broadcasted_iota(jnp.int32, (2 * tq, 2 * tq), 0)
    c2 = lax.broadcasted_iota(jnp.int32, (2 * tq, 2 * tq), 1)
    key = jnp.where(r2 >= tq, r2 - tq, r2)
    sums = ((c2 >= tq) | (key > c2)).astype(BF16)

    def tiles(qs, js, cs, accs, diag, lives=None):
        chains = range(n_chain)
        offs = [pl.multiple_of(js[r] * tq, tq) for r in chains]
        zs = [_dot_nt(qs[r], k_ref[pl.ds(offs[r], tq), :]) for r in chains]
        log_1m = [jnp.log(1.0 + jnp.exp2(-jnp.abs(z))) * (-LOG2E) - jnp.maximum(z, 0.0) for z in zs]
        if diag:
            log_1m = [jnp.where(before_diag, x, 0.0) for x in log_1m]
        his = [x.astype(BF16) for x in log_1m]
        mids = [(x - hi.astype(F32)).astype(BF16) for x, hi in zip(log_1m, his)]
        ts = [_dot(jnp.concatenate([hi, mid], axis=1), sums) for hi, mid in zip(his, mids)]
        ws = [jnp.exp2(zs[r] + log_1m[r] + ts[r][:, :tq] + cs[r]) for r in chains]
        tile_sums = [t[:, tq:] for t in ts]
        if diag:
            ws = [jnp.where(before_diag, w, 0.0) for w in ws]
        if lives is not None:
            ws = [jnp.where(lives[r], ws[r], 0.0) for r in chains]
            tile_sums = [jnp.where(lives[r], tile_sums[r], 0.0) for r in chains]
        accs = tuple(accs[r] + _dot(ws[r].astype(BF16), v_ref[pl.ds(offs[r], tq), :]) for r in chains)
        return tuple(cs[r] + tile_sums[r] for r in chains), accs

    def q_group(gi, carry):
        base = gi * n_chain
        chains = range(n_chain)
        qs = [q_ref[pl.ds(pl.multiple_of((base + r) * tq, tq), tq), :] for r in chains]
        zero = (jnp.zeros((tq, tq), F32),) * n_chain
        cs, accs = tiles(qs, [base + r for r in chains], zero, zero, True)

        def live_max(cs, n):
            vals = [jnp.where(base + r - n >= 0, jnp.max(cs[r]), -jnp.inf) for r in chains]
            return functools.reduce(jnp.maximum, vals)

        def cond(st):
            n, _, _, cmax = st
            return (n <= base + n_chain - 1) & (cmax > SB_STOP_LOG2)

        def body(st):
            n, cs, accs, _ = st
            js = [base + r - n for r in chains]
            cs, accs = tiles(qs, [jnp.maximum(j, 0) for j in js], cs, accs, False, [j >= 0 for j in js])
            return n + 1, cs, accs, live_max(cs, n + 1)

        _, _, accs, _ = lax.while_loop(cond, body, (1, cs, accs, live_max(cs, 1)))
        for r in chains:
            o_ref[pl.ds(pl.multiple_of((base + r) * tq, tq), tq), :] = accs[r].astype(BF16)
        return carry

    lax.fori_loop(0, nq // n_chain, q_group, 0)


def _sb_attention(qkv, bsz, seq, n_heads, tq, n_chain):
    assert tq == HEAD_DIM == LANES
    assert seq % (tq * n_chain) == 0
    T = bsz * seq
    return pl.pallas_call(
        functools.partial(_sb_body, tq, n_chain),
        grid=(bsz, n_heads),
        in_specs=[
            pl.BlockSpec((seq, HEAD_DIM), lambda b, h: (b, h)),
            pl.BlockSpec((seq, HEAD_DIM), lambda b, h: (b, n_heads + h)),
            pl.BlockSpec((seq, HEAD_DIM), lambda b, h: (b, 2 * n_heads + h)),
        ],
        out_specs=pl.BlockSpec((seq, HEAD_DIM), lambda b, h: (b, h)),
        out_shape=jax.ShapeDtypeStruct((T, n_heads * HEAD_DIM), BF16),
        compiler_params=_params("parallel", "parallel"),
        name="sb_attention",
    )(qkv, qkv, qkv)


def _compress_body(n_chunk, a_ref, w1_ref, w2_ref, pos_ref, gain_ref, o_ref):
    kv = pl.program_id(0)
    a = a_ref[0, 0, 0]
    half = a.shape[1]
    w1 = w1_ref[0]
    const = _dot(pos_ref[0], w1)[0:1, :]
    first = _dot(a, w1[:half, :])
    second = _dot(a, w1[half:, :])
    pre = first + pltpu.roll(second, n_chunk - 1, 0) + const
    hid = pre * jax.nn.sigmoid(pre)
    out = _dot(hid.astype(BF16), w2_ref[0])
    normed = _rms(out, gain_ref[...])
    o_ref[0, 0, 0] = jnp.where(kv == 0, normed, out).astype(BF16)


def _compress(a, w1, w2, pos, gain):
    _, bsz, ng, n_chunk, width = a.shape
    dh = HEAD_DIM
    return pl.pallas_call(
        functools.partial(_compress_body, n_chunk),
        grid=(2, bsz, ng),
        in_specs=[
            pl.BlockSpec((1, 1, 1, n_chunk, width), lambda s, b, g: (s, b, g, 0, 0)),
            pl.BlockSpec((1, 2 * width, dh), lambda s, b, g: (s, 0, 0)),
            pl.BlockSpec((1, dh, dh), lambda s, b, g: (s, 0, 0)),
            pl.BlockSpec((1, 8, 2 * width), lambda s, b, g: (s, 0, 0)),
            pl.BlockSpec((1, dh), lambda s, b, g: (0, 0)),
        ],
        out_specs=pl.BlockSpec((1, 1, 1, n_chunk, dh), lambda s, b, g: (s, b, g, 0, 0)),
        out_shape=jax.ShapeDtypeStruct((2, bsz, ng, n_chunk, dh), BF16),
        compiler_params=_params("parallel", "parallel", "parallel"),
        name="nsa_compress",
    )(a, w1, w2, pos, gain)


def _stack_heads(q_ref, hpg):
    return jnp.concatenate([q_ref[:, h * HEAD_DIM:(h + 1) * HEAD_DIM] for h in range(hpg)], axis=0)


def _gate_column(gates, col):
    lane = lax.broadcasted_iota(jnp.int32, gates.shape, 1)
    return jnp.sum(jnp.where(lane == col, gates, 0.0), axis=1, keepdims=True)


def _cmp_branch(qs, kc, vc, i, tq, hpg, n_sel, vt_scr, st_scr):
    n_chunk = kc.shape[0]
    s = _dot_nt(qs, kc)
    t1 = i * tq + lax.broadcasted_iota(jnp.int32, (tq, n_chunk), 0)
    c1 = lax.broadcasted_iota(jnp.int32, (tq, n_chunk), 1)
    valid1 = c1 * CMP_STRIDE + (CMP_LEN - 1) <= t1
    valid = jnp.concatenate([valid1] * hpg, axis=0)
    s = jnp.where(valid, s, NEG)
    m = jnp.max(s, axis=1, keepdims=True)
    e = jnp.where(valid, jnp.exp2(s - m), 0.0)
    l = jnp.sum(e, axis=1, keepdims=True)
    p = e / jnp.where(l > 0.0, l, 1.0)
    o = _dot(p.astype(BF16), vc)

    p_sum = p[0:tq, :]
    for h in range(1, hpg):
        p_sum = p_sum + p[h * tq:(h + 1) * tq, :]
    ci = lax.broadcasted_iota(jnp.int32, (n_chunk, LANES), 0)
    sj = lax.broadcasted_iota(jnp.int32, (n_chunk, LANES), 1)
    overlap = ((ci * CMP_STRIDE < (sj + 1) * SEL_LEN) & (ci * CMP_STRIDE + CMP_LEN > sj * SEL_LEN)
               & (ci < n_chunk - 1) & (sj < n_sel)).astype(BF16)
    imp = _split_dot(p_sum, overlap)
    t2 = i * tq + lax.broadcasted_iota(jnp.int32, (tq, LANES), 0)
    blk = lax.broadcasted_iota(jnp.int32, (tq, LANES), 1)
    cur = lax.shift_right_logical(t2, SEL_SHIFT)
    forced = (blk == 0) | (blk == cur) | (blk == cur - 1)
    blk_valid = blk * SEL_LEN <= t2
    score = jnp.where(forced, BIG, jnp.where(blk_valid, imp, -BIG))

    vt_scr[...] = score.T
    SUB = 8
    vts = [vt_scr[a * SUB:(a + 1) * SUB, :] for a in range(n_sel // SUB)]
    cnts = [jnp.zeros((SUB, tq), F32) for _ in vts]
    jrow = lax.broadcasted_iota(jnp.int32, (SUB, tq), 0)
    for b in range(n_sel):
        vb = vt_scr[b:b + 1, :]
        for a, vt in enumerate(vts):
            if b < a * SUB:
                ahead = vb >= vt
            elif b >= (a + 1) * SUB:
                ahead = vb > vt
            else:
                ahead = (vb > vt) | ((vb == vt) & (jrow > b - a * SUB))
            cnts[a] = cnts[a] + ahead.astype(F32)
    st_scr[...] = jnp.zeros_like(st_scr)
    for a, cnt in enumerate(cnts):
        st_scr[a * SUB:(a + 1) * SUB, :] = (cnt < float(min(SEL_TOPK, n_sel))).astype(F32)
    return o, st_scr[...].T.astype(BF16)


def _fill_values_and_ones(vo_scr, v_ref):
    vo_scr[:, :HEAD_DIM] = v_ref[...]
    vo_scr[:, HEAD_DIM:] = jnp.ones((vo_scr.shape[0], LANES), BF16)


def _sel_branch(qs, k_ref, vo_scr, sel, i, tq, tk, hpg, m_scr, acc_scr):
    dh = HEAD_DIM
    m_scr[...] = jnp.full_like(m_scr, NEG)
    acc_scr[...] = jnp.zeros_like(acc_scr)
    qpos = i * tq + lax.broadcasted_iota(jnp.int32, (tq, tk), 0)
    kcol = lax.broadcasted_iota(jnp.int32, (tq, tk), 1)
    eb = lax.broadcasted_iota(jnp.int32, (LANES, tk), 0)
    ek = lax.broadcasted_iota(jnp.int32, (LANES, tk), 1)
    n_tiles = ((i + 1) * tq + tk - 1) // tk

    def scores(j):
        k = k_ref[pl.ds(pl.multiple_of(j * tk, tk), tk), :]
        expand = (eb == lax.shift_right_logical(j * tk + ek, SEL_SHIFT)).astype(BF16)
        mask1 = (_dot(sel, expand) > 0.5) & (j * tk + kcol <= qpos)
        bias1 = jnp.where(mask1, 0.0, NEG)
        return _dot_nt(qs, k) + jnp.concatenate([bias1] * hpg, axis=0)

    def step(n, s):
        j = n_tiles - 1 - n
        s_next = scores(jnp.maximum(j - 1, 0))
        vo = vo_scr[pl.ds(pl.multiple_of(j * tk, tk), tk), :]
        chunks = [s[:, c * LANES:(c + 1) * LANES] for c in range(tk // LANES)]
        mx = chunks[0]
        for ch in chunks[1:]:
            mx = jnp.maximum(mx, ch)
        m_old = m_scr[...]
        m_new = jnp.maximum(m_old, jnp.max(mx, axis=1, keepdims=True))
        alpha = jnp.exp2(m_old - m_new)
        p = jnp.concatenate([jnp.exp2(ch - m_new).astype(BF16) for ch in chunks], axis=1)
        pv = _dot(p, vo)
        acc_scr[:, :dh] = alpha * acc_scr[:, :dh] + pv[:, :dh]
        acc_scr[:, dh:] = alpha * acc_scr[:, dh:] + pv[:, dh:]
        m_scr[...] = m_new
        return s_next

    lax.fori_loop(0, n_tiles, step, scores(n_tiles - 1))
    return acc_scr[:, :dh] / acc_scr[:, dh:]


def _win_branch(qs, k_ref, vo_scr, i, tq, hpg):
    dh = HEAD_DIM
    span = WINDOW + tq
    off = pl.multiple_of(jnp.maximum(i * tq - WINDOW, 0), tq)
    k = k_ref[pl.ds(off, span), :]
    vo = vo_scr[pl.ds(off, span), :]
    qpos = i * tq + lax.broadcasted_iota(jnp.int32, (tq, span), 0)
    kpos = off + lax.broadcasted_iota(jnp.int32, (tq, span), 1)
    bias1 = jnp.where((kpos <= qpos) & (qpos - kpos < WINDOW), 0.0, NEG)
    s = _dot_nt(qs, k) + jnp.concatenate([bias1] * hpg, axis=0)
    p = jnp.exp2(s - jnp.max(s, axis=1, keepdims=True)).astype(BF16)
    pv = _dot(p, vo)
    return pv[:, :dh] / pv[:, dh:]


def _nsa_attn_body(tq, tk, hpg, n_sel, q_ref, kc_ref, vc_ref, ks_ref, vs_ref, kw_ref, vw_ref, gates_ref, o_ref,
                   vos_scr, vow_scr, vt_scr, st_scr, m_scr, acc_scr):
    g = pl.program_id(1)
    i = pl.program_id(2)

    @pl.when(i == 0)
    def _():
        _fill_values_and_ones(vos_scr, vs_ref)
        _fill_values_and_ones(vow_scr, vw_ref)

    qs = _stack_heads(q_ref, hpg)
    o_cmp, sel = _cmp_branch(qs, kc_ref[0, 0, 0], vc_ref[0, 0, 0], i, tq, hpg, n_sel, vt_scr, st_scr)
    o_win = _win_branch(qs, kw_ref, vow_scr, i, tq, hpg)
    o_sel = _sel_branch(qs, ks_ref, vos_scr, sel, i, tq, tk, hpg, m_scr, acc_scr)
    gates = gates_ref[...]
    for h in range(hpg):
        rows = slice(h * tq, (h + 1) * tq)
        col = 3 * (g * hpg + h)
        o = (_gate_column(gates, col) * o_cmp[rows, :] + _gate_column(gates, col + 1) * o_sel[rows, :]
             + _gate_column(gates, col + 2) * o_win[rows, :])
        o_ref[:, h * HEAD_DIM:(h + 1) * HEAD_DIM] = o.astype(BF16)


def _nsa_attention(qkv, kvc, gates, col, bsz, seq, ng, hpg, tq, tk):
    T = bsz * seq
    nq = seq // tq
    gw = hpg * HEAD_DIM
    rows = hpg * tq
    n_chunk = kvc.shape[3]
    n_sel = seq // SEL_LEN
    assert n_sel <= LANES and n_sel % 8 == 0
    assert seq >= WINDOW + tq and WINDOW % tq == 0
    resident = lambda t: pl.BlockSpec((seq, HEAD_DIM), lambda b, g, i: (b, col(t) + g))
    compressed = lambda s: pl.BlockSpec((1, 1, 1, n_chunk, HEAD_DIM), lambda b, g, i: (s, b, g, 0, 0))
    return pl.pallas_call(
        functools.partial(_nsa_attn_body, tq, tk, hpg, n_sel),
        grid=(bsz, ng, nq),
        in_specs=[
            pl.BlockSpec((tq, gw), lambda b, g, i: (b * nq + i, g)),
            compressed(0), compressed(1),
            resident(ng + 2), resident(ng + 3), resident(ng + 4), resident(ng + 5),
            pl.BlockSpec((tq, LANES), lambda b, g, i: (b * nq + i, 0)),
        ],
        out_specs=pl.BlockSpec((tq, gw), lambda b, g, i: (b * nq + i, g)),
        out_shape=jax.ShapeDtypeStruct((T, ng * gw), BF16),
        scratch_shapes=[
            pltpu.VMEM((seq, HEAD_DIM + LANES), BF16), pltpu.VMEM((seq, HEAD_DIM + LANES), BF16),
            pltpu.VMEM((LANES, tq), F32), pltpu.VMEM((LANES, tq), F32),
            pltpu.VMEM((rows, LANES), F32), pltpu.VMEM((rows, HEAD_DIM + LANES), F32),
        ],
        compiler_params=_params("parallel", "parallel", "arbitrary"),
        name="nsa_attention",
    )(qkv, kvc, kvc, qkv, qkv, qkv, qkv, gates)


def _ident_prologue(a_ref):
    return a_ref[...]


TM_PROJ = 1024
TN_PROJ = 512
TM_OUT = 256
TM_MLP = 1024
TF_MLP = 512
TM_POOL = 256
TQ_SB = 128
SB_CHAINS = 8
TQ_NSA = 256
TK_SEL = 512


def _tile_gains(per_tile, tn):
    rows = [jnp.tile(g.astype(F32), tn // HEAD_DIM) if g is not None else jnp.ones((tn,), F32) for g in per_tile]
    return jnp.stack(rows).reshape(len(per_tile), 1, tn)


def _conv_layer(x, seq, norm_g, w_in, conv_w, w_out):
    n_tiles = w_in.shape[1] // TN_PROJ
    bcv = _norm_proj(x, norm_g, w_in.astype(BF16), _tile_gains([None] * n_tiles, TN_PROJ),
                     ["plain"] * n_tiles, TM_PROJ, TN_PROJ)
    return _conv_out(bcv, conv_w, w_out.astype(BF16), x, seq, TM_OUT)


def _nsa_layer(x, bsz, seq, norm_g, w_in, q_gain, k_gain, cmp_pos, cmp_w1, cmp_w2, w_out):
    T, D = x.shape
    dh, G = HEAD_DIM, NSA_KV_GROUPS
    H = D // dh
    hpg = H // G
    gw = hpg * dh
    assert gw == TN_PROJ and G * dh == TN_PROJ
    scale = LOG2E * dh ** -0.5
    width = w_in.shape[1]
    n_tiles = -(-width // TN_PROJ)
    w_pad = jnp.pad(w_in, ((0, 0), (0, n_tiles * TN_PROJ - width))).astype(BF16)
    modes = ["norm"] * G + ["plain", "plain", "norm", "plain", "norm", "plain", "gate"]
    gains = _tile_gains([q_gain * scale] * G + [None, None, k_gain[1], None, k_gain[2], None, None], TN_PROJ)
    qkv, gates = _norm_proj(x, norm_g, w_pad, gains, modes, TM_PROJ, TN_PROJ)

    n_chunk = seq // CMP_STRIDE
    kvc_in = qkv[:, H * dh:H * dh + 2 * G * dh].reshape(bsz, n_chunk, CMP_STRIDE, 2, G, dh)
    kvc_in = kvc_in.transpose(3, 0, 4, 1, 2, 5).reshape(2, bsz, G, n_chunk, CMP_STRIDE * dh)
    pos = jnp.broadcast_to(cmp_pos.reshape(2, 1, CMP_LEN * dh), (2, 8, CMP_LEN * dh)).astype(BF16)
    kvc = _compress(kvc_in, cmp_w1.astype(BF16), cmp_w2.astype(BF16), pos, k_gain[0].reshape(1, dh))

    col = lambda t: t * (TN_PROJ // dh)
    o = _nsa_attention(qkv, kvc, gates, col, bsz, seq, G, hpg, TQ_NSA, TK_SEL)
    spec = pl.BlockSpec((TM_OUT, H * dh), lambda i: (i, 0))
    return _res_proj([o], [spec], _ident_prologue, w_out.astype(BF16), x, TM_OUT, "nsa_out")


def _sb_layer(x, bsz, seq, norm_g, w_in, q_gain, k_gain, w_out):
    T, D = x.shape
    dh = HEAD_DIM
    H = D // dh
    scale = LOG2E * dh ** -0.5
    per = (H * dh) // TN_PROJ
    modes = ["norm"] * (2 * per) + ["plain"] * per
    gains = _tile_gains([q_gain * scale] * per + [k_gain] * per + [None] * per, TN_PROJ)
    qkv = _norm_proj(x, norm_g, w_in.astype(BF16), gains, modes, TM_PROJ, TN_PROJ)
    o = _sb_attention(qkv, bsz, seq, H, TQ_SB, SB_CHAINS)
    spec = pl.BlockSpec((TM_OUT, H * dh), lambda i: (i, 0))
    return _res_proj([o], [spec], _ident_prologue, w_out.astype(BF16), x, TM_OUT, "sb_out")


def kernel(x, mix_norm, mlp_norm, mlp_w1, mlp_w2, conv_w_in, conv_w, conv_w_out, nsa_w_in, nsa_q_gain, nsa_k_gain, nsa_cmp_pos, nsa_cmp_w1, nsa_cmp_w2, nsa_w_out, pool_w, pool_scale, sb_w_in, sb_q_gain, sb_k_gain, sb_w_out):
    bsz, seq, d = x.shape
    depth = mix_norm.shape[0]
    n_mixers = 4
    xf = x.reshape(bsz * seq, d)
    for i in range(depth):
        kind, j = i % n_mixers, i // n_mixers
        if kind == 0:
            xf = _conv_layer(xf, seq, mix_norm[i], conv_w_in[j], conv_w[j], conv_w_out[j])
        elif kind == 1:
            xf = _nsa_layer(xf, bsz, seq, mix_norm[i], nsa_w_in[j], nsa_q_gain[j], nsa_k_gain[j],
                            nsa_cmp_pos[j], nsa_cmp_w1[j], nsa_cmp_w2[j], nsa_w_out[j])
        elif kind == 2:
            xf = _pool_mixer(xf, mix_norm[i], pool_w[j].astype(BF16), pool_scale[j], seq, TM_POOL)
        else:
            xf = _sb_layer(xf, bsz, seq, mix_norm[i], sb_w_in[j], sb_q_gain[j], sb_k_gain[j], sb_w_out[j])
        xf = _mlp(xf, mlp_norm[i], mlp_w1, mlp_w2, i, TM_MLP, TF_MLP)
    return xf.reshape(bsz, seq, d)
```

```python
import functools

import jax
import jax.numpy as jnp
from jax import lax
from jax.experimental import pallas as pl
from jax.experimental.pallas import tpu as pltpu

F32 = jnp.float32
BF16 = jnp.bfloat16

HEAD_DIM = 128
EPS = 1e-6
NEG = -1e30
BIG = 1e4
CONV_WIDTH = 3
NSA_KV_GROUPS = 4
CMP_LEN = 32
CMP_STRIDE = 16
SEL_LEN = 64
SEL_SHIFT = SEL_LEN.bit_length() - 1
assert 1 << SEL_SHIFT == SEL_LEN
SEL_TOPK = 16
WINDOW = 512
POOL_WINDOWS = (2, 4, 8, 16)
LANES = 128
VMEM_LIMIT = 56 * 1024 * 1024


def _params(*sem):
    return pltpu.CompilerParams(dimension_semantics=sem, vmem_limit_bytes=VMEM_LIMIT)


def _rms(xv, g):
    ms = jnp.mean(xv * xv, axis=-1, keepdims=True)
    return xv * lax.rsqrt(ms + EPS) * g


def _dot(a, b):
    return jnp.dot(a, b, preferred_element_type=F32)


def _dot_nt(a, b):
    return lax.dot_general(a, b, (((1,), (1,)), ((), ())), preferred_element_type=F32)


def _split_dot(a, b):
    hi = a.astype(BF16)
    r1 = a - hi.astype(F32)
    mid = r1.astype(BF16)
    lo = (r1 - mid.astype(F32)).astype(BF16)
    return _dot(hi, b) + _dot(mid, b) + _dot(lo, b)


def _norm_proj_body(modes, tn, has_gate, x_ref, g_ref, w_ref, gain_ref, *rest):
    if has_gate:
        o_ref, og_ref, h_scr = rest
    else:
        o_ref, h_scr = rest
    j = pl.program_id(1)

    @pl.when(j == 0)
    def _():
        h_scr[...] = _rms(x_ref[...], g_ref[...]).astype(BF16)

    acc = _dot(h_scr[...], w_ref[...].astype(BF16))

    def tiles_of(mode):
        return [t for t, m in enumerate(modes) if m == mode]

    def any_of(tiles):
        c = j == tiles[0]
        for t in tiles[1:]:
            c = c | (j == t)
        return c

    if tiles_of("plain"):
        @pl.when(any_of(tiles_of("plain")))
        def _():
            o_ref[...] = acc.astype(BF16)

    if tiles_of("norm"):
        @pl.when(any_of(tiles_of("norm")))
        def _():
            gain = gain_ref[0]
            for h in range(tn // HEAD_DIM):
                sl = slice(h * HEAD_DIM, (h + 1) * HEAD_DIM)
                o_ref[:, sl] = _rms(acc[:, sl], gain[:, sl]).astype(BF16)

    if tiles_of("gate"):
        @pl.when(any_of(tiles_of("gate")))
        def _():
            o_ref[...] = acc.astype(BF16)
            og_ref[...] = jax.nn.sigmoid(acc[:, :LANES])


def _norm_proj(x, g, w, gains, modes, tm, tn):
    T, D = x.shape
    N = w.shape[1]
    assert N == len(modes) * tn and T % tm == 0
    has_gate = "gate" in modes
    out_shape = [jax.ShapeDtypeStruct((T, N), BF16)]
    out_specs = [pl.BlockSpec((tm, tn), lambda i, j: (i, j))]
    if has_gate:
        out_shape.append(jax.ShapeDtypeStruct((T, LANES), F32))
        out_specs.append(pl.BlockSpec((tm, LANES), lambda i, j: (i, 0)))
    res = pl.pallas_call(
        functools.partial(_norm_proj_body, tuple(modes), tn, has_gate),
        grid=(T // tm, N // tn),
        in_specs=[
            pl.BlockSpec((tm, D), lambda i, j: (i, 0)),
            pl.BlockSpec((1, D), lambda i, j: (0, 0)),
            pl.BlockSpec((D, tn), lambda i, j: (0, j)),
            pl.BlockSpec((1, 1, tn), lambda i, j: (j, 0, 0)),
        ],
        out_specs=out_specs,
        out_shape=out_shape,
        scratch_shapes=[pltpu.VMEM((tm, D), BF16)],
        compiler_params=_params("parallel", "arbitrary"),
        name="norm_proj",
    )(x, g.reshape(1, D), w, gains)
    return res if has_gate else res[0]


def _res_proj_body(prologue, n_rows, *refs):
    row_refs = refs[:n_rows]
    w_ref, x_ref, o_ref = refs[n_rows:n_rows + 3]
    a = prologue(*row_refs)
    o_ref[...] = x_ref[...] + _dot(a, w_ref[...])


def _res_proj(rows, row_specs, prologue, w, x, tm, name):
    T, D = x.shape
    K = w.shape[0]
    return pl.pallas_call(
        functools.partial(_res_proj_body, prologue, len(rows)),
        grid=(T // tm,),
        in_specs=list(row_specs) + [
            pl.BlockSpec((K, D), lambda i: (0, 0)),
            pl.BlockSpec((tm, D), lambda i: (i, 0)),
        ],
        out_specs=pl.BlockSpec((tm, D), lambda i: (i, 0)),
        out_shape=jax.ShapeDtypeStruct((T, D), F32),
        compiler_params=_params("parallel"),
        name=name,
    )(*rows, w, x)


def _mlp_body(x_ref, g_ref, w1_ref, w2_ref, o_ref, h_scr):
    f = pl.program_id(1)

    @pl.when(f == 0)
    def _():
        xv = x_ref[...]
        h_scr[...] = _rms(xv, g_ref[...]).astype(BF16)
        o_ref[...] = xv

    a = jnp.maximum(_dot(h_scr[...], w1_ref[...].astype(BF16)), 0.0)
    o_ref[...] += _dot((a * a).astype(BF16), w2_ref[...].astype(BF16))


def _mlp(x, g, w1, w2, layer, tm, tf):
    T, D = x.shape
    FF = w1.shape[2]
    return pl.pallas_call(
        _mlp_body,
        grid=(T // tm, FF // tf),
        in_specs=[
            pl.BlockSpec((tm, D), lambda i, f: (i, 0), pipeline_mode=pl.Buffered(1)),
            pl.BlockSpec((1, D), lambda i, f: (0, 0)),
            pl.BlockSpec((None, D, tf), lambda i, f: (layer, 0, f)),
            pl.BlockSpec((None, tf, D), lambda i, f: (layer, f, 0)),
        ],
        out_specs=pl.BlockSpec((tm, D), lambda i, f: (i, 0)),
        out_shape=jax.ShapeDtypeStruct((T, D), F32),
        scratch_shapes=[pltpu.VMEM((tm, D), BF16)],
        compiler_params=_params("parallel", "arbitrary"),
        name="mlp",
    )(x, g.reshape(1, D), w1, w2)


HALO = 16


def _conv_prologue(tm, tiles_per_seq, b_ref, c_ref, v_ref, cp_ref, vp_ref, cw_ref):
    i = pl.program_id(0)
    first = (i % tiles_per_seq) == 0
    u = c_ref[...].astype(F32) * v_ref[...].astype(F32)
    up = cp_ref[...].astype(F32) * vp_ref[...].astype(F32)
    up = jnp.where(first, 0.0, up)
    row = lax.broadcasted_iota(jnp.int32, u.shape, 0)
    r1 = jnp.where(row == 0, up[HALO - 1:HALO, :], pltpu.roll(u, 1, 0))
    r2 = pltpu.roll(u, 2, 0)
    r2 = jnp.where(row == 0, up[HALO - 2:HALO - 1, :], jnp.where(row == 1, up[HALO - 1:HALO, :], r2))
    cw = cw_ref[...]
    y = cw[0:1, :] * r2 + cw[1:2, :] * r1 + cw[2:3, :] * u
    return (b_ref[...].astype(F32) * y).astype(BF16)


def _conv_out(bcv, conv_w, w_out, x, seq, tm):
    T, D = x.shape
    hb = tm // HALO
    prev = lambda col: (lambda i: (jnp.maximum(i * hb - 1, 0), col))
    specs = [
        pl.BlockSpec((tm, D), lambda i: (i, 0)),
        pl.BlockSpec((tm, D), lambda i: (i, 1)),
        pl.BlockSpec((tm, D), lambda i: (i, 2)),
        pl.BlockSpec((HALO, D), prev(1)),
        pl.BlockSpec((HALO, D), prev(2)),
        pl.BlockSpec((CONV_WIDTH, D), lambda i: (0, 0)),
    ]
    prologue = functools.partial(_conv_prologue, tm, seq // tm)
    return _res_proj([bcv, bcv, bcv, bcv, bcv, conv_w], specs, prologue, w_out, x, tm, "conv_out")


def _pool_body(tm, tiles_per_seq, x_ref, xp_ref, g_ref, w_ref, sc_ref, o_ref):
    i = pl.program_id(0)
    first = (i % tiles_per_seq) == 0
    xv = x_ref[...]
    g = g_ref[...]
    h = _rms(xv, g)
    hp = jnp.where(first, 0.0, _rms(xp_ref[...], g))
    pos = (i % tiles_per_seq) * tm + lax.broadcasted_iota(jnp.int32, (tm, 1), 0)
    cg = h.shape[1] // len(POOL_WINDOWS)
    for gi, win in enumerate(POOL_WINDOWS):
        sl = slice(gi * cg, (gi + 1) * cg)
        hg = h[:, sl]
        s = jnp.concatenate([hp[:, sl], hg], axis=0)
        k = 1
        while k < win:
            s = s + pltpu.roll(s, k, 0)
            k *= 2
        cnt = jnp.minimum(pos + 1, win).astype(F32)
        pooled = s[HALO:, :] / cnt - hg
        y = _dot(pooled.astype(BF16), w_ref[gi])
        o_ref[:, sl] = xv[:, sl] + y * sc_ref[:, sl]


def _pool_mixer(x, g, w, scale, seq, tm):
    T, D = x.shape
    ng, cg, _ = w.shape
    hb = tm // HALO
    return pl.pallas_call(
        functools.partial(_pool_body, tm, seq // tm),
        grid=(T // tm,),
        in_specs=[
            pl.BlockSpec((tm, D), lambda i: (i, 0)),
            pl.BlockSpec((HALO, D), lambda i: (jnp.maximum(i * hb - 1, 0), 0)),
            pl.BlockSpec((1, D), lambda i: (0, 0)),
            pl.BlockSpec((ng, cg, cg), lambda i: (0, 0, 0)),
            pl.BlockSpec((1, D), lambda i: (0, 0)),
        ],
        out_specs=pl.BlockSpec((tm, D), lambda i: (i, 0)),
        out_shape=jax.ShapeDtypeStruct((T, D), F32),
        compiler_params=_params("parallel"),
        name="pool_mixer",
    )(x, x, g.reshape(1, D), w, scale.reshape(1, D))


LOG2E = 1.4426950408889634
SB_STOP_LOG2 = -151.0


def _sb_body(tq, n_chain, q_ref, k_ref, v_ref, o_ref):
    nq = q_ref.shape[0] // tq
    row = lax.broadcasted_iota(jnp.int32, (tq, tq), 0)
    col = lax.broadcasted_iota(jnp.int32, (tq, tq), 1)
    before_diag = col < row
    r2 = lax.broadcasted_iota(jnp.int32, (2 * tq, 2 * tq), 0)
    c2 = lax.broadcasted_iota(jnp.int32, (2 * tq, 2 * tq), 1)
    key = jnp.where(r2 >= tq, r2 - tq, r2)
    sums = ((c2 >= tq) | (key > c2)).astype(BF16)

    def tiles(qs, js, cs, accs, diag, lives=None):
        chains = range(n_chain)
        offs = [pl.multiple_of(js[r] * tq, tq) for r in chains]
        zs = [_dot_nt(qs[r], k_ref[pl.ds(offs[r], tq), :]) for r in chains]
        log_1m = [jnp.log(1.0 + jnp.exp2(-jnp.abs(z))) * (-LOG2E) - jnp.maximum(z, 0.0) for z in zs]
        if diag:
            log_1m = [jnp.where(before_diag, x, 0.0) for x in log_1m]
        his = [x.astype(BF16) for x in log_1m]
        mids = [(x - hi.astype(F32)).astype(BF16) for x, hi in zip(log_1m, his)]
        ts = [_dot(jnp.concatenate([hi, mid], axis=1), sums) for hi, mid in zip(his, mids)]
        ws = [jnp.exp2(zs[r] + log_1m[r] + ts[r][:, :tq] + cs[r]) for r in chains]
        tile_sums = [t[:, tq:] for t in ts]
        if diag:
            ws = [jnp.where(before_diag, w, 0.0) for w in ws]
        if lives is not None:
            ws = [jnp.where(lives[r], ws[r], 0.0) for r in chains]
            tile_sums = [jnp.where(lives[r], tile_sums[r], 0.0) for r in chains]
        accs = tuple(accs[r] + _dot(ws[r].astype(BF16), v_ref[pl.ds(offs[r], tq), :]) for r in chains)
        return tuple(cs[r] + tile_sums[r] for r in chains), accs

    def q_group(gi, carry):
        base = gi * n_chain
        chains = range(n_chain)
        qs = [q_ref[pl.ds(pl.multiple_of((base + r) * tq, tq), tq), :] for r in chains]
        zero = (jnp.zeros((tq, tq), F32),) * n_chain
        cs, accs = tiles(qs, [base + r for r in chains], zero, zero, True)

        def live_max(cs, n):
            vals = [jnp.where(base + r - n >= 0, jnp.max(cs[r]), -jnp.inf) for r in chains]
            return functools.reduce(jnp.maximum, vals)

        def cond(st):
            n, _, _, cmax = st
            return (n <= base + n_chain - 1) & (cmax > SB_STOP_LOG2)

        def body(st):
            n, cs, accs, _ = st
            js = [base + r - n for r in chains]
            cs, accs = tiles(qs, [jnp.maximum(j, 0) for j in js], cs, accs, False, [j >= 0 for j in js])
            return n + 1, cs, accs, live_max(cs, n + 1)

        _, _, accs, _ = lax.while_loop(cond, body, (1, cs, accs, live_max(cs, 1)))
        for r in chains:
            o_ref[pl.ds(pl.multiple_of((base + r) * tq, tq), tq), :] = accs[r].astype(BF16)
        return carry

    lax.fori_loop(0, nq // n_chain, q_group, 0)


def _sb_attention(qkv, bsz, seq, n_heads, tq, n_chain):
    assert tq == HEAD_DIM == LANES
    assert seq % (tq * n_chain) == 0
    T = bsz * seq
    return pl.pallas_call(
        functools.partial(_sb_body, tq, n_chain),
        grid=(bsz, n_heads),
        in_specs=[
            pl.BlockSpec((seq, HEAD_DIM), lambda b, h: (b, h)),
            pl.BlockSpec((seq, HEAD_DIM), lambda b, h: (b, n_heads + h)),
            pl.BlockSpec((seq, HEAD_DIM), lambda b, h: (b, 2 * n_heads + h)),
        ],
        out_specs=pl.BlockSpec((seq, HEAD_DIM), lambda b, h: (b, h)),
        out_shape=jax.ShapeDtypeStruct((T, n_heads * HEAD_DIM), BF16),
        compiler_params=_params("parallel", "parallel"),
        name="sb_attention",
    )(qkv, qkv, qkv)


def _compress_body(n_chunk, t_ref, w1_ref, w2_ref, pos_ref, gain_ref, o_ref):
    kv = pl.program_id(0)
    a = jnp.concatenate([t_ref[pl.ds(p, n_chunk, stride=CMP_STRIDE), :].astype(BF16) for p in range(CMP_STRIDE)],
                        axis=1)
    half = a.shape[1]
    w1 = w1_ref[0]
    const = _dot(pos_ref[0], w1)[0:1, :]
    first = _dot(a, w1[:half, :])
    second = _dot(a, w1[half:, :])
    pre = first + pltpu.roll(second, n_chunk - 1, 0) + const
    hid = pre * jax.nn.sigmoid(pre)
    out = _dot(hid.astype(BF16), w2_ref[0])
    normed = _rms(out, gain_ref[...])
    o_ref[0, 0, 0] = jnp.where(kv == 0, normed, out).astype(BF16)


def _compress(t, w1, w2, pos, gain, bsz, seq, ng):
    dh = HEAD_DIM
    n_chunk = seq // CMP_STRIDE
    width = CMP_STRIDE * dh
    return pl.pallas_call(
        functools.partial(_compress_body, n_chunk),
        grid=(2, bsz, ng),
        in_specs=[
            pl.BlockSpec((seq, dh), lambda s, b, g: (b, s * ng + g)),
            pl.BlockSpec((1, 2 * width, dh), lambda s, b, g: (s, 0, 0)),
            pl.BlockSpec((1, dh, dh), lambda s, b, g: (s, 0, 0)),
            pl.BlockSpec((1, 8, 2 * width), lambda s, b, g: (s, 0, 0)),
            pl.BlockSpec((1, dh), lambda s, b, g: (0, 0)),
        ],
        out_specs=pl.BlockSpec((1, 1, 1, n_chunk, dh), lambda s, b, g: (s, b, g, 0, 0)),
        out_shape=jax.ShapeDtypeStruct((2, bsz, ng, n_chunk, dh), BF16),
        compiler_params=_params("parallel", "parallel", "parallel"),
        name="nsa_compress",
    )(t, w1, w2, pos, gain)


def _stack_heads(q_ref, hpg):
    return jnp.concatenate([q_ref[:, h * HEAD_DIM:(h + 1) * HEAD_DIM] for h in range(hpg)], axis=0)


def _gate_column(gates, col):
    lane = lax.broadcasted_iota(jnp.int32, gates.shape, 1)
    return jnp.sum(jnp.where(lane == col, gates, 0.0), axis=1, keepdims=True)


def _cmp_branch(qs, kc, vc, i, tq, hpg, n_sel, vt_scr, st_scr):
    n_chunk = kc.shape[0]
    s = _dot_nt(qs, kc)
    t1 = i * tq + lax.broadcasted_iota(jnp.int32, (tq, n_chunk), 0)
    c1 = lax.broadcasted_iota(jnp.int32, (tq, n_chunk), 1)
    valid1 = c1 * CMP_STRIDE + (CMP_LEN - 1) <= t1
    valid = jnp.concatenate([valid1] * hpg, axis=0)
    s = jnp.where(valid, s, NEG)
    m = jnp.max(s, axis=1, keepdims=True)
    e = jnp.where(valid, jnp.exp2(s - m), 0.0)
    l = jnp.sum(e, axis=1, keepdims=True)
    p = e / jnp.where(l > 0.0, l, 1.0)
    o = _dot(p.astype(BF16), vc)

    p_sum = p[0:tq, :]
    for h in range(1, hpg):
        p_sum = p_sum + p[h * tq:(h + 1) * tq, :]
    ci = lax.broadcasted_iota(jnp.int32, (n_chunk, LANES), 0)
    sj = lax.broadcasted_iota(jnp.int32, (n_chunk, LANES), 1)
    overlap = ((ci * CMP_STRIDE < (sj + 1) * SEL_LEN) & (ci * CMP_STRIDE + CMP_LEN > sj * SEL_LEN)
               & (ci < n_chunk - 1) & (sj < n_sel)).astype(BF16)
    imp = _split_dot(p_sum, overlap)
    t2 = i * tq + lax.broadcasted_iota(jnp.int32, (tq, LANES), 0)
    blk = lax.broadcasted_iota(jnp.int32, (tq, LANES), 1)
    cur = lax.shift_right_logical(t2, SEL_SHIFT)
    forced = (blk == 0) | (blk == cur) | (blk == cur - 1)
    blk_valid = blk * SEL_LEN <= t2
    score = jnp.where(forced, BIG, jnp.where(blk_valid, imp, -BIG))

    vt_scr[...] = score.T
    SUB = 8
    vts = [vt_scr[a * SUB:(a + 1) * SUB, :] for a in range(n_sel // SUB)]
    cnts = [jnp.zeros((SUB, tq), F32) for _ in vts]
    jrow = lax.broadcasted_iota(jnp.int32, (SUB, tq), 0)
    for b in range(n_sel):
        vb = vt_scr[b:b + 1, :]
        for a, vt in enumerate(vts):
            if b < a * SUB:
                ahead = vb >= vt
            elif b >= (a + 1) * SUB:
                ahead = vb > vt
            else:
                ahead = (vb > vt) | ((vb == vt) & (jrow > b - a * SUB))
            cnts[a] = cnts[a] + ahead.astype(F32)
    st_scr[...] = jnp.zeros_like(st_scr)
    for a, cnt in enumerate(cnts):
        st_scr[a * SUB:(a + 1) * SUB, :] = (cnt < float(min(SEL_TOPK, n_sel))).astype(F32)
    return o, st_scr[...].T.astype(BF16)


def _fill_values_and_ones(vo_scr, v_ref):
    vo_scr[:, :HEAD_DIM] = v_ref[...]
    vo_scr[:, HEAD_DIM:] = jnp.ones((vo_scr.shape[0], LANES), BF16)


def _sel_branch(qs, k_ref, vo_scr, sel, i, tq, tk, hpg, m_scr, acc_scr):
    dh = HEAD_DIM
    m_scr[...] = jnp.full_like(m_scr, NEG)
    acc_scr[...] = jnp.zeros_like(acc_scr)
    qpos = i * tq + lax.broadcasted_iota(jnp.int32, (tq, tk), 0)
    kcol = lax.broadcasted_iota(jnp.int32, (tq, tk), 1)
    eb = lax.broadcasted_iota(jnp.int32, (LANES, tk), 0)
    ek = lax.broadcasted_iota(jnp.int32, (LANES, tk), 1)
    n_tiles = ((i + 1) * tq + tk - 1) // tk

    def scores(j):
        k = k_ref[pl.ds(pl.multiple_of(j * tk, tk), tk), :]
        expand = (eb == lax.shift_right_logical(j * tk + ek, SEL_SHIFT)).astype(BF16)
        mask1 = (_dot(sel, expand) > 0.5) & (j * tk + kcol <= qpos)
        bias1 = jnp.where(mask1, 0.0, NEG)
        return _dot_nt(qs, k) + jnp.concatenate([bias1] * hpg, axis=0)

    def step(n, s):
        j = n_tiles - 1 - n
        s_next = scores(jnp.maximum(j - 1, 0))
        vo = vo_scr[pl.ds(pl.multiple_of(j * tk, tk), tk), :]
        chunks = [s[:, c * LANES:(c + 1) * LANES] for c in range(tk // LANES)]
        mx = chunks[0]
        for ch in chunks[1:]:
            mx = jnp.maximum(mx, ch)
        m_old = m_scr[...]
        m_new = jnp.maximum(m_old, jnp.max(mx, axis=1, keepdims=True))
        alpha = jnp.exp2(m_old - m_new)
        p = jnp.concatenate([jnp.exp2(ch - m_new).astype(BF16) for ch in chunks], axis=1)
        pv = _dot(p, vo)
        acc_scr[:, :dh] = alpha * acc_scr[:, :dh] + pv[:, :dh]
        acc_scr[:, dh:] = alpha * acc_scr[:, dh:] + pv[:, dh:]
        m_scr[...] = m_new
        return s_next

    lax.fori_loop(0, n_tiles, step, scores(n_tiles - 1))
    return acc_scr[:, :dh] / acc_scr[:, dh:]


def _win_branch(qs, k_ref, vo_scr, i, tq, hpg):
    dh = HEAD_DIM
    span = WINDOW + tq
    off = pl.multiple_of(jnp.maximum(i * tq - WINDOW, 0), tq)
    k = k_ref[pl.ds(off, span), :]
    vo = vo_scr[pl.ds(off, span), :]
    qpos = i * tq + lax.broadcasted_iota(jnp.int32, (tq, span), 0)
    kpos = off + lax.broadcasted_iota(jnp.int32, (tq, span), 1)
    bias1 = jnp.where((kpos <= qpos) & (qpos - kpos < WINDOW), 0.0, NEG)
    s = _dot_nt(qs, k) + jnp.concatenate([bias1] * hpg, axis=0)
    p = jnp.exp2(s - jnp.max(s, axis=1, keepdims=True)).astype(BF16)
    pv = _dot(p, vo)
    return pv[:, :dh] / pv[:, dh:]


def _nsa_attn_body(tq, tk, hpg, n_sel, q_ref, kc_ref, vc_ref, ks_ref, vs_ref, kw_ref, vw_ref, gates_ref, o_ref,
                   vos_scr, vow_scr, vt_scr, st_scr, m_scr, acc_scr):
    g = pl.program_id(1)
    i = pl.program_id(2)

    @pl.when(i == 0)
    def _():
        _fill_values_and_ones(vos_scr, vs_ref)
        _fill_values_and_ones(vow_scr, vw_ref)

    qs = _stack_heads(q_ref, hpg)
    o_cmp, sel = _cmp_branch(qs, kc_ref[0, 0, 0], vc_ref[0, 0, 0], i, tq, hpg, n_sel, vt_scr, st_scr)
    o_win = _win_branch(qs, kw_ref, vow_scr, i, tq, hpg)
    o_sel = _sel_branch(qs, ks_ref, vos_scr, sel, i, tq, tk, hpg, m_scr, acc_scr)
    gates = gates_ref[...]
    for h in range(hpg):
        rows = slice(h * tq, (h + 1) * tq)
        col = 3 * (g * hpg + h)
        o = (_gate_column(gates, col) * o_cmp[rows, :] + _gate_column(gates, col + 1) * o_sel[rows, :]
             + _gate_column(gates, col + 2) * o_win[rows, :])
        o_ref[:, h * HEAD_DIM:(h + 1) * HEAD_DIM] = o.astype(BF16)


def _nsa_attention(qkv, kvc, gates, col, bsz, seq, ng, hpg, tq, tk):
    T = bsz * seq
    nq = seq // tq
    gw = hpg * HEAD_DIM
    rows = hpg * tq
    n_chunk = kvc.shape[3]
    n_sel = seq // SEL_LEN
    assert n_sel <= LANES and n_sel % 8 == 0
    assert seq >= WINDOW + tq and WINDOW % tq == 0
    resident = lambda t: pl.BlockSpec((seq, HEAD_DIM), lambda b, g, i: (b, col(t) + g))
    compressed = lambda s: pl.BlockSpec((1, 1, 1, n_chunk, HEAD_DIM), lambda b, g, i: (s, b, g, 0, 0))
    return pl.pallas_call(
        functools.partial(_nsa_attn_body, tq, tk, hpg, n_sel),
        grid=(bsz, ng, nq),
        in_specs=[
            pl.BlockSpec((tq, gw), lambda b, g, i: (b * nq + i, g)),
            compressed(0), compressed(1),
            resident(ng + 2), resident(ng + 3), resident(ng + 4), resident(ng + 5),
            pl.BlockSpec((tq, LANES), lambda b, g, i: (b * nq + i, 0)),
        ],
        out_specs=pl.BlockSpec((tq, gw), lambda b, g, i: (b * nq + i, g)),
        out_shape=jax.ShapeDtypeStruct((T, ng * gw), BF16),
        scratch_shapes=[
            pltpu.VMEM((seq, HEAD_DIM + LANES), BF16), pltpu.VMEM((seq, HEAD_DIM + LANES), BF16),
            pltpu.VMEM((LANES, tq), F32), pltpu.VMEM((LANES, tq), F32),
            pltpu.VMEM((rows, LANES), F32), pltpu.VMEM((rows, HEAD_DIM + LANES), F32),
        ],
        compiler_params=_params("parallel", "parallel", "arbitrary"),
        name="nsa_attention",
    )(qkv, kvc, kvc, qkv, qkv, qkv, qkv, gates)


def _ident_prologue(a_ref):
    return a_ref[...]


TM_PROJ = 1024
TN_PROJ = 512
TM_OUT = 256
TM_MLP = 1024
TF_MLP = 512
TM_POOL = 256
TQ_SB = 128
SB_CHAINS = 8
TQ_NSA = 256
TK_SEL = 512


def _tile_gains(per_tile, tn):
    rows = [jnp.tile(g.astype(F32), tn // HEAD_DIM) if g is not None else jnp.ones((tn,), F32) for g in per_tile]
    return jnp.stack(rows).reshape(len(per_tile), 1, tn)


def _conv_layer(x, seq, norm_g, w_in, conv_w, w_out):
    n_tiles = w_in.shape[1] // TN_PROJ
    bcv = _norm_proj(x, norm_g, w_in, _tile_gains([None] * n_tiles, TN_PROJ),
                     ["plain"] * n_tiles, TM_PROJ, TN_PROJ)
    return _conv_out(bcv, conv_w, w_out.astype(BF16), x, seq, TM_OUT)


def _nsa_layer(x, bsz, seq, norm_g, w_in, q_gain, k_gain, cmp_pos, cmp_w1, cmp_w2, w_out):
    T, D = x.shape
    dh, G = HEAD_DIM, NSA_KV_GROUPS
    H = D // dh
    hpg = H // G
    gw = hpg * dh
    assert gw == TN_PROJ and G * dh == TN_PROJ
    scale = LOG2E * dh ** -0.5
    width = w_in.shape[1]
    n_tiles = -(-width // TN_PROJ)
    w_pad = jnp.pad(w_in, ((0, 0), (0, n_tiles * TN_PROJ - width))).astype(BF16)
    modes = ["norm"] * G + ["plain", "plain", "norm", "plain", "norm", "plain", "gate"]
    gains = _tile_gains([q_gain * scale] * G + [None, None, k_gain[1], None, k_gain[2], None, None], TN_PROJ)
    qkv, gates = _norm_proj(x, norm_g, w_pad, gains, modes, TM_PROJ, TN_PROJ)

    kvc_in = qkv[:, H * dh:H * dh + 2 * G * dh].astype(F32)
    pos = jnp.broadcast_to(cmp_pos.reshape(2, 1, CMP_LEN * dh), (2, 8, CMP_LEN * dh)).astype(BF16)
    kvc = _compress(kvc_in, cmp_w1.astype(BF16), cmp_w2.astype(BF16), pos, k_gain[0].reshape(1, dh), bsz, seq, G)

    col = lambda t: t * (TN_PROJ // dh)
    o = _nsa_attention(qkv, kvc, gates, col, bsz, seq, G, hpg, TQ_NSA, TK_SEL)
    spec = pl.BlockSpec((TM_OUT, H * dh), lambda i: (i, 0))
    return _res_proj([o], [spec], _ident_prologue, w_out.astype(BF16), x, TM_OUT, "nsa_out")


def _sb_layer(x, bsz, seq, norm_g, w_in, q_gain, k_gain, w_out):
    T, D = x.shape
    dh = HEAD_DIM
    H = D // dh
    scale = LOG2E * dh ** -0.5
    per = (H * dh) // TN_PROJ
    modes = ["norm"] * (2 * per) + ["plain"] * per
    gains = _tile_gains([q_gain * scale] * per + [k_gain] * per + [None] * per, TN_PROJ)
    qkv = _norm_proj(x, norm_g, w_in, gains, modes, TM_PROJ, TN_PROJ)
    o = _sb_attention(qkv, bsz, seq, H, TQ_SB, SB_CHAINS)
    spec = pl.BlockSpec((TM_OUT, H * dh), lambda i: (i, 0))
    return _res_proj([o], [spec], _ident_prologue, w_out.astype(BF16), x, TM_OUT, "sb_out")


def kernel(x, mix_norm, mlp_norm, mlp_w1, mlp_w2, conv_w_in, conv_w, conv_w_out, nsa_w_in, nsa_q_gain, nsa_k_gain, nsa_cmp_pos, nsa_cmp_w1, nsa_cmp_w2, nsa_w_out, pool_w, pool_scale, sb_w_in, sb_q_gain, sb_k_gain, sb_w_out):
    bsz, seq, d = x.shape
    depth = mix_norm.shape[0]
    n_mixers = 4
    xf = x.reshape(bsz * seq, d)
    for i in range(depth):
        kind, j = i % n_mixers, i // n_mixers
        if kind == 0:
            xf = _conv_layer(xf, seq, mix_norm[i], conv_w_in[j], conv_w[j], conv_w_out[j])
        elif kind == 1:
            xf = _nsa_layer(xf, bsz, seq, mix_norm[i], nsa_w_in[j], nsa_q_gain[j], nsa_k_gain[j],
                            nsa_cmp_pos[j], nsa_cmp_w1[j], nsa_cmp_w2[j], nsa_w_out[j])
        elif kind == 2:
            xf = _pool_mixer(xf, mix_norm[i], pool_w[j].astype(BF16), pool_scale[j], seq, TM_POOL)
        else:
            xf = _sb_layer(xf, bsz, seq, mix_norm[i], sb_w_in[j], sb_q_gain[j], sb_k_gain[j], sb_w_out[j])
        xf = _mlp(xf, mlp_norm[i], mlp_w1, mlp_w2, i, TM_MLP, TF_MLP)
    return xf.reshape(bsz, seq, d)
```

```python
import functools

import jax
import jax.numpy as jnp
from jax import lax
from jax.experimental import pallas as pl
from jax.experimental.pallas import tpu as pltpu

F32 = jnp.float32
BF16 = jnp.bfloat16

HEAD_DIM = 128
EPS = 1e-6
NEG = -1e30
BIG = 1e4
CONV_WIDTH = 3
NSA_KV_GROUPS = 4
CMP_LEN = 32
CMP_STRIDE = 16
SEL_LEN = 64
SEL_SHIFT = SEL_LEN.bit_length() - 1
assert 1 << SEL_SHIFT == SEL_LEN
SEL_TOPK = 16
WINDOW = 512
POOL_WINDOWS = (2, 4, 8, 16)
LANES = 128
VMEM_LIMIT = 56 * 1024 * 1024


def _params(*sem):
    return pltpu.CompilerParams(dimension_semantics=sem, vmem_limit_bytes=VMEM_LIMIT)


def _rms(xv, g):
    ms = jnp.mean(xv * xv, axis=-1, keepdims=True)
    return xv * lax.rsqrt(ms + EPS) * g


def _dot(a, b):
    return jnp.dot(a, b, preferred_element_type=F32)


def _dot_nt(a, b):
    return lax.dot_general(a, b, (((1,), (1,)), ((), ())), preferred_element_type=F32)


def _split_dot(a, b):
    hi = a.astype(BF16)
    r1 = a - hi.astype(F32)
    mid = r1.astype(BF16)
    lo = (r1 - mid.astype(F32)).astype(BF16)
    return _dot(hi, b) + _dot(mid, b) + _dot(lo, b)


def _norm_proj_body(modes, tn, has_gate, x_ref, g_ref, w_ref, gain_ref, *rest):
    if has_gate:
        o_ref, og_ref, h_scr = rest
    else:
        o_ref, h_scr = rest
    j = pl.program_id(1)

    @pl.when(j == 0)
    def _():
        h_scr[...] = _rms(x_ref[...], g_ref[...]).astype(BF16)

    acc = _dot(h_scr[...], w_ref[...].astype(BF16))

    def tiles_of(mode):
        return [t for t, m in enumerate(modes) if m == mode]

    def any_of(tiles):
        c = j == tiles[0]
        for t in tiles[1:]:
            c = c | (j == t)
        return c

    if tiles_of("plain"):
        @pl.when(any_of(tiles_of("plain")))
        def _():
            o_ref[...] = acc.astype(BF16)

    if tiles_of("norm"):
        @pl.when(any_of(tiles_of("norm")))
        def _():
            gain = gain_ref[0]
            for h in range(tn // HEAD_DIM):
                sl = slice(h * HEAD_DIM, (h + 1) * HEAD_DIM)
                o_ref[:, sl] = _rms(acc[:, sl], gain[:, sl]).astype(BF16)

    if tiles_of("gate"):
        @pl.when(any_of(tiles_of("gate")))
        def _():
            o_ref[...] = acc.astype(BF16)
            og_ref[...] = jax.nn.sigmoid(acc[:, :LANES])


def _norm_proj(x, g, w, gains, modes, tm, tn):
    T, D = x.shape
    N = w.shape[1]
    assert N == len(modes) * tn and T % tm == 0
    has_gate = "gate" in modes
    out_shape = [jax.ShapeDtypeStruct((T, N), BF16)]
    out_specs = [pl.BlockSpec((tm, tn), lambda i, j: (i, j))]
    if has_gate:
        out_shape.append(jax.ShapeDtypeStruct((T, LANES), F32))
        out_specs.append(pl.BlockSpec((tm, LANES), lambda i, j: (i, 0)))
    res = pl.pallas_call(
        functools.partial(_norm_proj_body, tuple(modes), tn, has_gate),
        grid=(T // tm, N // tn),
        in_specs=[
            pl.BlockSpec((tm, D), lambda i, j: (i, 0)),
            pl.BlockSpec((1, D), lambda i, j: (0, 0)),
            pl.BlockSpec((D, tn), lambda i, j: (0, j)),
            pl.BlockSpec((1, 1, tn), lambda i, j: (j, 0, 0)),
        ],
        out_specs=out_specs,
        out_shape=out_shape,
        scratch_shapes=[pltpu.VMEM((tm, D), BF16)],
        compiler_params=_params("parallel", "arbitrary"),
        name="norm_proj",
    )(x, g.reshape(1, D), w, gains)
    return res if has_gate else res[0]


def _res_proj_body(prologue, n_rows, *refs):
    row_refs = refs[:n_rows]
    w_ref, x_ref, o_ref = refs[n_rows:n_rows + 3]
    a = prologue(*row_refs)
    o_ref[...] = x_ref[...] + _dot(a, w_ref[...])


def _res_proj(rows, row_specs, prologue, w, x, tm, name):
    T, D = x.shape
    K = w.shape[0]
    return pl.pallas_call(
        functools.partial(_res_proj_body, prologue, len(rows)),
        grid=(T // tm,),
        in_specs=list(row_specs) + [
            pl.BlockSpec((K, D), lambda i: (0, 0)),
            pl.BlockSpec((tm, D), lambda i: (i, 0)),
        ],
        out_specs=pl.BlockSpec((tm, D), lambda i: (i, 0)),
        out_shape=jax.ShapeDtypeStruct((T, D), F32),
        compiler_params=_params("parallel"),
        name=name,
    )(*rows, w, x)


def _mlp_body(x_ref, g_ref, w1_ref, w2_ref, o_ref, h_scr):
    f = pl.program_id(1)

    @pl.when(f == 0)
    def _():
        xv = x_ref[...]
        h_scr[...] = _rms(xv, g_ref[...]).astype(BF16)
        o_ref[...] = xv

    a = jnp.maximum(_dot(h_scr[...], w1_ref[...].astype(BF16)), 0.0)
    o_ref[...] += _dot((a * a).astype(BF16), w2_ref[...].astype(BF16))


def _mlp(x, g, w1, w2, layer, tm, tf):
    T, D = x.shape
    FF = w1.shape[2]
    return pl.pallas_call(
        _mlp_body,
        grid=(T // tm, FF // tf),
        in_specs=[
            pl.BlockSpec((tm, D), lambda i, f: (i, 0)),
            pl.BlockSpec((1, D), lambda i, f: (0, 0)),
            pl.BlockSpec((None, D, tf), lambda i, f: (layer, 0, f)),
            pl.BlockSpec((None, tf, D), lambda i, f: (layer, f, 0)),
        ],
        out_specs=pl.BlockSpec((tm, D), lambda i, f: (i, 0)),
        out_shape=jax.ShapeDtypeStruct((T, D), F32),
        scratch_shapes=[pltpu.VMEM((tm, D), BF16)],
        compiler_params=_params("parallel", "arbitrary"),
        name="mlp",
    )(x, g.reshape(1, D), w1, w2)


HALO = 16


def _conv_prologue(tm, tiles_per_seq, b_ref, c_ref, v_ref, cp_ref, vp_ref, cw_ref):
    i = pl.program_id(0)
    first = (i % tiles_per_seq) == 0
    u = c_ref[...].astype(F32) * v_ref[...].astype(F32)
    up = cp_ref[...].astype(F32) * vp_ref[...].astype(F32)
    up = jnp.where(first, 0.0, up)
    row = lax.broadcasted_iota(jnp.int32, u.shape, 0)
    r1 = jnp.where(row == 0, up[HALO - 1:HALO, :], pltpu.roll(u, 1, 0))
    r2 = pltpu.roll(u, 2, 0)
    r2 = jnp.where(row == 0, up[HALO - 2:HALO - 1, :], jnp.where(row == 1, up[HALO - 1:HALO, :], r2))
    cw = cw_ref[...]
    y = cw[0:1, :] * r2 + cw[1:2, :] * r1 + cw[2:3, :] * u
    return (b_ref[...].astype(F32) * y).astype(BF16)


def _conv_out(bcv, conv_w, w_out, x, seq, tm):
    T, D = x.shape
    hb = tm // HALO
    prev = lambda col: (lambda i: (jnp.maximum(i * hb - 1, 0), col))
    specs = [
        pl.BlockSpec((tm, D), lambda i: (i, 0)),
        pl.BlockSpec((tm, D), lambda i: (i, 1)),
        pl.BlockSpec((tm, D), lambda i: (i, 2)),
        pl.BlockSpec((HALO, D), prev(1)),
        pl.BlockSpec((HALO, D), prev(2)),
        pl.BlockSpec((CONV_WIDTH, D), lambda i: (0, 0)),
    ]
    prologue = functools.partial(_conv_prologue, tm, seq // tm)
    return _res_proj([bcv, bcv, bcv, bcv, bcv, conv_w], specs, prologue, w_out, x, tm, "conv_out")


def _pool_body(tm, tiles_per_seq, x_ref, xp_ref, g_ref, w_ref, sc_ref, o_ref):
    i = pl.program_id(0)
    first = (i % tiles_per_seq) == 0
    xv = x_ref[...]
    g = g_ref[...]
    h = _rms(xv, g)
    hp = jnp.where(first, 0.0, _rms(xp_ref[...], g))
    pos = (i % tiles_per_seq) * tm + lax.broadcasted_iota(jnp.int32, (tm, 1), 0)
    cg = h.shape[1] // len(POOL_WINDOWS)
    for gi, win in enumerate(POOL_WINDOWS):
        sl = slice(gi * cg, (gi + 1) * cg)
        hg = h[:, sl]
        s = jnp.concatenate([hp[:, sl], hg], axis=0)
        k = 1
        while k < win:
            s = s + pltpu.roll(s, k, 0)
            k *= 2
        cnt = jnp.minimum(pos + 1, win).astype(F32)
        pooled = s[HALO:, :] / cnt - hg
        y = _dot(pooled.astype(BF16), w_ref[gi])
        o_ref[:, sl] = xv[:, sl] + y * sc_ref[:, sl]


def _pool_mixer(x, g, w, scale, seq, tm):
    T, D = x.shape
    ng, cg, _ = w.shape
    hb = tm // HALO
    return pl.pallas_call(
        functools.partial(_pool_body, tm, seq // tm),
        grid=(T // tm,),
        in_specs=[
            pl.BlockSpec((tm, D), lambda i: (i, 0)),
            pl.BlockSpec((HALO, D), lambda i: (jnp.maximum(i * hb - 1, 0), 0)),
            pl.BlockSpec((1, D), lambda i: (0, 0)),
            pl.BlockSpec((ng, cg, cg), lambda i: (0, 0, 0)),
            pl.BlockSpec((1, D), lambda i: (0, 0)),
        ],
        out_specs=pl.BlockSpec((tm, D), lambda i: (i, 0)),
        out_shape=jax.ShapeDtypeStruct((T, D), F32),
        compiler_params=_params("parallel"),
        name="pool_mixer",
    )(x, x, g.reshape(1, D), w, scale.reshape(1, D))


LOG2E = 1.4426950408889634
SB_STOP_LOG2 = -151.0


def _sb_body(tq, n_chain, q_ref, k_ref, v_ref, o_ref):
    nq = q_ref.shape[0] // tq
    row = lax.broadcasted_iota(jnp.int32, (tq, tq), 0)
    col = lax.broadcasted_iota(jnp.int32, (tq, tq), 1)
    before_diag = col < row
    r2 = lax.broadcasted_iota(jnp.int32, (2 * tq, 2 * tq), 0)
    c2 = lax.broadcasted_iota(jnp.int32, (2 * tq, 2 * tq), 1)
    key = jnp.where(r2 >= tq, r2 - tq, r2)
    sums = ((c2 >= tq) | (key > c2)).astype(BF16)

    def tiles(qs, js, cs, accs, diag, lives=None):
        chains = range(n_chain)
        offs = [pl.multiple_of(js[r] * tq, tq) for r in chains]
        zs = [_dot_nt(qs[r], k_ref[pl.ds(offs[r], tq), :]) for r in chains]
        log_1m = [jnp.log(1.0 + jnp.exp2(-jnp.abs(z))) * (-LOG2E) - jnp.maximum(z, 0.0) for z in zs]
        if diag:
            log_1m = [jnp.where(before_diag, x, 0.0) for x in log_1m]
        his = [x.astype(BF16) for x in log_1m]
        mids = [(x - hi.astype(F32)).astype(BF16) for x, hi in zip(log_1m, his)]
        ts = [_dot(jnp.concatenate([hi, mid], axis=1), sums) for hi, mid in zip(his, mids)]
        ws = [jnp.exp2(zs[r] + log_1m[r] + ts[r][:, :tq] + cs[r]) for r in chains]
        tile_sums = [t[:, tq:] for t in ts]
        if diag:
            ws = [jnp.where(before_diag, w, 0.0) for w in ws]
        if lives is not None:
            ws = [jnp.where(lives[r], ws[r], 0.0) for r in chains]
            tile_sums = [jnp.where(lives[r], tile_sums[r], 0.0) for r in chains]
        accs = tuple(accs[r] + _dot(ws[r].astype(BF16), v_ref[pl.ds(offs[r], tq), :]) for r in chains)
        return tuple(cs[r] + tile_sums[r] for r in chains), accs

    def q_group(gi, carry):
        base = gi * n_chain
        chains = range(n_chain)
        qs = [q_ref[pl.ds(pl.multiple_of((base + r) * tq, tq), tq), :] for r in chains]
        zero = (jnp.zeros((tq, tq), F32),) * n_chain
        cs, accs = tiles(qs, [base + r for r in chains], zero, zero, True)

        def live_max(cs, n):
            vals = [jnp.where(base + r - n >= 0, jnp.max(cs[r]), -jnp.inf) for r in chains]
            return functools.reduce(jnp.maximum, vals)

        def cond(st):
            n, _, _, cmax = st
            return (n <= base + n_chain - 1) & (cmax > SB_STOP_LOG2)

        def body(st):
            n, cs, accs, _ = st
            js = [base + r - n for r in chains]
            cs, accs = tiles(qs, [jnp.maximum(j, 0) for j in js], cs, accs, False, [j >= 0 for j in js])
            return n + 1, cs, accs, live_max(cs, n + 1)

        _, _, accs, _ = lax.while_loop(cond, body, (1, cs, accs, live_max(cs, 1)))
        for r in chains:
            o_ref[pl.ds(pl.multiple_of((base + r) * tq, tq), tq), :] = accs[r].astype(BF16)
        return carry

    lax.fori_loop(0, nq // n_chain, q_group, 0)


def _sb_attention(qkv, bsz, seq, n_heads, tq, n_chain):
    assert tq == HEAD_DIM == LANES
    assert seq % (tq * n_chain) == 0
    T = bsz * seq
    return pl.pallas_call(
        functools.partial(_sb_body, tq, n_chain),
        grid=(bsz, n_heads),
        in_specs=[
            pl.BlockSpec((seq, HEAD_DIM), lambda b, h: (b, h)),
            pl.BlockSpec((seq, HEAD_DIM), lambda b, h: (b, n_heads + h)),
            pl.BlockSpec((seq, HEAD_DIM), lambda b, h: (b, 2 * n_heads + h)),
        ],
        out_specs=pl.BlockSpec((seq, HEAD_DIM), lambda b, h: (b, h)),
        out_shape=jax.ShapeDtypeStruct((T, n_heads * HEAD_DIM), BF16),
        compiler_params=_params("parallel", "parallel"),
        name="sb_attention",
    )(qkv, qkv, qkv)


def _compress_body(n_chunk, t_ref, w1_ref, w2_ref, pos_ref, gain_ref, o_ref):
    kv = pl.program_id(0)
    a = jnp.concatenate([t_ref[pl.ds(p, n_chunk, stride=CMP_STRIDE), :].astype(BF16) for p in range(CMP_STRIDE)],
                        axis=1)
    half = a.shape[1]
    w1 = w1_ref[0]
    const = _dot(pos_ref[0], w1)[0:1, :]
    first = _dot(a, w1[:half, :])
    second = _dot(a, w1[half:, :])
    pre = first + pltpu.roll(second, n_chunk - 1, 0) + const
    hid = pre * jax.nn.sigmoid(pre)
    out = _dot(hid.astype(BF16), w2_ref[0])
    normed = _rms(out, gain_ref[...])
    o_ref[0, 0, 0] = jnp.where(kv == 0, normed, out).astype(BF16)


def _compress(t, w1, w2, pos, gain, bsz, seq, ng):
    dh = HEAD_DIM
    n_chunk = seq // CMP_STRIDE
    width = CMP_STRIDE * dh
    return pl.pallas_call(
        functools.partial(_compress_body, n_chunk),
        grid=(2, bsz, ng),
        in_specs=[
            pl.BlockSpec((seq, dh), lambda s, b, g: (b, s * ng + g)),
            pl.BlockSpec((1, 2 * width, dh), lambda s, b, g: (s, 0, 0)),
            pl.BlockSpec((1, dh, dh), lambda s, b, g: (s, 0, 0)),
            pl.BlockSpec((1, 8, 2 * width), lambda s, b, g: (s, 0, 0)),
            pl.BlockSpec((1, dh), lambda s, b, g: (0, 0)),
        ],
        out_specs=pl.BlockSpec((1, 1, 1, n_chunk, dh), lambda s, b, g: (s, b, g, 0, 0)),
        out_shape=jax.ShapeDtypeStruct((2, bsz, ng, n_chunk, dh), BF16),
        compiler_params=_params("parallel", "parallel", "parallel"),
        name="nsa_compress",
    )(t, w1, w2, pos, gain)


def _stack_heads(q_ref, hpg):
    return jnp.concatenate([q_ref[:, h * HEAD_DIM:(h + 1) * HEAD_DIM] for h in range(hpg)], axis=0)


def _gate_column(gates, col):
    lane = lax.broadcasted_iota(jnp.int32, gates.shape, 1)
    return jnp.sum(jnp.where(lane == col, gates, 0.0), axis=1, keepdims=True)


def _cmp_branch(qs, kc, vc, i, tq, hpg, n_sel, vt_scr, st_scr):
    n_chunk = kc.shape[0]
    s = _dot_nt(qs, kc)
    t1 = i * tq + lax.broadcasted_iota(jnp.int32, (tq, n_chunk), 0)
    c1 = lax.broadcasted_iota(jnp.int32, (tq, n_chunk), 1)
    valid1 = c1 * CMP_STRIDE + (CMP_LEN - 1) <= t1
    valid = jnp.concatenate([valid1] * hpg, axis=0)
    s = jnp.where(valid, s, NEG)
    m = jnp.max(s, axis=1, keepdims=True)
    e = jnp.where(valid, jnp.exp2(s - m), 0.0)
    l = jnp.sum(e, axis=1, keepdims=True)
    p = e / jnp.where(l > 0.0, l, 1.0)
    o = _dot(p.astype(BF16), vc)

    p_sum = p[0:tq, :]
    for h in range(1, hpg):
        p_sum = p_sum + p[h * tq:(h + 1) * tq, :]
    ci = lax.broadcasted_iota(jnp.int32, (n_chunk, LANES), 0)
    sj = lax.broadcasted_iota(jnp.int32, (n_chunk, LANES), 1)
    overlap = ((ci * CMP_STRIDE < (sj + 1) * SEL_LEN) & (ci * CMP_STRIDE + CMP_LEN > sj * SEL_LEN)
               & (ci < n_chunk - 1) & (sj < n_sel)).astype(BF16)
    imp = _split_dot(p_sum, overlap)
    t2 = i * tq + lax.broadcasted_iota(jnp.int32, (tq, LANES), 0)
    blk = lax.broadcasted_iota(jnp.int32, (tq, LANES), 1)
    cur = lax.shift_right_logical(t2, SEL_SHIFT)
    forced = (blk == 0) | (blk == cur) | (blk == cur - 1)
    blk_valid = blk * SEL_LEN <= t2
    score = jnp.where(forced, BIG, jnp.where(blk_valid, imp, -BIG))

    vt_scr[...] = score.T
    SUB = 8
    vts = [vt_scr[a * SUB:(a + 1) * SUB, :] for a in range(n_sel // SUB)]
    cnts = [jnp.zeros((SUB, tq), F32) for _ in vts]
    jrow = lax.broadcasted_iota(jnp.int32, (SUB, tq), 0)
    for b in range(n_sel):
        vb = vt_scr[b:b + 1, :]
        for a, vt in enumerate(vts):
            if b < a * SUB:
                ahead = vb >= vt
            elif b >= (a + 1) * SUB:
                ahead = vb > vt
            else:
                ahead = (vb > vt) | ((vb == vt) & (jrow > b - a * SUB))
            cnts[a] = cnts[a] + ahead.astype(F32)
    st_scr[...] = jnp.zeros_like(st_scr)
    for a, cnt in enumerate(cnts):
        st_scr[a * SUB:(a + 1) * SUB, :] = (cnt < float(min(SEL_TOPK, n_sel))).astype(F32)
    return o, st_scr[...].T.astype(BF16)


def _fill_values_and_ones(vo_scr, v_ref):
    vo_scr[:, :HEAD_DIM] = v_ref[...]
    vo_scr[:, HEAD_DIM:] = jnp.ones((vo_scr.shape[0], LANES), BF16)


def _sel_branch(qs, k_ref, vo_scr, sel, i, tq, tk, hpg, m_scr, acc_scr):
    dh = HEAD_DIM
    m_scr[...] = jnp.full_like(m_scr, NEG)
    acc_scr[...] = jnp.zeros_like(acc_scr)
    qpos = i * tq + lax.broadcasted_iota(jnp.int32, (tq, tk), 0)
    kcol = lax.broadcasted_iota(jnp.int32, (tq, tk), 1)
    eb = lax.broadcasted_iota(jnp.int32, (LANES, tk), 0)
    ek = lax.broadcasted_iota(jnp.int32, (LANES, tk), 1)
    n_tiles = ((i + 1) * tq + tk - 1) // tk

    def scores(j):
        k = k_ref[pl.ds(pl.multiple_of(j * tk, tk), tk), :]
        expand = (eb == lax.shift_right_logical(j * tk + ek, SEL_SHIFT)).astype(BF16)
        mask1 = (_dot(sel, expand) > 0.5) & (j * tk + kcol <= qpos)
        bias1 = jnp.where(mask1, 0.0, NEG).astype(BF16)
        return _dot_nt(qs, k).astype(BF16) + jnp.concatenate([bias1] * hpg, axis=0)

    def step(n, s):
        j = n_tiles - 1 - n
        s_next = scores(jnp.maximum(j - 1, 0))
        vo = vo_scr[pl.ds(pl.multiple_of(j * tk, tk), tk), :]
        chunks = [s[:, c * LANES:(c + 1) * LANES] for c in range(tk // LANES)]
        mx = chunks[0]
        for ch in chunks[1:]:
            mx = jnp.maximum(mx, ch)
        m_old = m_scr[...]
        m_new = jnp.maximum(m_old, jnp.max(mx, axis=1, keepdims=True).astype(F32))
        alpha = jnp.exp2(m_old - m_new)
        m_b = m_new.astype(BF16)
        p = jnp.concatenate([jnp.exp2(ch - m_b) for ch in chunks], axis=1)
        pv = _dot(p, vo)
        acc_scr[:, :dh] = alpha * acc_scr[:, :dh] + pv[:, :dh]
        acc_scr[:, dh:] = alpha * acc_scr[:, dh:] + pv[:, dh:]
        m_scr[...] = m_new
        return s_next

    lax.fori_loop(0, n_tiles, step, scores(n_tiles - 1))
    return acc_scr[:, :dh] / acc_scr[:, dh:]


def _win_branch(qs, k_ref, vo_scr, i, tq, hpg):
    dh = HEAD_DIM
    span = WINDOW + tq
    off = pl.multiple_of(jnp.maximum(i * tq - WINDOW, 0), tq)
    k = k_ref[pl.ds(off, span), :]
    vo = vo_scr[pl.ds(off, span), :]
    qpos = i * tq + lax.broadcasted_iota(jnp.int32, (tq, span), 0)
    kpos = off + lax.broadcasted_iota(jnp.int32, (tq, span), 1)
    bias1 = jnp.where((kpos <= qpos) & (qpos - kpos < WINDOW), 0.0, NEG).astype(BF16)
    s = _dot_nt(qs, k).astype(BF16) + jnp.concatenate([bias1] * hpg, axis=0)
    p = jnp.exp2(s - jnp.max(s, axis=1, keepdims=True))
    pv = _dot(p, vo)
    return pv[:, :dh] / pv[:, dh:]


def _nsa_attn_body(tq, tk, hpg, n_sel, q_ref, kc_ref, vc_ref, ks_ref, vs_ref, kw_ref, vw_ref, gates_ref, o_ref,
                   vos_scr, vow_scr, vt_scr, st_scr, m_scr, acc_scr):
    g = pl.program_id(1)
    i = pl.program_id(2)

    @pl.when(i == 0)
    def _():
        _fill_values_and_ones(vos_scr, vs_ref)
        _fill_values_and_ones(vow_scr, vw_ref)

    qs = _stack_heads(q_ref, hpg)
    o_cmp, sel = _cmp_branch(qs, kc_ref[0, 0, 0], vc_ref[0, 0, 0], i, tq, hpg, n_sel, vt_scr, st_scr)
    o_win = _win_branch(qs, kw_ref, vow_scr, i, tq, hpg)
    o_sel = _sel_branch(qs, ks_ref, vos_scr, sel, i, tq, tk, hpg, m_scr, acc_scr)
    gates = gates_ref[...]
    for h in range(hpg):
        rows = slice(h * tq, (h + 1) * tq)
        col = 3 * (g * hpg + h)
        o = (_gate_column(gates, col) * o_cmp[rows, :] + _gate_column(gates, col + 1) * o_sel[rows, :]
             + _gate_column(gates, col + 2) * o_win[rows, :])
        o_ref[:, h * HEAD_DIM:(h + 1) * HEAD_DIM] = o.astype(BF16)


def _nsa_attention(qkv, kvc, gates, col, bsz, seq, ng, hpg, tq, tk):
    T = bsz * seq
    nq = seq // tq
    gw = hpg * HEAD_DIM
    rows = hpg * tq
    n_chunk = kvc.shape[3]
    n_sel = seq // SEL_LEN
    assert n_sel <= LANES and n_sel % 8 == 0
    assert seq >= WINDOW + tq and WINDOW % tq == 0
    resident = lambda t: pl.BlockSpec((seq, HEAD_DIM), lambda b, g, i: (b, col(t) + g))
    compressed = lambda s: pl.BlockSpec((1, 1, 1, n_chunk, HEAD_DIM), lambda b, g, i: (s, b, g, 0, 0))
    return pl.pallas_call(
        functools.partial(_nsa_attn_body, tq, tk, hpg, n_sel),
        grid=(bsz, ng, nq),
        in_specs=[
            pl.BlockSpec((tq, gw), lambda b, g, i: (b * nq + i, g)),
            compressed(0), compressed(1),
            resident(ng + 2), resident(ng + 3), resident(ng + 4), resident(ng + 5),
            pl.BlockSpec((tq, LANES), lambda b, g, i: (b * nq + i, 0)),
        ],
        out_specs=pl.BlockSpec((tq, gw), lambda b, g, i: (b * nq + i, g)),
        out_shape=jax.ShapeDtypeStruct((T, ng * gw), BF16),
        scratch_shapes=[
            pltpu.VMEM((seq, HEAD_DIM + LANES), BF16), pltpu.VMEM((seq, HEAD_DIM + LANES), BF16),
            pltpu.VMEM((LANES, tq), F32), pltpu.VMEM((LANES, tq), F32),
            pltpu.VMEM((rows, LANES), F32), pltpu.VMEM((rows, HEAD_DIM + LANES), F32),
        ],
        compiler_params=_params("parallel", "parallel", "arbitrary"),
        name="nsa_attention",
    )(qkv, kvc, kvc, qkv, qkv, qkv, qkv, gates)


def _ident_prologue(a_ref):
    return a_ref[...]


TM_PROJ = 1024
TN_PROJ = 512
TM_OUT = 256
TM_MLP = 1024
TF_MLP = 512
TM_POOL = 256
TQ_SB = 128
SB_CHAINS = 8
TQ_NSA = 256
TK_SEL = 512


def _tile_gains(per_tile, tn):
    rows = [jnp.tile(g.astype(F32), tn // HEAD_DIM) if g is not None else jnp.ones((tn,), F32) for g in per_tile]
    return jnp.stack(rows).reshape(len(per_tile), 1, tn)


def _conv_layer(x, seq, norm_g, w_in, conv_w, w_out):
    n_tiles = w_in.shape[1] // TN_PROJ
    bcv = _norm_proj(x, norm_g, w_in, _tile_gains([None] * n_tiles, TN_PROJ),
                     ["plain"] * n_tiles, TM_PROJ, TN_PROJ)
    return _conv_out(bcv, conv_w, w_out.astype(BF16), x, seq, TM_OUT)


def _nsa_layer(x, bsz, seq, norm_g, w_in, q_gain, k_gain, cmp_pos, cmp_w1, cmp_w2, w_out):
    T, D = x.shape
    dh, G = HEAD_DIM, NSA_KV_GROUPS
    H = D // dh
    hpg = H // G
    gw = hpg * dh
    assert gw == TN_PROJ and G * dh == TN_PROJ
    scale = LOG2E * dh ** -0.5
    width = w_in.shape[1]
    n_tiles = -(-width // TN_PROJ)
    w_pad = jnp.pad(w_in, ((0, 0), (0, n_tiles * TN_PROJ - width))).astype(BF16)
    modes = ["norm"] * G + ["plain", "plain", "norm", "plain", "norm", "plain", "gate"]
    gains = _tile_gains([q_gain * scale] * G + [None, None, k_gain[1], None, k_gain[2], None, None], TN_PROJ)
    qkv, gates = _norm_proj(x, norm_g, w_pad, gains, modes, TM_PROJ, TN_PROJ)

    kvc_in = qkv[:, H * dh:H * dh + 2 * G * dh].astype(F32)
    pos = jnp.broadcast_to(cmp_pos.reshape(2, 1, CMP_LEN * dh), (2, 8, CMP_LEN * dh)).astype(BF16)
    kvc = _compress(kvc_in, cmp_w1.astype(BF16), cmp_w2.astype(BF16), pos, k_gain[0].reshape(1, dh), bsz, seq, G)

    col = lambda t: t * (TN_PROJ // dh)
    o = _nsa_attention(qkv, kvc, gates, col, bsz, seq, G, hpg, TQ_NSA, TK_SEL)
    spec = pl.BlockSpec((TM_OUT, H * dh), lambda i: (i, 0))
    return _res_proj([o], [spec], _ident_prologue, w_out.astype(BF16), x, TM_OUT, "nsa_out")


def _sb_layer(x, bsz, seq, norm_g, w_in, q_gain, k_gain, w_out):
    T, D = x.shape
    dh = HEAD_DIM
    H = D // dh
    scale = LOG2E * dh ** -0.5
    per = (H * dh) // TN_PROJ
    modes = ["norm"] * (2 * per) + ["plain"] * per
    gains = _tile_gains([q_gain * scale] * per + [k_gain] * per + [None] * per, TN_PROJ)
    qkv = _norm_proj(x, norm_g, w_in, gains, modes, TM_PROJ, TN_PROJ)
    o = _sb_attention(qkv, bsz, seq, H, TQ_SB, SB_CHAINS)
    spec = pl.BlockSpec((TM_OUT, H * dh), lambda i: (i, 0))
    return _res_proj([o], [spec], _ident_prologue, w_out.astype(BF16), x, TM_OUT, "sb_out")


def kernel(x, mix_norm, mlp_norm, mlp_w1, mlp_w2, conv_w_in, conv_w, conv_w_out, nsa_w_in, nsa_q_gain, nsa_k_gain, nsa_cmp_pos, nsa_cmp_w1, nsa_cmp_w2, nsa_w_out, pool_w, pool_scale, sb_w_in, sb_q_gain, sb_k_gain, sb_w_out):
    bsz, seq, d = x.shape
    depth = mix_norm.shape[0]
    n_mixers = 4
    xf = x.reshape(bsz * seq, d)
    for i in range(depth):
        kind, j = i % n_mixers, i // n_mixers
        if kind == 0:
            xf = _conv_layer(xf, seq, mix_norm[i], conv_w_in[j], conv_w[j], conv_w_out[j])
        elif kind == 1:
            xf = _nsa_layer(xf, bsz, seq, mix_norm[i], nsa_w_in[j], nsa_q_gain[j], nsa_k_gain[j],
                            nsa_cmp_pos[j], nsa_cmp_w1[j], nsa_cmp_w2[j], nsa_w_out[j])
        elif kind == 2:
            xf = _pool_mixer(xf, mix_norm[i], pool_w[j].astype(BF16), pool_scale[j], seq, TM_POOL)
        else:
            xf = _sb_layer(xf, bsz, seq, mix_norm[i], sb_w_in[j], sb_q_gain[j], sb_k_gain[j], sb_w_out[j])
        xf = _mlp(xf, mlp_norm[i], mlp_w1, mlp_w2, i, TM_MLP, TF_MLP)
    return xf.reshape(bsz, seq, d)
```

```python
import functools

import jax
import jax.numpy as jnp
from jax import lax
from jax.experimental import pallas as pl
from jax.experimental.pallas import tpu as pltpu

F32 = jnp.float32
BF16 = jnp.bfloat16

HEAD_DIM = 128
EPS = 1e-6
NEG = -1e30
BIG = 1e4
CONV_WIDTH = 3
NSA_KV_GROUPS = 4
CMP_LEN = 32
CMP_STRIDE = 16
SEL_LEN = 64
SEL_SHIFT = SEL_LEN.bit_length() - 1
assert 1 << SEL_SHIFT == SEL_LEN
SEL_TOPK = 16
WINDOW = 512
POOL_WINDOWS = (2, 4, 8, 16)
LANES = 128
VMEM_LIMIT = 56 * 1024 * 1024


def _params(*sem):
    return pltpu.CompilerParams(dimension_semantics=sem, vmem_limit_bytes=VMEM_LIMIT)


def _rms(xv, g):
    ms = jnp.mean(xv * xv, axis=-1, keepdims=True)
    return xv * lax.rsqrt(ms + EPS) * g


def _dot(a, b):
    return jnp.dot(a, b, preferred_element_type=F32)


def _dot_nt(a, b):
    return lax.dot_general(a, b, (((1,), (1,)), ((), ())), preferred_element_type=F32)


def _split_dot(a, b):
    hi = a.astype(BF16)
    r1 = a - hi.astype(F32)
    mid = r1.astype(BF16)
    lo = (r1 - mid.astype(F32)).astype(BF16)
    return _dot(hi, b) + _dot(mid, b) + _dot(lo, b)


def _norm_proj_body(modes, tn, has_gate, x_ref, g_ref, w_ref, gain_ref, *rest):
    if has_gate:
        o_ref, og_ref, h_scr = rest
    else:
        o_ref, h_scr = rest
    j = pl.program_id(1)

    @pl.when(j == 0)
    def _():
        h_scr[...] = _rms(x_ref[...], g_ref[...]).astype(BF16)

    acc = _dot(h_scr[...], w_ref[...].astype(BF16))

    def tiles_of(mode):
        return [t for t, m in enumerate(modes) if m == mode]

    def any_of(tiles):
        c = j == tiles[0]
        for t in tiles[1:]:
            c = c | (j == t)
        return c

    if tiles_of("plain"):
        @pl.when(any_of(tiles_of("plain")))
        def _():
            o_ref[...] = acc.astype(BF16)

    if tiles_of("norm"):
        @pl.when(any_of(tiles_of("norm")))
        def _():
            gain = gain_ref[0]
            for h in range(tn // HEAD_DIM):
                sl = slice(h * HEAD_DIM, (h + 1) * HEAD_DIM)
                o_ref[:, sl] = _rms(acc[:, sl], gain[:, sl]).astype(BF16)

    if tiles_of("gate"):
        @pl.when(any_of(tiles_of("gate")))
        def _():
            o_ref[...] = acc.astype(BF16)
            og_ref[...] = jax.nn.sigmoid(acc[:, :LANES])


def _norm_proj(x, g, w, gains, modes, tm, tn):
    T, D = x.shape
    N = w.shape[1]
    assert N == len(modes) * tn and T % tm == 0
    has_gate = "gate" in modes
    out_shape = [jax.ShapeDtypeStruct((T, N), BF16)]
    out_specs = [pl.BlockSpec((tm, tn), lambda i, j: (i, j))]
    if has_gate:
        out_shape.append(jax.ShapeDtypeStruct((T, LANES), F32))
        out_specs.append(pl.BlockSpec((tm, LANES), lambda i, j: (i, 0)))
    res = pl.pallas_call(
        functools.partial(_norm_proj_body, tuple(modes), tn, has_gate),
        grid=(T // tm, N // tn),
        in_specs=[
            pl.BlockSpec((tm, D), lambda i, j: (i, 0)),
            pl.BlockSpec((1, D), lambda i, j: (0, 0)),
            pl.BlockSpec((D, tn), lambda i, j: (0, j)),
            pl.BlockSpec((1, 1, tn), lambda i, j: (j, 0, 0)),
        ],
        out_specs=out_specs,
        out_shape=out_shape,
        scratch_shapes=[pltpu.VMEM((tm, D), BF16)],
        compiler_params=_params("parallel", "arbitrary"),
        name="norm_proj",
    )(x, g.reshape(1, D), w, gains)
    return res if has_gate else res[0]


def _res_proj_body(prologue, n_rows, *refs):
    row_refs = refs[:n_rows]
    w_ref, x_ref, o_ref, wb_scr = refs[n_rows:n_rows + 4]

    @pl.when(pl.program_id(0) == 0)
    def _():
        wb_scr[...] = w_ref[...].astype(BF16)

    a = prologue(*row_refs)
    o_ref[...] = x_ref[...] + _dot(a, wb_scr[...])


def _res_proj(rows, row_specs, prologue, w, x, tm, name):
    T, D = x.shape
    K = w.shape[0]
    return pl.pallas_call(
        functools.partial(_res_proj_body, prologue, len(rows)),
        grid=(T // tm,),
        in_specs=list(row_specs) + [
            pl.BlockSpec((K, D), lambda i: (0, 0), pipeline_mode=pl.Buffered(1)),
            pl.BlockSpec((tm, D), lambda i: (i, 0)),
        ],
        out_specs=pl.BlockSpec((tm, D), lambda i: (i, 0)),
        out_shape=jax.ShapeDtypeStruct((T, D), F32),
        scratch_shapes=[pltpu.VMEM((K, D), BF16)],
        compiler_params=_params("arbitrary"),
        name=name,
    )(*rows, w, x)


def _mlp_body(x_ref, g_ref, w1_ref, w2_ref, o_ref, h_scr):
    f = pl.program_id(1)

    @pl.when(f == 0)
    def _():
        xv = x_ref[...]
        h_scr[...] = _rms(xv, g_ref[...]).astype(BF16)
        o_ref[...] = xv

    a = jnp.maximum(_dot(h_scr[...], w1_ref[...].astype(BF16)), 0.0)
    o_ref[...] += _dot((a * a).astype(BF16), w2_ref[...].astype(BF16))


def _mlp(x, g, w1, w2, layer, tm, tf):
    T, D = x.shape
    FF = w1.shape[2]
    return pl.pallas_call(
        _mlp_body,
        grid=(T // tm, FF // tf),
        in_specs=[
            pl.BlockSpec((tm, D), lambda i, f: (i, 0)),
            pl.BlockSpec((1, D), lambda i, f: (0, 0)),
            pl.BlockSpec((None, D, tf), lambda i, f: (layer, 0, f)),
            pl.BlockSpec((None, tf, D), lambda i, f: (layer, f, 0)),
        ],
        out_specs=pl.BlockSpec((tm, D), lambda i, f: (i, 0)),
        out_shape=jax.ShapeDtypeStruct((T, D), F32),
        scratch_shapes=[pltpu.VMEM((tm, D), BF16)],
        compiler_params=_params("parallel", "arbitrary"),
        name="mlp",
    )(x, g.reshape(1, D), w1, w2)


HALO = 16


def _conv_prologue(tm, tiles_per_seq, b_ref, c_ref, v_ref, cp_ref, vp_ref, cw_ref):
    i = pl.program_id(0)
    first = (i % tiles_per_seq) == 0
    u = c_ref[...].astype(F32) * v_ref[...].astype(F32)
    up = cp_ref[...].astype(F32) * vp_ref[...].astype(F32)
    up = jnp.where(first, 0.0, up)
    row = lax.broadcasted_iota(jnp.int32, u.shape, 0)
    r1 = jnp.where(row == 0, up[HALO - 1:HALO, :], pltpu.roll(u, 1, 0))
    r2 = pltpu.roll(u, 2, 0)
    r2 = jnp.where(row == 0, up[HALO - 2:HALO - 1, :], jnp.where(row == 1, up[HALO - 1:HALO, :], r2))
    cw = cw_ref[...]
    y = cw[0:1, :] * r2 + cw[1:2, :] * r1 + cw[2:3, :] * u
    return (b_ref[...].astype(F32) * y).astype(BF16)


def _conv_out(bcv, conv_w, w_out, x, seq, tm):
    T, D = x.shape
    hb = tm // HALO
    prev = lambda col: (lambda i: (jnp.maximum(i * hb - 1, 0), col))
    specs = [
        pl.BlockSpec((tm, D), lambda i: (i, 0)),
        pl.BlockSpec((tm, D), lambda i: (i, 1)),
        pl.BlockSpec((tm, D), lambda i: (i, 2)),
        pl.BlockSpec((HALO, D), prev(1)),
        pl.BlockSpec((HALO, D), prev(2)),
        pl.BlockSpec((CONV_WIDTH, D), lambda i: (0, 0)),
    ]
    prologue = functools.partial(_conv_prologue, tm, seq // tm)
    return _res_proj([bcv, bcv, bcv, bcv, bcv, conv_w], specs, prologue, w_out, x, tm, "conv_out")


def _pool_body(tm, tiles_per_seq, x_ref, xp_ref, g_ref, w_ref, sc_ref, o_ref):
    i = pl.program_id(0)
    first = (i % tiles_per_seq) == 0
    xv = x_ref[...]
    g = g_ref[...]
    h = _rms(xv, g)
    hp = jnp.where(first, 0.0, _rms(xp_ref[...], g))
    pos = (i % tiles_per_seq) * tm + lax.broadcasted_iota(jnp.int32, (tm, 1), 0)
    cg = h.shape[1] // len(POOL_WINDOWS)
    for gi, win in enumerate(POOL_WINDOWS):
        sl = slice(gi * cg, (gi + 1) * cg)
        hg = h[:, sl]
        s = jnp.concatenate([hp[:, sl], hg], axis=0)
        k = 1
        while k < win:
            s = s + pltpu.roll(s, k, 0)
            k *= 2
        cnt = jnp.minimum(pos + 1, win).astype(F32)
        pooled = s[HALO:, :] / cnt - hg
        y = _dot(pooled.astype(BF16), w_ref[gi])
        o_ref[:, sl] = xv[:, sl] + y * sc_ref[:, sl]


def _pool_mixer(x, g, w, scale, seq, tm):
    T, D = x.shape
    ng, cg, _ = w.shape
    hb = tm // HALO
    return pl.pallas_call(
        functools.partial(_pool_body, tm, seq // tm),
        grid=(T // tm,),
        in_specs=[
            pl.BlockSpec((tm, D), lambda i: (i, 0)),
            pl.BlockSpec((HALO, D), lambda i: (jnp.maximum(i * hb - 1, 0), 0)),
            pl.BlockSpec((1, D), lambda i: (0, 0)),
            pl.BlockSpec((ng, cg, cg), lambda i: (0, 0, 0)),
            pl.BlockSpec((1, D), lambda i: (0, 0)),
        ],
        out_specs=pl.BlockSpec((tm, D), lambda i: (i, 0)),
        out_shape=jax.ShapeDtypeStruct((T, D), F32),
        compiler_params=_params("parallel"),
        name="pool_mixer",
    )(x, x, g.reshape(1, D), w, scale.reshape(1, D))


LOG2E = 1.4426950408889634
SB_STOP_LOG2 = -135.0


def _sb_body(tq, n_chain, q_ref, k_ref, v_ref, o_ref):
    nq = q_ref.shape[0] // tq
    row = lax.broadcasted_iota(jnp.int32, (tq, tq), 0)
    col = lax.broadcasted_iota(jnp.int32, (tq, tq), 1)
    before_diag = col < row
    r2 = lax.broadcasted_iota(jnp.int32, (2 * tq, 2 * tq), 0)
    c2 = lax.broadcasted_iota(jnp.int32, (2 * tq, 2 * tq), 1)
    key = jnp.where(r2 >= tq, r2 - tq, r2)
    sums = ((c2 >= tq) | (key > c2)).astype(BF16)

    def tiles(qs, js, cs, accs, diag, lives=None):
        chains = range(n_chain)
        offs = [pl.multiple_of(js[r] * tq, tq) for r in chains]
        zs = [_dot_nt(qs[r], k_ref[pl.ds(offs[r], tq), :]) for r in chains]
        log_1m = [jnp.log(1.0 + jnp.exp2(-jnp.abs(z))) * (-LOG2E) - jnp.maximum(z, 0.0) for z in zs]
        if diag:
            log_1m = [jnp.where(before_diag, x, 0.0) for x in log_1m]
        his = [x.astype(BF16) for x in log_1m]
        mids = [(x - hi.astype(F32)).astype(BF16) for x, hi in zip(log_1m, his)]
        ts = [_dot(jnp.concatenate([hi, mid], axis=1), sums) for hi, mid in zip(his, mids)]
        ws = [jnp.exp2(zs[r] + log_1m[r] + ts[r][:, :tq] + cs[r]) for r in chains]
        tile_sums = [t[:, tq:] for t in ts]
        if diag:
            ws = [jnp.where(before_diag, w, 0.0) for w in ws]
        if lives is not None:
            ws = [jnp.where(lives[r], ws[r], 0.0) for r in chains]
            tile_sums = [jnp.where(lives[r], tile_sums[r], 0.0) for r in chains]
        accs = tuple(accs[r] + _dot(ws[r].astype(BF16), v_ref[pl.ds(offs[r], tq), :]) for r in chains)
        return tuple(cs[r] + tile_sums[r] for r in chains), accs

    def q_group(gi, carry):
        base = gi * n_chain
        chains = range(n_chain)
        qs = [q_ref[pl.ds(pl.multiple_of((base + r) * tq, tq), tq), :] for r in chains]
        zero = (jnp.zeros((tq, tq), F32),) * n_chain
        cs, accs = tiles(qs, [base + r for r in chains], zero, zero, True)

        def live_max(cs, n):
            vals = [jnp.where(base + r - n >= 0, jnp.max(cs[r]), -jnp.inf) for r in chains]
            return functools.reduce(jnp.maximum, vals)

        def cond(st):
            n, _, _, cmax = st
            return (n <= base + n_chain - 1) & (cmax > SB_STOP_LOG2)

        def body(st):
            n, cs, accs, _ = st
            js = [base + r - n for r in chains]
            cs, accs = tiles(qs, [jnp.maximum(j, 0) for j in js], cs, accs, False, [j >= 0 for j in js])
            return n + 1, cs, accs, live_max(cs, n + 1)

        _, _, accs, _ = lax.while_loop(cond, body, (1, cs, accs, live_max(cs, 1)))
        for r in chains:
            o_ref[pl.ds(pl.multiple_of((base + r) * tq, tq), tq), :] = accs[r].astype(BF16)
        return carry

    lax.fori_loop(0, nq // n_chain, q_group, 0)


def _sb_attention(qkv, bsz, seq, n_heads, tq, n_chain):
    assert tq == HEAD_DIM == LANES
    assert seq % (tq * n_chain) == 0
    T = bsz * seq
    return pl.pallas_call(
        functools.partial(_sb_body, tq, n_chain),
        grid=(bsz, n_heads),
        in_specs=[
            pl.BlockSpec((seq, HEAD_DIM), lambda b, h: (b, h)),
            pl.BlockSpec((seq, HEAD_DIM), lambda b, h: (b, n_heads + h)),
            pl.BlockSpec((seq, HEAD_DIM), lambda b, h: (b, 2 * n_heads + h)),
        ],
        out_specs=pl.BlockSpec((seq, HEAD_DIM), lambda b, h: (b, h)),
        out_shape=jax.ShapeDtypeStruct((T, n_heads * HEAD_DIM), BF16),
        compiler_params=_params("parallel", "parallel"),
        name="sb_attention",
    )(qkv, qkv, qkv)


def _compress_body(n_chunk, t_ref, w1_ref, w2_ref, pos_ref, gain_ref, o_ref):
    kv = pl.program_id(0)
    a = jnp.concatenate([t_ref[pl.ds(p, n_chunk, stride=CMP_STRIDE), :].astype(BF16) for p in range(CMP_STRIDE)],
                        axis=1)
    half = a.shape[1]
    w1 = w1_ref[0]
    const = _dot(pos_ref[0], w1)[0:1, :]
    first = _dot(a, w1[:half, :])
    second = _dot(a, w1[half:, :])
    pre = first + pltpu.roll(second, n_chunk - 1, 0) + const
    hid = pre * jax.nn.sigmoid(pre)
    out = _dot(hid.astype(BF16), w2_ref[0])
    normed = _rms(out, gain_ref[...])
    o_ref[0, 0, 0] = jnp.where(kv == 0, normed, out).astype(BF16)


def _compress(t, w1, w2, pos, gain, bsz, seq, ng):
    dh = HEAD_DIM
    n_chunk = seq // CMP_STRIDE
    width = CMP_STRIDE * dh
    return pl.pallas_call(
        functools.partial(_compress_body, n_chunk),
        grid=(2, bsz, ng),
        in_specs=[
            pl.BlockSpec((seq, dh), lambda s, b, g: (b, s * ng + g)),
            pl.BlockSpec((1, 2 * width, dh), lambda s, b, g: (s, 0, 0)),
            pl.BlockSpec((1, dh, dh), lambda s, b, g: (s, 0, 0)),
            pl.BlockSpec((1, 8, 2 * width), lambda s, b, g: (s, 0, 0)),
            pl.BlockSpec((1, dh), lambda s, b, g: (0, 0)),
        ],
        out_specs=pl.BlockSpec((1, 1, 1, n_chunk, dh), lambda s, b, g: (s, b, g, 0, 0)),
        out_shape=jax.ShapeDtypeStruct((2, bsz, ng, n_chunk, dh), BF16),
        compiler_params=_params("parallel", "parallel", "parallel"),
        name="nsa_compress",
    )(t, w1, w2, pos, gain)


def _stack_heads(q_ref, hpg):
    return jnp.concatenate([q_ref[:, h * HEAD_DIM:(h + 1) * HEAD_DIM] for h in range(hpg)], axis=0)


def _gate_column(gates, col):
    lane = lax.broadcasted_iota(jnp.int32, gates.shape, 1)
    return jnp.sum(jnp.where(lane == col, gates, 0.0), axis=1, keepdims=True)


def _cmp_branch(qs, kc, vc, i, tq, hpg, n_sel, vt_scr, st_scr):
    n_chunk = kc.shape[0]
    s = _dot_nt(qs, kc)
    t1 = i * tq + lax.broadcasted_iota(jnp.int32, (tq, n_chunk), 0)
    c1 = lax.broadcasted_iota(jnp.int32, (tq, n_chunk), 1)
    valid1 = c1 * CMP_STRIDE + (CMP_LEN - 1) <= t1
    valid = jnp.concatenate([valid1] * hpg, axis=0)
    s = jnp.where(valid, s, NEG)
    m = jnp.max(s, axis=1, keepdims=True)
    e = jnp.where(valid, jnp.exp2(s - m), 0.0)
    l = jnp.sum(e, axis=1, keepdims=True)
    p = e * (1.0 / jnp.where(l > 0.0, l, 1.0))
    o = _dot(p.astype(BF16), vc)

    p_sum = p[0:tq, :]
    for h in range(1, hpg):
        p_sum = p_sum + p[h * tq:(h + 1) * tq, :]
    ci = lax.broadcasted_iota(jnp.int32, (n_chunk, LANES), 0)
    sj = lax.broadcasted_iota(jnp.int32, (n_chunk, LANES), 1)
    overlap = ((ci * CMP_STRIDE < (sj + 1) * SEL_LEN) & (ci * CMP_STRIDE + CMP_LEN > sj * SEL_LEN)
               & (ci < n_chunk - 1) & (sj < n_sel)).astype(BF16)
    imp = _split_dot(p_sum, overlap)
    t2 = i * tq + lax.broadcasted_iota(jnp.int32, (tq, LANES), 0)
    blk = lax.broadcasted_iota(jnp.int32, (tq, LANES), 1)
    cur = lax.shift_right_logical(t2, SEL_SHIFT)
    forced = (blk == 0) | (blk == cur) | (blk == cur - 1)
    blk_valid = blk * SEL_LEN <= t2
    score = jnp.where(forced, BIG, jnp.where(blk_valid, imp, -BIG))

    vt_scr[...] = score.T
    SUB = 8
    vts = [vt_scr[a * SUB:(a + 1) * SUB, :] for a in range(n_sel // SUB)]
    cnts = [jnp.zeros((SUB, tq), F32) for _ in vts]
    jrow = lax.broadcasted_iota(jnp.int32, (SUB, tq), 0)
    for b in range(n_sel):
        vb = vt_scr[b:b + 1, :]
        for a, vt in enumerate(vts):
            if b < a * SUB:
                ahead = vb >= vt
            elif b >= (a + 1) * SUB:
                ahead = vb > vt
            else:
                ahead = (vb > vt) | ((vb == vt) & (jrow > b - a * SUB))
            cnts[a] = cnts[a] + ahead.astype(F32)
    st_scr[...] = jnp.zeros_like(st_scr)
    for a, cnt in enumerate(cnts):
        st_scr[a * SUB:(a + 1) * SUB, :] = (cnt < float(min(SEL_TOPK, n_sel))).astype(F32)
    return o, st_scr[...].T.astype(BF16)


def _fill_values_and_ones(vo_scr, v_ref):
    vo_scr[:, :HEAD_DIM] = v_ref[...]
    vo_scr[:, HEAD_DIM:] = jnp.ones((vo_scr.shape[0], LANES), BF16)


def _sel_branch(qs, k_ref, vo_scr, sel, i, tq, tk, hpg, m_scr, acc_scr):
    dh = HEAD_DIM
    m_scr[...] = jnp.full_like(m_scr, NEG)
    acc_scr[...] = jnp.zeros_like(acc_scr)
    qpos = i * tq + lax.broadcasted_iota(jnp.int32, (tq, tk), 0)
    kcol = lax.broadcasted_iota(jnp.int32, (tq, tk), 1)
    eb = lax.broadcasted_iota(jnp.int32, (LANES, tk), 0)
    ek = lax.broadcasted_iota(jnp.int32, (LANES, tk), 1)
    n_tiles = ((i + 1) * tq + tk - 1) // tk

    def scores(j):
        k = k_ref[pl.ds(pl.multiple_of(j * tk, tk), tk), :]
        expand = (eb == lax.shift_right_logical(j * tk + ek, SEL_SHIFT)).astype(BF16)
        mask1 = (_dot(sel, expand) > 0.5) & (j * tk + kcol <= qpos)
        bias1 = jnp.where(mask1, 0.0, NEG).astype(BF16)
        return _dot_nt(qs, k).astype(BF16) + jnp.concatenate([bias1] * hpg, axis=0)

    def step(n, s):
        j = n_tiles - 1 - n
        s_next = scores(jnp.maximum(j - 1, 0))
        vo = vo_scr[pl.ds(pl.multiple_of(j * tk, tk), tk), :]
        chunks = [s[:, c * LANES:(c + 1) * LANES] for c in range(tk // LANES)]
        mx = chunks[0]
        for ch in chunks[1:]:
            mx = jnp.maximum(mx, ch)
        m_old = m_scr[...]
        m_new = jnp.maximum(m_old, jnp.max(mx, axis=1, keepdims=True).astype(F32))
        alpha = jnp.exp2(m_old - m_new)
        m_b = m_new.astype(BF16)
        p = jnp.concatenate([jnp.exp2(ch - m_b) for ch in chunks], axis=1)
        pv = _dot(p, vo)
        acc_scr[:, :dh] = alpha * acc_scr[:, :dh] + pv[:, :dh]
        acc_scr[:, dh:] = alpha * acc_scr[:, dh:] + pv[:, dh:]
        m_scr[...] = m_new
        return s_next

    lax.fori_loop(0, n_tiles, step, scores(n_tiles - 1))
    return acc_scr[:, :dh] / acc_scr[:, dh:]


def _win_branch(qs, k_ref, vo_scr, i, tq, hpg):
    dh = HEAD_DIM
    span = WINDOW + tq
    off = pl.multiple_of(jnp.maximum(i * tq - WINDOW, 0), tq)
    k = k_ref[pl.ds(off, span), :]
    vo = vo_scr[pl.ds(off, span), :]
    qpos = i * tq + lax.broadcasted_iota(jnp.int32, (tq, span), 0)
    kpos = off + lax.broadcasted_iota(jnp.int32, (tq, span), 1)
    bias1 = jnp.where((kpos <= qpos) & (qpos - kpos < WINDOW), 0.0, NEG).astype(BF16)
    s = _dot_nt(qs, k).astype(BF16) + jnp.concatenate([bias1] * hpg, axis=0)
    p = jnp.exp2(s - jnp.max(s, axis=1, keepdims=True))
    pv = _dot(p, vo)
    return pv[:, :dh] / pv[:, dh:]


def _nsa_attn_body(tq, tk, hpg, n_sel, q_ref, kc_ref, vc_ref, ks_ref, vs_ref, kw_ref, vw_ref, gates_ref, o_ref,
                   vos_scr, vow_scr, vt_scr, st_scr, m_scr, acc_scr):
    g = pl.program_id(1)
    i = pl.program_id(2)

    @pl.when(i == 0)
    def _():
        _fill_values_and_ones(vos_scr, vs_ref)
        _fill_values_and_ones(vow_scr, vw_ref)

    qs = _stack_heads(q_ref, hpg)
    o_cmp, sel = _cmp_branch(qs, kc_ref[0, 0, 0], vc_ref[0, 0, 0], i, tq, hpg, n_sel, vt_scr, st_scr)
    o_win = _win_branch(qs, kw_ref, vow_scr, i, tq, hpg)
    o_sel = _sel_branch(qs, ks_ref, vos_scr, sel, i, tq, tk, hpg, m_scr, acc_scr)
    gates = gates_ref[...]
    for h in range(hpg):
        rows = slice(h * tq, (h + 1) * tq)
        col = 3 * (g * hpg + h)
        o = (_gate_column(gates, col) * o_cmp[rows, :] + _gate_column(gates, col + 1) * o_sel[rows, :]
             + _gate_column(gates, col + 2) * o_win[rows, :])
        o_ref[:, h * HEAD_DIM:(h + 1) * HEAD_DIM] = o.astype(BF16)


def _nsa_attention(qkv, kvc, gates, col, bsz, seq, ng, hpg, tq, tk):
    T = bsz * seq
    nq = seq // tq
    gw = hpg * HEAD_DIM
    rows = hpg * tq
    n_chunk = kvc.shape[3]
    n_sel = seq // SEL_LEN
    assert n_sel <= LANES and n_sel % 8 == 0
    assert seq >= WINDOW + tq and WINDOW % tq == 0
    resident = lambda t: pl.BlockSpec((seq, HEAD_DIM), lambda b, g, i: (b, col(t) + g))
    compressed = lambda s: pl.BlockSpec((1, 1, 1, n_chunk, HEAD_DIM), lambda b, g, i: (s, b, g, 0, 0))
    return pl.pallas_call(
        functools.partial(_nsa_attn_body, tq, tk, hpg, n_sel),
        grid=(bsz, ng, nq),
        in_specs=[
            pl.BlockSpec((tq, gw), lambda b, g, i: (b * nq + i, g)),
            compressed(0), compressed(1),
            resident(ng + 2), resident(ng + 3), resident(ng + 4), resident(ng + 5),
            pl.BlockSpec((tq, LANES), lambda b, g, i: (b * nq + i, 0)),
        ],
        out_specs=pl.BlockSpec((tq, gw), lambda b, g, i: (b * nq + i, g)),
        out_shape=jax.ShapeDtypeStruct((T, ng * gw), BF16),
        scratch_shapes=[
            pltpu.VMEM((seq, HEAD_DIM + LANES), BF16), pltpu.VMEM((seq, HEAD_DIM + LANES), BF16),
            pltpu.VMEM((LANES, tq), F32), pltpu.VMEM((LANES, tq), F32),
            pltpu.VMEM((rows, LANES), F32), pltpu.VMEM((rows, HEAD_DIM + LANES), F32),
        ],
        compiler_params=_params("parallel", "parallel", "arbitrary"),
        name="nsa_attention",
    )(qkv, kvc, kvc, qkv, qkv, qkv, qkv, gates)


def _ident_prologue(a_ref):
    return a_ref[...]


TM_PROJ = 1024
TN_PROJ = 512
TM_OUT = 256
TM_MLP = 1024
TF_MLP = 512
TM_POOL = 256
TQ_SB = 128
SB_CHAINS = 8
TQ_NSA = 256
TK_SEL = 512


def _tile_gains(per_tile, tn):
    rows = [jnp.tile(g.astype(F32), tn // HEAD_DIM) if g is not None else jnp.ones((tn,), F32) for g in per_tile]
    return jnp.stack(rows).reshape(len(per_tile), 1, tn)


def _conv_layer(x, seq, norm_g, w_in, conv_w, w_out):
    n_tiles = w_in.shape[1] // TN_PROJ
    bcv = _norm_proj(x, norm_g, w_in, _tile_gains([None] * n_tiles, TN_PROJ),
                     ["plain"] * n_tiles, TM_PROJ, TN_PROJ)
    return _conv_out(bcv, conv_w, w_out, x, seq, TM_OUT)


def _nsa_layer(x, bsz, seq, norm_g, w_in, q_gain, k_gain, cmp_pos, cmp_w1, cmp_w2, w_out):
    T, D = x.shape
    dh, G = HEAD_DIM, NSA_KV_GROUPS
    H = D // dh
    hpg = H // G
    gw = hpg * dh
    assert gw == TN_PROJ and G * dh == TN_PROJ
    scale = LOG2E * dh ** -0.5
    width = w_in.shape[1]
    n_tiles = -(-width // TN_PROJ)
    w_pad = jnp.pad(w_in, ((0, 0), (0, n_tiles * TN_PROJ - width))).astype(BF16)
    modes = ["norm"] * G + ["plain", "plain", "norm", "plain", "norm", "plain", "gate"]
    gains = _tile_gains([q_gain * scale] * G + [None, None, k_gain[1], None, k_gain[2], None, None], TN_PROJ)
    qkv, gates = _norm_proj(x, norm_g, w_pad, gains, modes, TM_PROJ, TN_PROJ)

    kvc_in = qkv[:, H * dh:H * dh + 2 * G * dh].astype(F32)
    pos = jnp.broadcast_to(cmp_pos.reshape(2, 1, CMP_LEN * dh), (2, 8, CMP_LEN * dh)).astype(BF16)
    kvc = _compress(kvc_in, cmp_w1.astype(BF16), cmp_w2.astype(BF16), pos, k_gain[0].reshape(1, dh), bsz, seq, G)

    col = lambda t: t * (TN_PROJ // dh)
    o = _nsa_attention(qkv, kvc, gates, col, bsz, seq, G, hpg, TQ_NSA, TK_SEL)
    spec = pl.BlockSpec((TM_OUT, H * dh), lambda i: (i, 0))
    return _res_proj([o], [spec], _ident_prologue, w_out, x, TM_OUT, "nsa_out")


def _sb_layer(x, bsz, seq, norm_g, w_in, q_gain, k_gain, w_out):
    T, D = x.shape
    dh = HEAD_DIM
    H = D // dh
    scale = LOG2E * dh ** -0.5
    per = (H * dh) // TN_PROJ
    modes = ["norm"] * (2 * per) + ["plain"] * per
    gains = _tile_gains([q_gain * scale] * per + [k_gain] * per + [None] * per, TN_PROJ)
    qkv = _norm_proj(x, norm_g, w_in, gains, modes, TM_PROJ, TN_PROJ)
    o = _sb_attention(qkv, bsz, seq, H, TQ_SB, SB_CHAINS)
    spec = pl.BlockSpec((TM_OUT, H * dh), lambda i: (i, 0))
    return _res_proj([o], [spec], _ident_prologue, w_out, x, TM_OUT, "sb_out")


def kernel(x, mix_norm, mlp_norm, mlp_w1, mlp_w2, conv_w_in, conv_w, conv_w_out, nsa_w_in, nsa_q_gain, nsa_k_gain, nsa_cmp_pos, nsa_cmp_w1, nsa_cmp_w2, nsa_w_out, pool_w, pool_scale, sb_w_in, sb_q_gain, sb_k_gain, sb_w_out):
    bsz, seq, d = x.shape
    depth = mix_norm.shape[0]
    n_mixers = 4
    xf = x.reshape(bsz * seq, d)
    for i in range(depth):
        kind, j = i % n_mixers, i // n_mixers
        if kind == 0:
            xf = _conv_layer(xf, seq, mix_norm[i], conv_w_in[j], conv_w[j], conv_w_out[j])
        elif kind == 1:
            xf = _nsa_layer(xf, bsz, seq, mix_norm[i], nsa_w_in[j], nsa_q_gain[j], nsa_k_gain[j],
                            nsa_cmp_pos[j], nsa_cmp_w1[j], nsa_cmp_w2[j], nsa_w_out[j])
        elif kind == 2:
            xf = _pool_mixer(xf, mix_norm[i], pool_w[j].astype(BF16), pool_scale[j], seq, TM_POOL)
        else:
            xf = _sb_layer(xf, bsz, seq, mix_norm[i], sb_w_in[j], sb_q_gain[j], sb_k_gain[j], sb_w_out[j])
        xf = _mlp(xf, mlp_norm[i], mlp_w1, mlp_w2, i, TM_MLP, TF_MLP)
    return xf.reshape(bsz, seq, d)
```

```python
import functools

import jax
import jax.numpy as jnp
from jax import lax
from jax.experimental import pallas as pl
from jax.experimental.pallas import tpu as pltpu

F32 = jnp.float32
BF16 = jnp.bfloat16

HEAD_DIM = 128
EPS = 1e-6
NEG = -1e30
BIG = 1e4
CONV_WIDTH = 3
NSA_KV_GROUPS = 4
CMP_LEN = 32
CMP_STRIDE = 16
SEL_LEN = 64
SEL_SHIFT = SEL_LEN.bit_length() - 1
assert 1 << SEL_SHIFT == SEL_LEN
SEL_TOPK = 16
WINDOW = 512
POOL_WINDOWS = (2, 4, 8, 16)
LANES = 128
VMEM_LIMIT = 56 * 1024 * 1024


def _params(*sem):
    return pltpu.CompilerParams(dimension_semantics=sem, vmem_limit_bytes=VMEM_LIMIT)


def _rms(xv, g):
    ms = jnp.mean(xv * xv, axis=-1, keepdims=True)
    return xv * lax.rsqrt(ms + EPS) * g


def _dot(a, b):
    return jnp.dot(a, b, preferred_element_type=F32)


def _dot_nt(a, b):
    return lax.dot_general(a, b, (((1,), (1,)), ((), ())), preferred_element_type=F32)


def _split_dot(a, b):
    hi = a.astype(BF16)
    r1 = a - hi.astype(F32)
    mid = r1.astype(BF16)
    lo = (r1 - mid.astype(F32)).astype(BF16)
    return _dot(hi, b) + _dot(mid, b) + _dot(lo, b)


def _norm_proj_body(modes, tn, has_gate, x_ref, g_ref, w_ref, gain_ref, *rest):
    if has_gate:
        o_ref, og_ref, h_scr = rest
    else:
        o_ref, h_scr = rest
    j = pl.program_id(1)

    @pl.when(j == 0)
    def _():
        h_scr[...] = _rms(x_ref[...], g_ref[...]).astype(BF16)

    acc = _dot(h_scr[...], w_ref[...].astype(BF16))

    def tiles_of(mode):
        return [t for t, m in enumerate(modes) if m == mode]

    def any_of(tiles):
        c = j == tiles[0]
        for t in tiles[1:]:
            c = c | (j == t)
        return c

    if tiles_of("plain"):
        @pl.when(any_of(tiles_of("plain")))
        def _():
            o_ref[...] = acc.astype(BF16)

    if tiles_of("norm"):
        @pl.when(any_of(tiles_of("norm")))
        def _():
            gain = gain_ref[0]
            for h in range(tn // HEAD_DIM):
                sl = slice(h * HEAD_DIM, (h + 1) * HEAD_DIM)
                o_ref[:, sl] = _rms(acc[:, sl], gain[:, sl]).astype(BF16)

    if tiles_of("gate"):
        @pl.when(any_of(tiles_of("gate")))
        def _():
            o_ref[...] = acc.astype(BF16)
            og_ref[...] = jax.nn.sigmoid(acc[:, :LANES])


def _norm_proj(x, g, w, gains, modes, tm, tn):
    T, D = x.shape
    N = w.shape[1]
    assert N == len(modes) * tn and T % tm == 0
    has_gate = "gate" in modes
    out_shape = [jax.ShapeDtypeStruct((T, N), BF16)]
    out_specs = [pl.BlockSpec((tm, tn), lambda i, j: (i, j))]
    if has_gate:
        out_shape.append(jax.ShapeDtypeStruct((T, LANES), F32))
        out_specs.append(pl.BlockSpec((tm, LANES), lambda i, j: (i, 0)))
    res = pl.pallas_call(
        functools.partial(_norm_proj_body, tuple(modes), tn, has_gate),
        grid=(T // tm, N // tn),
        in_specs=[
            pl.BlockSpec((tm, D), lambda i, j: (i, 0)),
            pl.BlockSpec((1, D), lambda i, j: (0, 0)),
            pl.BlockSpec((D, tn), lambda i, j: (0, j)),
            pl.BlockSpec((1, 1, tn), lambda i, j: (j, 0, 0)),
        ],
        out_specs=out_specs,
        out_shape=out_shape,
        scratch_shapes=[pltpu.VMEM((tm, D), BF16)],
        compiler_params=_params("parallel", "arbitrary"),
        name="norm_proj",
    )(x, g.reshape(1, D), w, gains)
    return res if has_gate else res[0]


def _res_proj_body(prologue, n_rows, *refs):
    row_refs = refs[:n_rows]
    w_ref, x_ref, o_ref, wb_scr = refs[n_rows:n_rows + 4]

    @pl.when(pl.program_id(0) == 0)
    def _():
        wb_scr[...] = w_ref[...].astype(BF16)

    a = prologue(*row_refs)
    o_ref[...] = x_ref[...] + _dot(a, wb_scr[...])


def _res_proj(rows, row_specs, prologue, w, x, tm, name):
    T, D = x.shape
    K = w.shape[0]
    return pl.pallas_call(
        functools.partial(_res_proj_body, prologue, len(rows)),
        grid=(T // tm,),
        in_specs=list(row_specs) + [
            pl.BlockSpec((K, D), lambda i: (0, 0), pipeline_mode=pl.Buffered(1)),
            pl.BlockSpec((tm, D), lambda i: (i, 0)),
        ],
        out_specs=pl.BlockSpec((tm, D), lambda i: (i, 0)),
        out_shape=jax.ShapeDtypeStruct((T, D), F32),
        scratch_shapes=[pltpu.VMEM((K, D), BF16)],
        compiler_params=_params("arbitrary"),
        name=name,
    )(*rows, w, x)


def _mlp_body(x_ref, g_ref, w1_ref, w2_ref, o_ref, h_scr):
    f = pl.program_id(1)

    @pl.when(f == 0)
    def _():
        xv = x_ref[...]
        h_scr[...] = _rms(xv, g_ref[...]).astype(BF16)
        o_ref[...] = xv

    a = jnp.maximum(_dot(h_scr[...], w1_ref[...].astype(BF16)), 0.0)
    o_ref[...] += _dot((a * a).astype(BF16), w2_ref[...].astype(BF16))


def _mlp(x, g, w1, w2, layer, tm, tf):
    T, D = x.shape
    FF = w1.shape[2]
    return pl.pallas_call(
        _mlp_body,
        grid=(T // tm, FF // tf),
        in_specs=[
            pl.BlockSpec((tm, D), lambda i, f: (i, 0)),
            pl.BlockSpec((1, D), lambda i, f: (0, 0)),
            pl.BlockSpec((None, D, tf), lambda i, f: (layer, 0, f)),
            pl.BlockSpec((None, tf, D), lambda i, f: (layer, f, 0)),
        ],
        out_specs=pl.BlockSpec((tm, D), lambda i, f: (i, 0)),
        out_shape=jax.ShapeDtypeStruct((T, D), F32),
        scratch_shapes=[pltpu.VMEM((tm, D), BF16)],
        compiler_params=_params("parallel", "arbitrary"),
        name="mlp",
    )(x, g.reshape(1, D), w1, w2)


HALO = 16


def _conv_prologue(tm, tiles_per_seq, b_ref, c_ref, v_ref, cp_ref, vp_ref, cw_ref):
    i = pl.program_id(0)
    first = (i % tiles_per_seq) == 0
    u = c_ref[...].astype(F32) * v_ref[...].astype(F32)
    up = cp_ref[...].astype(F32) * vp_ref[...].astype(F32)
    up = jnp.where(first, 0.0, up)
    row = lax.broadcasted_iota(jnp.int32, u.shape, 0)
    r1 = jnp.where(row == 0, up[HALO - 1:HALO, :], pltpu.roll(u, 1, 0))
    r2 = pltpu.roll(u, 2, 0)
    r2 = jnp.where(row == 0, up[HALO - 2:HALO - 1, :], jnp.where(row == 1, up[HALO - 1:HALO, :], r2))
    cw = cw_ref[...]
    y = cw[0:1, :] * r2 + cw[1:2, :] * r1 + cw[2:3, :] * u
    return (b_ref[...].astype(F32) * y).astype(BF16)


def _conv_out(bcv, conv_w, w_out, x, seq, tm):
    T, D = x.shape
    hb = tm // HALO
    prev = lambda col: (lambda i: (jnp.maximum(i * hb - 1, 0), col))
    specs = [
        pl.BlockSpec((tm, D), lambda i: (i, 0)),
        pl.BlockSpec((tm, D), lambda i: (i, 1)),
        pl.BlockSpec((tm, D), lambda i: (i, 2)),
        pl.BlockSpec((HALO, D), prev(1)),
        pl.BlockSpec((HALO, D), prev(2)),
        pl.BlockSpec((CONV_WIDTH, D), lambda i: (0, 0)),
    ]
    prologue = functools.partial(_conv_prologue, tm, seq // tm)
    return _res_proj([bcv, bcv, bcv, bcv, bcv, conv_w], specs, prologue, w_out, x, tm, "conv_out")


def _pool_body(tm, tiles_per_seq, x_ref, xp_ref, g_ref, w_ref, sc_ref, o_ref):
    i = pl.program_id(0)
    first = (i % tiles_per_seq) == 0
    xv = x_ref[...]
    g = g_ref[...]
    h = _rms(xv, g)
    hp = jnp.where(first, 0.0, _rms(xp_ref[...], g))
    pos = (i % tiles_per_seq) * tm + lax.broadcasted_iota(jnp.int32, (tm, 1), 0)
    cg = h.shape[1] // len(POOL_WINDOWS)
    for gi, win in enumerate(POOL_WINDOWS):
        sl = slice(gi * cg, (gi + 1) * cg)
        hg = h[:, sl]
        s = jnp.concatenate([hp[:, sl], hg], axis=0)
        k = 1
        while k < win:
            s = s + pltpu.roll(s, k, 0)
            k *= 2
        cnt = jnp.minimum(pos + 1, win).astype(F32)
        pooled = s[HALO:, :] / cnt - hg
        y = _dot(pooled.astype(BF16), w_ref[gi])
        o_ref[:, sl] = xv[:, sl] + y * sc_ref[:, sl]


def _pool_mixer(x, g, w, scale, seq, tm):
    T, D = x.shape
    ng, cg, _ = w.shape
    hb = tm // HALO
    return pl.pallas_call(
        functools.partial(_pool_body, tm, seq // tm),
        grid=(T // tm,),
        in_specs=[
            pl.BlockSpec((tm, D), lambda i: (i, 0)),
            pl.BlockSpec((HALO, D), lambda i: (jnp.maximum(i * hb - 1, 0), 0)),
            pl.BlockSpec((1, D), lambda i: (0, 0)),
            pl.BlockSpec((ng, cg, cg), lambda i: (0, 0, 0)),
            pl.BlockSpec((1, D), lambda i: (0, 0)),
        ],
        out_specs=pl.BlockSpec((tm, D), lambda i: (i, 0)),
        out_shape=jax.ShapeDtypeStruct((T, D), F32),
        compiler_params=_params("parallel"),
        name="pool_mixer",
    )(x, x, g.reshape(1, D), w, scale.reshape(1, D))


LOG2E = 1.4426950408889634
SB_STOP_LOG2 = -135.0


def _sb_body(tq, n_chain, q_ref, k_ref, v_ref, o_ref):
    nq = q_ref.shape[0] // tq
    row = lax.broadcasted_iota(jnp.int32, (tq, tq), 0)
    col = lax.broadcasted_iota(jnp.int32, (tq, tq), 1)
    before_diag = col < row
    r2 = lax.broadcasted_iota(jnp.int32, (2 * tq, 2 * tq), 0)
    c2 = lax.broadcasted_iota(jnp.int32, (2 * tq, 2 * tq), 1)
    key = jnp.where(r2 >= tq, r2 - tq, r2)
    sums = ((c2 >= tq) | (key > c2)).astype(BF16)

    def tiles(qs, js, cs, accs, diag, lives=None):
        chains = range(n_chain)
        offs = [pl.multiple_of(js[r] * tq, tq) for r in chains]
        zs = [_dot_nt(qs[r], k_ref[pl.ds(offs[r], tq), :]) for r in chains]
        log_1m = [jnp.log(1.0 + jnp.exp2(-jnp.abs(z))) * (-LOG2E) - jnp.maximum(z, 0.0) for z in zs]
        if diag:
            log_1m = [jnp.where(before_diag, x, 0.0) for x in log_1m]
        his = [x.astype(BF16) for x in log_1m]
        mids = [(x - hi.astype(F32)).astype(BF16) for x, hi in zip(log_1m, his)]
        ts = [_dot(jnp.concatenate([hi, mid], axis=1), sums) for hi, mid in zip(his, mids)]
        ws = [jnp.exp2(zs[r] + log_1m[r] + ts[r][:, :tq] + cs[r]) for r in chains]
        tile_sums = [t[:, tq:] for t in ts]
        if diag:
            ws = [jnp.where(before_diag, w, 0.0) for w in ws]
        if lives is not None:
            ws = [jnp.where(lives[r], ws[r], 0.0) for r in chains]
            tile_sums = [jnp.where(lives[r], tile_sums[r], 0.0) for r in chains]
        accs = tuple(accs[r] + _dot(ws[r].astype(BF16), v_ref[pl.ds(offs[r], tq), :]) for r in chains)
        return tuple(cs[r] + tile_sums[r] for r in chains), accs

    def q_group(gi, carry):
        base = gi * n_chain
        chains = range(n_chain)
        qs = [q_ref[pl.ds(pl.multiple_of((base + r) * tq, tq), tq), :] for r in chains]
        zero = (jnp.zeros((tq, tq), F32),) * n_chain
        cs, accs = tiles(qs, [base + r for r in chains], zero, zero, True)

        def live_max(cs, n):
            vals = [jnp.where(base + r - n >= 0, jnp.max(cs[r]), -jnp.inf) for r in chains]
            return functools.reduce(jnp.maximum, vals)

        def cond(st):
            n, _, _, cmax = st
            return (n <= base + n_chain - 1) & (cmax > SB_STOP_LOG2)

        def body(st):
            n, cs, accs, _ = st
            js = [base + r - n for r in chains]
            cs, accs = tiles(qs, [jnp.maximum(j, 0) for j in js], cs, accs, False, [j >= 0 for j in js])
            return n + 1, cs, accs, live_max(cs, n + 1)

        _, _, accs, _ = lax.while_loop(cond, body, (1, cs, accs, live_max(cs, 1)))
        for r in chains:
            o_ref[pl.ds(pl.multiple_of((base + r) * tq, tq), tq), :] = accs[r].astype(BF16)
        return carry

    lax.fori_loop(0, nq // n_chain, q_group, 0)


def _sb_attention(qkv, bsz, seq, n_heads, tq, n_chain):
    assert tq == HEAD_DIM == LANES
    assert seq % (tq * n_chain) == 0
    T = bsz * seq
    return pl.pallas_call(
        functools.partial(_sb_body, tq, n_chain),
        grid=(bsz, n_heads),
        in_specs=[
            pl.BlockSpec((seq, HEAD_DIM), lambda b, h: (b, h)),
            pl.BlockSpec((seq, HEAD_DIM), lambda b, h: (b, n_heads + h)),
            pl.BlockSpec((seq, HEAD_DIM), lambda b, h: (b, 2 * n_heads + h)),
        ],
        out_specs=pl.BlockSpec((seq, HEAD_DIM), lambda b, h: (b, h)),
        out_shape=jax.ShapeDtypeStruct((T, n_heads * HEAD_DIM), BF16),
        compiler_params=_params("parallel", "parallel"),
        name="sb_attention",
    )(qkv, qkv, qkv)


def _compress_body(n_chunk, t_ref, w1_ref, w2_ref, pos_ref, gain_ref, o_ref):
    kv = pl.program_id(0)
    a = jnp.concatenate([t_ref[pl.ds(p, n_chunk, stride=CMP_STRIDE), :].astype(BF16) for p in range(CMP_STRIDE)],
                        axis=1)
    half = a.shape[1]
    w1 = w1_ref[0]
    const = _dot(pos_ref[0], w1)[0:1, :]
    first = _dot(a, w1[:half, :])
    second = _dot(a, w1[half:, :])
    pre = first + pltpu.roll(second, n_chunk - 1, 0) + const
    hid = pre * jax.nn.sigmoid(pre)
    out = _dot(hid.astype(BF16), w2_ref[0])
    normed = _rms(out, gain_ref[...])
    o_ref[0, 0, 0] = jnp.where(kv == 0, normed, out).astype(BF16)


def _compress(t, w1, w2, pos, gain, bsz, seq, ng):
    dh = HEAD_DIM
    n_chunk = seq // CMP_STRIDE
    width = CMP_STRIDE * dh
    return pl.pallas_call(
        functools.partial(_compress_body, n_chunk),
        grid=(2, bsz, ng),
        in_specs=[
            pl.BlockSpec((seq, dh), lambda s, b, g: (b, s * ng + g)),
            pl.BlockSpec((1, 2 * width, dh), lambda s, b, g: (s, 0, 0)),
            pl.BlockSpec((1, dh, dh), lambda s, b, g: (s, 0, 0)),
            pl.BlockSpec((1, 8, 2 * width), lambda s, b, g: (s, 0, 0)),
            pl.BlockSpec((1, dh), lambda s, b, g: (0, 0)),
        ],
        out_specs=pl.BlockSpec((1, 1, 1, n_chunk, dh), lambda s, b, g: (s, b, g, 0, 0)),
        out_shape=jax.ShapeDtypeStruct((2, bsz, ng, n_chunk, dh), BF16),
        compiler_params=_params("parallel", "parallel", "parallel"),
        name="nsa_compress",
    )(t, w1, w2, pos, gain)


def _stack_heads(q_ref, hpg):
    return jnp.concatenate([q_ref[:, h * HEAD_DIM:(h + 1) * HEAD_DIM] for h in range(hpg)], axis=0)


def _gate_column(gates, col):
    lane = lax.broadcasted_iota(jnp.int32, gates.shape, 1)
    return jnp.sum(jnp.where(lane == col, gates, 0.0), axis=1, keepdims=True)


def _cmp_branch(qs, kc, vc, i, tq, hpg, n_sel, vt_scr, st_scr):
    n_chunk = kc.shape[0]
    s = _dot_nt(qs, kc)
    t1 = i * tq + lax.broadcasted_iota(jnp.int32, (tq, n_chunk), 0)
    c1 = lax.broadcasted_iota(jnp.int32, (tq, n_chunk), 1)
    valid1 = c1 * CMP_STRIDE + (CMP_LEN - 1) <= t1
    valid = jnp.concatenate([valid1] * hpg, axis=0)
    s = jnp.where(valid, s, NEG)
    m = jnp.max(s, axis=1, keepdims=True)
    e = jnp.where(valid, jnp.exp2(s - m), 0.0)
    l = jnp.sum(e, axis=1, keepdims=True)
    p = e * (1.0 / jnp.where(l > 0.0, l, 1.0))
    o = _dot(p.astype(BF16), vc)

    p_sum = p[0:tq, :]
    for h in range(1, hpg):
        p_sum = p_sum + p[h * tq:(h + 1) * tq, :]
    ci = lax.broadcasted_iota(jnp.int32, (n_chunk, LANES), 0)
    sj = lax.broadcasted_iota(jnp.int32, (n_chunk, LANES), 1)
    overlap = ((ci * CMP_STRIDE < (sj + 1) * SEL_LEN) & (ci * CMP_STRIDE + CMP_LEN > sj * SEL_LEN)
               & (ci < n_chunk - 1) & (sj < n_sel)).astype(BF16)
    imp = _split_dot(p_sum, overlap)
    t2 = i * tq + lax.broadcasted_iota(jnp.int32, (tq, LANES), 0)
    blk = lax.broadcasted_iota(jnp.int32, (tq, LANES), 1)
    cur = lax.shift_right_logical(t2, SEL_SHIFT)
    forced = (blk == 0) | (blk == cur) | (blk == cur - 1)
    blk_valid = blk * SEL_LEN <= t2
    score = jnp.where(forced, BIG, jnp.where(blk_valid, imp, -BIG))

    vt_scr[...] = score.T
    SUB = 8
    vts = [vt_scr[a * SUB:(a + 1) * SUB, :] for a in range(n_sel // SUB)]
    cnts = [jnp.zeros((SUB, tq), F32) for _ in vts]
    jrow = lax.broadcasted_iota(jnp.int32, (SUB, tq), 0)
    for b in range(n_sel):
        vb = vt_scr[b:b + 1, :]
        for a, vt in enumerate(vts):
            if b < a * SUB:
                ahead = vb >= vt
            elif b >= (a + 1) * SUB:
                ahead = vb > vt
            else:
                ahead = (vb > vt) | ((vb == vt) & (jrow > b - a * SUB))
            cnts[a] = cnts[a] + ahead.astype(F32)
    st_scr[...] = jnp.zeros_like(st_scr)
    for a, cnt in enumerate(cnts):
        st_scr[a * SUB:(a + 1) * SUB, :] = (cnt < float(min(SEL_TOPK, n_sel))).astype(F32)
    return o, st_scr[...].T.astype(BF16)


def _fill_values_and_ones(vo_scr, v_ref):
    vo_scr[:, :HEAD_DIM] = v_ref[...]
    vo_scr[:, HEAD_DIM:] = jnp.ones((vo_scr.shape[0], LANES), BF16)


def _sel_branch(qs, k_ref, vo_scr, sel, i, tq, tk, hpg, m_scr, acc_scr):
    dh = HEAD_DIM
    m_scr[...] = jnp.full_like(m_scr, NEG)
    acc_scr[...] = jnp.zeros_like(acc_scr)
    qpos = i * tq + lax.broadcasted_iota(jnp.int32, (tq, tk), 0)
    kcol = lax.broadcasted_iota(jnp.int32, (tq, tk), 1)
    eb = lax.broadcasted_iota(jnp.int32, (LANES, tk), 0)
    ek = lax.broadcasted_iota(jnp.int32, (LANES, tk), 1)
    n_tiles = ((i + 1) * tq + tk - 1) // tk

    def scores(j):
        k = k_ref[pl.ds(pl.multiple_of(j * tk, tk), tk), :]
        expand = (eb == lax.shift_right_logical(j * tk + ek, SEL_SHIFT)).astype(BF16)
        mask1 = (_dot(sel, expand) > 0.5) & (j * tk + kcol <= qpos)
        bias1 = jnp.where(mask1, 0.0, NEG).astype(BF16)
        return _dot_nt(qs, k).astype(BF16) + jnp.concatenate([bias1] * hpg, axis=0)

    def step(n, s):
        j = n_tiles - 1 - n
        s_next = scores(jnp.maximum(j - 1, 0))
        vo = vo_scr[pl.ds(pl.multiple_of(j * tk, tk), tk), :]
        chunks = [s[:, c * LANES:(c + 1) * LANES] for c in range(tk // LANES)]
        mx = chunks[0]
        for ch in chunks[1:]:
            mx = jnp.maximum(mx, ch)
        m_old = m_scr[...]
        m_new = jnp.maximum(m_old, jnp.max(mx, axis=1, keepdims=True).astype(F32))
        alpha = jnp.exp2(m_old - m_new)
        m_b = m_new.astype(BF16)
        p = jnp.concatenate([jnp.exp2(ch - m_b) for ch in chunks], axis=1)
        pv = _dot(p, vo)
        acc_scr[:, :dh] = alpha * acc_scr[:, :dh] + pv[:, :dh]
        acc_scr[:, dh:] = alpha * acc_scr[:, dh:] + pv[:, dh:]
        m_scr[...] = m_new
        return s_next

    lax.fori_loop(0, n_tiles, step, scores(n_tiles - 1))
    return acc_scr[:, :dh] / acc_scr[:, dh:]


def _win_branch(qs, k_ref, vo_scr, i, tq, hpg):
    dh = HEAD_DIM
    span = WINDOW + tq
    off = pl.multiple_of(jnp.maximum(i * tq - WINDOW, 0), tq)
    k = k_ref[pl.ds(off, span), :]
    vo = vo_scr[pl.ds(off, span), :]
    qpos = i * tq + lax.broadcasted_iota(jnp.int32, (tq, span), 0)
    kpos = off + lax.broadcasted_iota(jnp.int32, (tq, span), 1)
    bias1 = jnp.where((kpos <= qpos) & (qpos - kpos < WINDOW), 0.0, NEG).astype(BF16)
    s = _dot_nt(qs, k).astype(BF16) + jnp.concatenate([bias1] * hpg, axis=0)
    p = jnp.exp2(s - jnp.max(s, axis=1, keepdims=True))
    pv = _dot(p, vo)
    return pv[:, :dh] / pv[:, dh:]


def _nsa_attn_body(tq, tk, hpg, n_sel, q_ref, kc_ref, vc_ref, ks_ref, vs_ref, kw_ref, vw_ref, gates_ref, o_ref,
                   vos_scr, vow_scr, vt_scr, st_scr, m_scr, acc_scr):
    g = pl.program_id(1)
    i = pl.program_id(2)

    @pl.when(i == 0)
    def _():
        _fill_values_and_ones(vos_scr, vs_ref)
        _fill_values_and_ones(vow_scr, vw_ref)

    qs = _stack_heads(q_ref, hpg)
    o_cmp, sel = _cmp_branch(qs, kc_ref[0, 0, 0], vc_ref[0, 0, 0], i, tq, hpg, n_sel, vt_scr, st_scr)
    o_win = _win_branch(qs, kw_ref, vow_scr, i, tq, hpg)
    o_sel = _sel_branch(qs, ks_ref, vos_scr, sel, i, tq, tk, hpg, m_scr, acc_scr)
    gates = gates_ref[...]
    for h in range(hpg):
        rows = slice(h * tq, (h + 1) * tq)
        col = 3 * (g * hpg + h)
        o = (_gate_column(gates, col) * o_cmp[rows, :] + _gate_column(gates, col + 1) * o_sel[rows, :]
             + _gate_column(gates, col + 2) * o_win[rows, :])
        o_ref[:, h * HEAD_DIM:(h + 1) * HEAD_DIM] = o.astype(BF16)


def _nsa_attention(qkv, kvc, gates, col, bsz, seq, ng, hpg, tq, tk):
    T = bsz * seq
    nq = seq // tq
    gw = hpg * HEAD_DIM
    rows = hpg * tq
    n_chunk = kvc.shape[3]
    n_sel = seq // SEL_LEN
    assert n_sel <= LANES and n_sel % 8 == 0
    assert seq >= WINDOW + tq and WINDOW % tq == 0
    resident = lambda t: pl.BlockSpec((seq, HEAD_DIM), lambda b, g, i: (b, col(t) + g))
    compressed = lambda s: pl.BlockSpec((1, 1, 1, n_chunk, HEAD_DIM), lambda b, g, i: (s, b, g, 0, 0))
    return pl.pallas_call(
        functools.partial(_nsa_attn_body, tq, tk, hpg, n_sel),
        grid=(bsz, ng, nq),
        in_specs=[
            pl.BlockSpec((tq, gw), lambda b, g, i: (b * nq + i, g)),
            compressed(0), compressed(1),
            resident(ng + 2), resident(ng + 3), resident(ng + 4), resident(ng + 5),
            pl.BlockSpec((tq, LANES), lambda b, g, i: (b * nq + i, 0)),
        ],
        out_specs=pl.BlockSpec((tq, gw), lambda b, g, i: (b * nq + i, g)),
        out_shape=jax.ShapeDtypeStruct((T, ng * gw), BF16),
        scratch_shapes=[
            pltpu.VMEM((seq, HEAD_DIM + LANES), BF16), pltpu.VMEM((seq, HEAD_DIM + LANES), BF16),
            pltpu.VMEM((LANES, tq), F32), pltpu.VMEM((LANES, tq), F32),
            pltpu.VMEM((rows, LANES), F32), pltpu.VMEM((rows, HEAD_DIM + LANES), F32),
        ],
        compiler_params=_params("parallel", "parallel", "arbitrary"),
        name="nsa_attention",
    )(qkv, kvc, kvc, qkv, qkv, qkv, qkv, gates)


def _ident_prologue(a_ref):
    return a_ref[...]


TM_PROJ = 1024
TN_PROJ = 512
TN_WIDE = 1024
TM_OUT = 256
TM_MLP = 1024
TF_MLP = 512
TM_POOL = 256
TQ_SB = 128
SB_CHAINS = 8
TQ_NSA = 512
TK_SEL = 512


def _tile_gains(per_tile, tn):
    rows = [jnp.tile(g.astype(F32), tn // HEAD_DIM) if g is not None else jnp.ones((tn,), F32) for g in per_tile]
    return jnp.stack(rows).reshape(len(per_tile), 1, tn)


def _conv_layer(x, seq, norm_g, w_in, conv_w, w_out):
    n_tiles = w_in.shape[1] // TN_WIDE
    bcv = _norm_proj(x, norm_g, w_in, _tile_gains([None] * n_tiles, TN_WIDE),
                     ["plain"] * n_tiles, TM_PROJ, TN_WIDE)
    return _conv_out(bcv, conv_w, w_out, x, seq, TM_OUT)


def _nsa_layer(x, bsz, seq, norm_g, w_in, q_gain, k_gain, cmp_pos, cmp_w1, cmp_w2, w_out):
    T, D = x.shape
    dh, G = HEAD_DIM, NSA_KV_GROUPS
    H = D // dh
    hpg = H // G
    gw = hpg * dh
    assert gw == TN_PROJ and G * dh == TN_PROJ
    scale = LOG2E * dh ** -0.5
    width = w_in.shape[1]
    n_tiles = -(-width // TN_PROJ)
    w_pad = jnp.pad(w_in, ((0, 0), (0, n_tiles * TN_PROJ - width))).astype(BF16)
    modes = ["norm"] * G + ["plain", "plain", "norm", "plain", "norm", "plain", "gate"]
    gains = _tile_gains([q_gain * scale] * G + [None, None, k_gain[1], None, k_gain[2], None, None], TN_PROJ)
    qkv, gates = _norm_proj(x, norm_g, w_pad, gains, modes, TM_PROJ, TN_PROJ)

    kvc_in = qkv[:, H * dh:H * dh + 2 * G * dh].astype(F32)
    pos = jnp.broadcast_to(cmp_pos.reshape(2, 1, CMP_LEN * dh), (2, 8, CMP_LEN * dh)).astype(BF16)
    kvc = _compress(kvc_in, cmp_w1.astype(BF16), cmp_w2.astype(BF16), pos, k_gain[0].reshape(1, dh), bsz, seq, G)

    col = lambda t: t * (TN_PROJ // dh)
    o = _nsa_attention(qkv, kvc, gates, col, bsz, seq, G, hpg, TQ_NSA, TK_SEL)
    spec = pl.BlockSpec((TM_OUT, H * dh), lambda i: (i, 0))
    return _res_proj([o], [spec], _ident_prologue, w_out, x, TM_OUT, "nsa_out")


def _sb_layer(x, bsz, seq, norm_g, w_in, q_gain, k_gain, w_out):
    T, D = x.shape
    dh = HEAD_DIM
    H = D // dh
    scale = LOG2E * dh ** -0.5
    per = (H * dh) // TN_WIDE
    modes = ["norm"] * (2 * per) + ["plain"] * per
    gains = _tile_gains([q_gain * scale] * per + [k_gain] * per + [None] * per, TN_WIDE)
    qkv = _norm_proj(x, norm_g, w_in, gains, modes, TM_PROJ, TN_WIDE)
    o = _sb_attention(qkv, bsz, seq, H, TQ_SB, SB_CHAINS)
    spec = pl.BlockSpec((TM_OUT, H * dh), lambda i: (i, 0))
    return _res_proj([o], [spec], _ident_prologue, w_out, x, TM_OUT, "sb_out")


def kernel(x, mix_norm, mlp_norm, mlp_w1, mlp_w2, conv_w_in, conv_w, conv_w_out, nsa_w_in, nsa_q_gain, nsa_k_gain, nsa_cmp_pos, nsa_cmp_w1, nsa_cmp_w2, nsa_w_out, pool_w, pool_scale, sb_w_in, sb_q_gain, sb_k_gain, sb_w_out):
    bsz, seq, d = x.shape
    depth = mix_norm.shape[0]
    n_mixers = 4
    xf = x.reshape(bsz * seq, d)
    for i in range(depth):
        kind, j = i % n_mixers, i // n_mixers
        if kind == 0:
            xf = _conv_layer(xf, seq, mix_norm[i], conv_w_in[j], conv_w[j], conv_w_out[j])
        elif kind == 1:
            xf = _nsa_layer(xf, bsz, seq, mix_norm[i], nsa_w_in[j], nsa_q_gain[j], nsa_k_gain[j],
                            nsa_cmp_pos[j], nsa_cmp_w1[j], nsa_cmp_w2[j], nsa_w_out[j])
        elif kind == 2:
            xf = _pool_mixer(xf, mix_norm[i], pool_w[j].astype(BF16), pool_scale[j], seq, TM_POOL)
        else:
            xf = _sb_layer(xf, bsz, seq, mix_norm[i], sb_w_in[j], sb_q_gain[j], sb_k_gain[j], sb_w_out[j])
        xf = _mlp(xf, mlp_norm[i], mlp_w1, mlp_w2, i, TM_MLP, TF_MLP)
    return xf.reshape(bsz, seq, d)
```

```python
import functools

import jax
import jax.numpy as jnp
from jax import lax
from jax.experimental import pallas as pl
from jax.experimental.pallas import tpu as pltpu

F32 = jnp.float32
BF16 = jnp.bfloat16

HEAD_DIM = 128
EPS = 1e-6
NEG = -1e30
BIG = 1e4
CONV_WIDTH = 3
NSA_KV_GROUPS = 4
CMP_LEN = 32
CMP_STRIDE = 16
SEL_LEN = 64
SEL_SHIFT = SEL_LEN.bit_length() - 1
assert 1 << SEL_SHIFT == SEL_LEN
SEL_TOPK = 16
WINDOW = 512
POOL_WINDOWS = (2, 4, 8, 16)
LANES = 128
VMEM_LIMIT = 56 * 1024 * 1024


def _params(*sem):
    return pltpu.CompilerParams(dimension_semantics=sem, vmem_limit_bytes=VMEM_LIMIT)


def _rms(xv, g):
    ms = jnp.mean(xv * xv, axis=-1, keepdims=True)
    return xv * lax.rsqrt(ms + EPS) * g


def _dot(a, b):
    return jnp.dot(a, b, preferred_element_type=F32)


def _dot_nt(a, b):
    return lax.dot_general(a, b, (((1,), (1,)), ((), ())), preferred_element_type=F32)


def _split_dot(a, b):
    hi = a.astype(BF16)
    r1 = a - hi.astype(F32)
    mid = r1.astype(BF16)
    lo = (r1 - mid.astype(F32)).astype(BF16)
    return _dot(hi, b) + _dot(mid, b) + _dot(lo, b)


def _norm_proj_body(modes, tn, has_gate, x_ref, g_ref, w_ref, gain_ref, *rest):
    if has_gate:
        o_ref, og_ref, h_scr = rest
    else:
        o_ref, h_scr = rest
    j = pl.program_id(1)

    @pl.when(j == 0)
    def _():
        h_scr[...] = _rms(x_ref[...], g_ref[...]).astype(BF16)

    acc = _dot(h_scr[...], w_ref[...].astype(BF16))

    def tiles_of(mode):
        return [t for t, m in enumerate(modes) if m == mode]

    def any_of(tiles):
        c = j == tiles[0]
        for t in tiles[1:]:
            c = c | (j == t)
        return c

    if tiles_of("plain"):
        @pl.when(any_of(tiles_of("plain")))
        def _():
            o_ref[...] = acc.astype(BF16)

    if tiles_of("norm"):
        @pl.when(any_of(tiles_of("norm")))
        def _():
            gain = gain_ref[0]
            for h in range(tn // HEAD_DIM):
                sl = slice(h * HEAD_DIM, (h + 1) * HEAD_DIM)
                o_ref[:, sl] = _rms(acc[:, sl], gain[:, sl]).astype(BF16)

    if tiles_of("gate"):
        @pl.when(any_of(tiles_of("gate")))
        def _():
            o_ref[...] = acc.astype(BF16)
            og_ref[...] = jax.nn.sigmoid(acc[:, :LANES])


def _norm_proj(x, g, w, gains, modes, tm, tn):
    T, D = x.shape
    N = w.shape[1]
    assert N == len(modes) * tn and T % tm == 0
    has_gate = "gate" in modes
    out_shape = [jax.ShapeDtypeStruct((T, N), BF16)]
    out_specs = [pl.BlockSpec((tm, tn), lambda i, j: (i, j))]
    if has_gate:
        out_shape.append(jax.ShapeDtypeStruct((T, LANES), F32))
        out_specs.append(pl.BlockSpec((tm, LANES), lambda i, j: (i, 0)))
    res = pl.pallas_call(
        functools.partial(_norm_proj_body, tuple(modes), tn, has_gate),
        grid=(T // tm, N // tn),
        in_specs=[
            pl.BlockSpec((tm, D), lambda i, j: (i, 0)),
            pl.BlockSpec((1, D), lambda i, j: (0, 0)),
            pl.BlockSpec((D, tn), lambda i, j: (0, j)),
            pl.BlockSpec((1, 1, tn), lambda i, j: (j, 0, 0)),
        ],
        out_specs=out_specs,
        out_shape=out_shape,
        scratch_shapes=[pltpu.VMEM((tm, D), BF16)],
        compiler_params=_params("parallel", "arbitrary"),
        name="norm_proj",
    )(x, g.reshape(1, D), w, gains)
    return res if has_gate else res[0]


def _res_proj_body(prologue, n_rows, *refs):
    row_refs = refs[:n_rows]
    w_ref, x_ref, o_ref, wb_scr = refs[n_rows:n_rows + 4]

    @pl.when(pl.program_id(0) == 0)
    def _():
        wb_scr[...] = w_ref[...].astype(BF16)

    a = prologue(*row_refs)
    o_ref[...] = x_ref[...] + _dot(a, wb_scr[...])


def _res_proj(rows, row_specs, prologue, w, x, tm, name):
    T, D = x.shape
    K = w.shape[0]
    return pl.pallas_call(
        functools.partial(_res_proj_body, prologue, len(rows)),
        grid=(T // tm,),
        in_specs=list(row_specs) + [
            pl.BlockSpec((K, D), lambda i: (0, 0), pipeline_mode=pl.Buffered(1)),
            pl.BlockSpec((tm, D), lambda i: (i, 0)),
        ],
        out_specs=pl.BlockSpec((tm, D), lambda i: (i, 0)),
        out_shape=jax.ShapeDtypeStruct((T, D), F32),
        scratch_shapes=[pltpu.VMEM((K, D), BF16)],
        compiler_params=_params("arbitrary"),
        name=name,
    )(*rows, w, x)


def _mlp_body(x_ref, g_ref, w1_ref, w2_ref, o_ref, h_scr):
    f = pl.program_id(1)

    @pl.when(f == 0)
    def _():
        xv = x_ref[...]
        h_scr[...] = _rms(xv, g_ref[...]).astype(BF16)
        o_ref[...] = xv

    a = jnp.maximum(_dot(h_scr[...], w1_ref[...].astype(BF16)), 0.0)
    o_ref[...] += _dot((a * a).astype(BF16), w2_ref[...].astype(BF16))


def _mlp(x, g, w1, w2, layer, tm, tf):
    T, D = x.shape
    FF = w1.shape[2]
    return pl.pallas_call(
        _mlp_body,
        grid=(T // tm, FF // tf),
        in_specs=[
            pl.BlockSpec((tm, D), lambda i, f: (i, 0)),
            pl.BlockSpec((1, D), lambda i, f: (0, 0)),
            pl.BlockSpec((None, D, tf), lambda i, f: (layer, 0, f)),
            pl.BlockSpec((None, tf, D), lambda i, f: (layer, f, 0)),
        ],
        out_specs=pl.BlockSpec((tm, D), lambda i, f: (i, 0)),
        out_shape=jax.ShapeDtypeStruct((T, D), F32),
        scratch_shapes=[pltpu.VMEM((tm, D), BF16)],
        compiler_params=_params("parallel", "arbitrary"),
        name="mlp",
    )(x, g.reshape(1, D), w1, w2)


HALO = 16


def _conv_prologue(tm, tiles_per_seq, b_ref, c_ref, v_ref, cp_ref, vp_ref, cw_ref):
    i = pl.program_id(0)
    first = (i % tiles_per_seq) == 0
    u = c_ref[...].astype(F32) * v_ref[...].astype(F32)
    up = cp_ref[...].astype(F32) * vp_ref[...].astype(F32)
    up = jnp.where(first, 0.0, up)
    row = lax.broadcasted_iota(jnp.int32, u.shape, 0)
    r1 = jnp.where(row == 0, up[HALO - 1:HALO, :], pltpu.roll(u, 1, 0))
    r2 = pltpu.roll(u, 2, 0)
    r2 = jnp.where(row == 0, up[HALO - 2:HALO - 1, :], jnp.where(row == 1, up[HALO - 1:HALO, :], r2))
    cw = cw_ref[...]
    y = cw[0:1, :] * r2 + cw[1:2, :] * r1 + cw[2:3, :] * u
    return (b_ref[...].astype(F32) * y).astype(BF16)


def _conv_out(bcv, conv_w, w_out, x, seq, tm):
    T, D = x.shape
    hb = tm // HALO
    prev = lambda col: (lambda i: (jnp.maximum(i * hb - 1, 0), col))
    specs = [
        pl.BlockSpec((tm, D), lambda i: (i, 0)),
        pl.BlockSpec((tm, D), lambda i: (i, 1)),
        pl.BlockSpec((tm, D), lambda i: (i, 2)),
        pl.BlockSpec((HALO, D), prev(1)),
        pl.BlockSpec((HALO, D), prev(2)),
        pl.BlockSpec((CONV_WIDTH, D), lambda i: (0, 0)),
    ]
    prologue = functools.partial(_conv_prologue, tm, seq // tm)
    return _res_proj([bcv, bcv, bcv, bcv, bcv, conv_w], specs, prologue, w_out, x, tm, "conv_out")


def _pool_body(tm, tiles_per_seq, x_ref, xp_ref, g_ref, w_ref, sc_ref, o_ref):
    i = pl.program_id(0)
    first = (i % tiles_per_seq) == 0
    xv = x_ref[...]
    g = g_ref[...]
    h = _rms(xv, g)
    hp = jnp.where(first, 0.0, _rms(xp_ref[...], g))
    pos = (i % tiles_per_seq) * tm + lax.broadcasted_iota(jnp.int32, (tm, 1), 0)
    cg = h.shape[1] // len(POOL_WINDOWS)
    for gi, win in enumerate(POOL_WINDOWS):
        sl = slice(gi * cg, (gi + 1) * cg)
        hg = h[:, sl]
        s = jnp.concatenate([hp[:, sl], hg], axis=0)
        k = 1
        while k < win:
            s = s + pltpu.roll(s, k, 0)
            k *= 2
        cnt = jnp.minimum(pos + 1, win).astype(F32)
        pooled = s[HALO:, :] / cnt - hg
        y = _dot(pooled.astype(BF16), w_ref[gi])
        o_ref[:, sl] = xv[:, sl] + y * sc_ref[:, sl]


def _pool_mixer(x, g, w, scale, seq, tm):
    T, D = x.shape
    ng, cg, _ = w.shape
    hb = tm // HALO
    return pl.pallas_call(
        functools.partial(_pool_body, tm, seq // tm),
        grid=(T // tm,),
        in_specs=[
            pl.BlockSpec((tm, D), lambda i: (i, 0)),
            pl.BlockSpec((HALO, D), lambda i: (jnp.maximum(i * hb - 1, 0), 0)),
            pl.BlockSpec((1, D), lambda i: (0, 0)),
            pl.BlockSpec((ng, cg, cg), lambda i: (0, 0, 0)),
            pl.BlockSpec((1, D), lambda i: (0, 0)),
        ],
        out_specs=pl.BlockSpec((tm, D), lambda i: (i, 0)),
        out_shape=jax.ShapeDtypeStruct((T, D), F32),
        compiler_params=_params("parallel"),
        name="pool_mixer",
    )(x, x, g.reshape(1, D), w, scale.reshape(1, D))


LOG2E = 1.4426950408889634
SB_STOP_LOG2 = -135.0


def _sb_body(tq, n_chain, q_ref, k_ref, v_ref, o_ref):
    nq = q_ref.shape[0] // tq
    row = lax.broadcasted_iota(jnp.int32, (tq, tq), 0)
    col = lax.broadcasted_iota(jnp.int32, (tq, tq), 1)
    before_diag = col < row
    r2 = lax.broadcasted_iota(jnp.int32, (2 * tq, 2 * tq), 0)
    c2 = lax.broadcasted_iota(jnp.int32, (2 * tq, 2 * tq), 1)
    key = jnp.where(r2 >= tq, r2 - tq, r2)
    sums = ((c2 >= tq) | (key > c2)).astype(BF16)

    def tiles(qs, js, cs, accs, diag, lives=None):
        chains = range(n_chain)
        offs = [pl.multiple_of(js[r] * tq, tq) for r in chains]
        zs = [_dot_nt(qs[r], k_ref[pl.ds(offs[r], tq), :]) for r in chains]
        log_1m = [jnp.log(1.0 + jnp.exp2(-jnp.abs(z))) * (-LOG2E) - jnp.maximum(z, 0.0) for z in zs]
        if diag:
            log_1m = [jnp.where(before_diag, x, 0.0) for x in log_1m]
        his = [x.astype(BF16) for x in log_1m]
        mids = [(x - hi.astype(F32)).astype(BF16) for x, hi in zip(log_1m, his)]
        ts = [_dot(jnp.concatenate([hi, mid], axis=1), sums) for hi, mid in zip(his, mids)]
        ws = [jnp.exp2(zs[r] + log_1m[r] + ts[r][:, :tq] + cs[r]) for r in chains]
        tile_sums = [t[:, tq:] for t in ts]
        if diag:
            ws = [jnp.where(before_diag, w, 0.0) for w in ws]
        if lives is not None:
            ws = [jnp.where(lives[r], ws[r], 0.0) for r in chains]
            tile_sums = [jnp.where(lives[r], tile_sums[r], 0.0) for r in chains]
        accs = tuple(accs[r] + _dot(ws[r].astype(BF16), v_ref[pl.ds(offs[r], tq), :]) for r in chains)
        return tuple(cs[r] + tile_sums[r] for r in chains), accs

    def q_group(gi, carry):
        base = gi * n_chain
        chains = range(n_chain)
        qs = [q_ref[pl.ds(pl.multiple_of((base + r) * tq, tq), tq), :] for r in chains]
        zero = (jnp.zeros((tq, tq), F32),) * n_chain
        cs, accs = tiles(qs, [base + r for r in chains], zero, zero, True)

        def live_max(cs, n):
            vals = [jnp.where(base + r - n >= 0, jnp.max(cs[r]), -jnp.inf) for r in chains]
            return functools.reduce(jnp.maximum, vals)

        def cond(st):
            n, _, _, cmax = st
            return (n <= base + n_chain - 1) & (cmax > SB_STOP_LOG2)

        def body(st):
            n, cs, accs, _ = st
            js = [base + r - n for r in chains]
            cs, accs = tiles(qs, [jnp.maximum(j, 0) for j in js], cs, accs, False, [j >= 0 for j in js])
            return n + 1, cs, accs, live_max(cs, n + 1)

        _, _, accs, _ = lax.while_loop(cond, body, (1, cs, accs, live_max(cs, 1)))
        for r in chains:
            o_ref[pl.ds(pl.multiple_of((base + r) * tq, tq), tq), :] = accs[r].astype(BF16)
        return carry

    lax.fori_loop(0, nq // n_chain, q_group, 0)


def _sb_attention(qkv, bsz, seq, n_heads, tq, n_chain):
    assert tq == HEAD_DIM == LANES
    assert seq % (tq * n_chain) == 0
    T = bsz * seq
    return pl.pallas_call(
        functools.partial(_sb_body, tq, n_chain),
        grid=(bsz, n_heads),
        in_specs=[
            pl.BlockSpec((seq, HEAD_DIM), lambda b, h: (b, h)),
            pl.BlockSpec((seq, HEAD_DIM), lambda b, h: (b, n_heads + h)),
            pl.BlockSpec((seq, HEAD_DIM), lambda b, h: (b, 2 * n_heads + h)),
        ],
        out_specs=pl.BlockSpec((seq, HEAD_DIM), lambda b, h: (b, h)),
        out_shape=jax.ShapeDtypeStruct((T, n_heads * HEAD_DIM), BF16),
        compiler_params=_params("parallel", "parallel"),
        name="sb_attention",
    )(qkv, qkv, qkv)


def _compress_body(n_chunk, t_ref, w1_ref, w2_ref, pos_ref, gain_ref, o_ref):
    kv = pl.program_id(0)
    a = jnp.concatenate([t_ref[pl.ds(p, n_chunk, stride=CMP_STRIDE), :].astype(BF16) for p in range(CMP_STRIDE)],
                        axis=1)
    half = a.shape[1]
    w1 = w1_ref[0]
    const = _dot(pos_ref[0], w1)[0:1, :]
    first = _dot(a, w1[:half, :])
    second = _dot(a, w1[half:, :])
    pre = first + pltpu.roll(second, n_chunk - 1, 0) + const
    hid = pre * jax.nn.sigmoid(pre)
    out = _dot(hid.astype(BF16), w2_ref[0])
    normed = _rms(out, gain_ref[...])
    o_ref[0, 0, 0] = jnp.where(kv == 0, normed, out).astype(BF16)


def _compress(t, w1, w2, pos, gain, bsz, seq, ng):
    dh = HEAD_DIM
    n_chunk = seq // CMP_STRIDE
    width = CMP_STRIDE * dh
    return pl.pallas_call(
        functools.partial(_compress_body, n_chunk),
        grid=(2, bsz, ng),
        in_specs=[
            pl.BlockSpec((seq, dh), lambda s, b, g: (b, s * ng + g)),
            pl.BlockSpec((1, 2 * width, dh), lambda s, b, g: (s, 0, 0)),
            pl.BlockSpec((1, dh, dh), lambda s, b, g: (s, 0, 0)),
            pl.BlockSpec((1, 8, 2 * width), lambda s, b, g: (s, 0, 0)),
            pl.BlockSpec((1, dh), lambda s, b, g: (0, 0)),
        ],
        out_specs=pl.BlockSpec((1, 1, 1, n_chunk, dh), lambda s, b, g: (s, b, g, 0, 0)),
        out_shape=jax.ShapeDtypeStruct((2, bsz, ng, n_chunk, dh), BF16),
        compiler_params=_params("parallel", "parallel", "parallel"),
        name="nsa_compress",
    )(t, w1, w2, pos, gain)


def _stack_heads(q_ref, hpg):
    return jnp.concatenate([q_ref[:, h * HEAD_DIM:(h + 1) * HEAD_DIM] for h in range(hpg)], axis=0)


def _gate_column(gates, col):
    lane = lax.broadcasted_iota(jnp.int32, gates.shape, 1)
    return jnp.sum(jnp.where(lane == col, gates, 0.0), axis=1, keepdims=True)


def _cmp_branch(qs, kc, vc, i, tq, hpg, n_sel, vt_scr, st_scr):
    n_chunk = kc.shape[0]
    s = _dot_nt(qs, kc)
    t1 = i * tq + lax.broadcasted_iota(jnp.int32, (tq, n_chunk), 0)
    c1 = lax.broadcasted_iota(jnp.int32, (tq, n_chunk), 1)
    valid1 = c1 * CMP_STRIDE + (CMP_LEN - 1) <= t1
    valid = jnp.concatenate([valid1] * hpg, axis=0)
    s = jnp.where(valid, s, NEG)
    m = jnp.max(s, axis=1, keepdims=True)
    e = jnp.where(valid, jnp.exp2(s - m), 0.0)
    l = jnp.sum(e, axis=1, keepdims=True)
    p = e * (1.0 / jnp.where(l > 0.0, l, 1.0))
    o = _dot(p.astype(BF16), vc)

    p_sum = p[0:tq, :]
    for h in range(1, hpg):
        p_sum = p_sum + p[h * tq:(h + 1) * tq, :]
    ci = lax.broadcasted_iota(jnp.int32, (n_chunk, LANES), 0)
    sj = lax.broadcasted_iota(jnp.int32, (n_chunk, LANES), 1)
    overlap = ((ci * CMP_STRIDE < (sj + 1) * SEL_LEN) & (ci * CMP_STRIDE + CMP_LEN > sj * SEL_LEN)
               & (ci < n_chunk - 1) & (sj < n_sel)).astype(BF16)
    imp = _split_dot(p_sum, overlap)
    t2 = i * tq + lax.broadcasted_iota(jnp.int32, (tq, LANES), 0)
    blk = lax.broadcasted_iota(jnp.int32, (tq, LANES), 1)
    cur = lax.shift_right_logical(t2, SEL_SHIFT)
    forced = (blk == 0) | (blk == cur) | (blk == cur - 1)
    blk_valid = blk * SEL_LEN <= t2
    score = jnp.where(forced, BIG, jnp.where(blk_valid, imp, -BIG))

    vt_scr[...] = score.T
    SUB = 8
    vts = [vt_scr[a * SUB:(a + 1) * SUB, :] for a in range(n_sel // SUB)]
    cnts = [jnp.zeros((SUB, tq), F32) for _ in vts]
    jrow = lax.broadcasted_iota(jnp.int32, (SUB, tq), 0)
    for b in range(n_sel):
        vb = vt_scr[b:b + 1, :]
        for a, vt in enumerate(vts):
            if b < a * SUB:
                ahead = vb >= vt
            elif b >= (a + 1) * SUB:
                ahead = vb > vt
            else:
                ahead = (vb > vt) | ((vb == vt) & (jrow > b - a * SUB))
            cnts[a] = cnts[a] + ahead.astype(F32)
    st_scr[...] = jnp.zeros_like(st_scr)
    for a, cnt in enumerate(cnts):
        st_scr[a * SUB:(a + 1) * SUB, :] = (cnt < float(min(SEL_TOPK, n_sel))).astype(F32)
    return o, st_scr[...].T.astype(BF16)


def _fill_values_and_ones(vo_scr, v_ref):
    vo_scr[:, :HEAD_DIM] = v_ref[...]
    vo_scr[:, HEAD_DIM:] = jnp.ones((vo_scr.shape[0], LANES), BF16)


def _sel_branch(qs, k_ref, vo_scr, sel, i, tq, tk, hpg, m_scr, acc_scr):
    dh = HEAD_DIM
    m_scr[...] = jnp.full_like(m_scr, NEG)
    acc_scr[...] = jnp.zeros_like(acc_scr)
    qpos = i * tq + lax.broadcasted_iota(jnp.int32, (tq, tk), 0)
    kcol = lax.broadcasted_iota(jnp.int32, (tq, tk), 1)
    eb = lax.broadcasted_iota(jnp.int32, (LANES, tk), 0)
    ek = lax.broadcasted_iota(jnp.int32, (LANES, tk), 1)
    n_tiles = ((i + 1) * tq + tk - 1) // tk

    def scores(j):
        k = k_ref[pl.ds(pl.multiple_of(j * tk, tk), tk), :]
        expand = (eb == lax.shift_right_logical(j * tk + ek, SEL_SHIFT)).astype(BF16)
        mask1 = (_dot(sel, expand) > 0.5) & (j * tk + kcol <= qpos)
        bias1 = jnp.where(mask1, 0.0, NEG).astype(BF16)
        return _dot_nt(qs, k).astype(BF16) + jnp.concatenate([bias1] * hpg, axis=0)

    def step(n, s):
        j = n_tiles - 1 - n
        s_next = scores(jnp.maximum(j - 1, 0))
        vo = vo_scr[pl.ds(pl.multiple_of(j * tk, tk), tk), :]
        chunks = [s[:, c * LANES:(c + 1) * LANES] for c in range(tk // LANES)]
        mx = chunks[0]
        for ch in chunks[1:]:
            mx = jnp.maximum(mx, ch)
        m_old = m_scr[...]
        m_new = jnp.maximum(m_old, jnp.max(mx, axis=1, keepdims=True).astype(F32))
        alpha = jnp.exp2(m_old - m_new)
        m_b = m_new.astype(BF16)
        p = jnp.concatenate([jnp.exp2(ch - m_b) for ch in chunks], axis=1)
        pv = _dot(p, vo)
        acc_scr[:, :dh] = alpha * acc_scr[:, :dh] + pv[:, :dh]
        acc_scr[:, dh:] = alpha * acc_scr[:, dh:] + pv[:, dh:]
        m_scr[...] = m_new
        return s_next

    lax.fori_loop(0, n_tiles, step, scores(n_tiles - 1))
    return acc_scr[:, :dh] / acc_scr[:, dh:]


def _win_branch(qs, k_ref, vo_scr, i, tq, tw, hpg):
    dh = HEAD_DIM
    span = WINDOW + tw
    outs = []
    for r in range(tq // tw):
        q0 = i * tq + r * tw
        qr = jnp.concatenate([qs[h * tq + r * tw:h * tq + (r + 1) * tw, :] for h in range(hpg)], axis=0)
        off = pl.multiple_of(jnp.maximum(q0 - WINDOW, 0), tw)
        k = k_ref[pl.ds(off, span), :]
        vo = vo_scr[pl.ds(off, span), :]
        qpos = q0 + lax.broadcasted_iota(jnp.int32, (tw, span), 0)
        kpos = off + lax.broadcasted_iota(jnp.int32, (tw, span), 1)
        bias1 = jnp.where((kpos <= qpos) & (qpos - kpos < WINDOW), 0.0, NEG).astype(BF16)
        s = _dot_nt(qr, k).astype(BF16) + jnp.concatenate([bias1] * hpg, axis=0)
        p = jnp.exp2(s - jnp.max(s, axis=1, keepdims=True))
        pv = _dot(p, vo)
        outs.append(pv[:, :dh] / pv[:, dh:])
    return jnp.concatenate([o[h * tw:(h + 1) * tw, :] for h in range(hpg) for o in outs], axis=0)


def _nsa_attn_body(tq, tk, tw, hpg, n_sel, q_ref, kc_ref, vc_ref, ks_ref, vs_ref, kw_ref, vw_ref, gates_ref, o_ref,
                   vos_scr, vow_scr, vt_scr, st_scr, m_scr, acc_scr):
    g = pl.program_id(1)
    i = pl.program_id(2)

    @pl.when(i == 0)
    def _():
        _fill_values_and_ones(vos_scr, vs_ref)
        _fill_values_and_ones(vow_scr, vw_ref)

    qs = _stack_heads(q_ref, hpg)
    o_cmp, sel = _cmp_branch(qs, kc_ref[0, 0, 0], vc_ref[0, 0, 0], i, tq, hpg, n_sel, vt_scr, st_scr)
    o_win = _win_branch(qs, kw_ref, vow_scr, i, tq, tw, hpg)
    o_sel = _sel_branch(qs, ks_ref, vos_scr, sel, i, tq, tk, hpg, m_scr, acc_scr)
    gates = gates_ref[...]
    for h in range(hpg):
        rows = slice(h * tq, (h + 1) * tq)
        col = 3 * (g * hpg + h)
        o = (_gate_column(gates, col) * o_cmp[rows, :] + _gate_column(gates, col + 1) * o_sel[rows, :]
             + _gate_column(gates, col + 2) * o_win[rows, :])
        o_ref[:, h * HEAD_DIM:(h + 1) * HEAD_DIM] = o.astype(BF16)


def _nsa_attention(qkv, kvc, gates, col, bsz, seq, ng, hpg, tq, tk, tw):
    T = bsz * seq
    nq = seq // tq
    gw = hpg * HEAD_DIM
    rows = hpg * tq
    n_chunk = kvc.shape[3]
    n_sel = seq // SEL_LEN
    assert n_sel <= LANES and n_sel % 8 == 0
    assert seq >= WINDOW + tw and WINDOW % tw == 0 and tq % tw == 0
    resident = lambda t: pl.BlockSpec((seq, HEAD_DIM), lambda b, g, i: (b, col(t) + g))
    compressed = lambda s: pl.BlockSpec((1, 1, 1, n_chunk, HEAD_DIM), lambda b, g, i: (s, b, g, 0, 0))
    return pl.pallas_call(
        functools.partial(_nsa_attn_body, tq, tk, tw, hpg, n_sel),
        grid=(bsz, ng, nq),
        in_specs=[
            pl.BlockSpec((tq, gw), lambda b, g, i: (b * nq + i, g)),
            compressed(0), compressed(1),
            resident(ng + 2), resident(ng + 3), resident(ng + 4), resident(ng + 5),
            pl.BlockSpec((tq, LANES), lambda b, g, i: (b * nq + i, 0)),
        ],
        out_specs=pl.BlockSpec((tq, gw), lambda b, g, i: (b * nq + i, g)),
        out_shape=jax.ShapeDtypeStruct((T, ng * gw), BF16),
        scratch_shapes=[
            pltpu.VMEM((seq, HEAD_DIM + LANES), BF16), pltpu.VMEM((seq, HEAD_DIM + LANES), BF16),
            pltpu.VMEM((LANES, tq), F32), pltpu.VMEM((LANES, tq), F32),
            pltpu.VMEM((rows, LANES), F32), pltpu.VMEM((rows, HEAD_DIM + LANES), F32),
        ],
        compiler_params=_params("parallel", "parallel", "arbitrary"),
        name="nsa_attention",
    )(qkv, kvc, kvc, qkv, qkv, qkv, qkv, gates)


def _ident_prologue(a_ref):
    return a_ref[...]


TM_PROJ = 1024
TN_PROJ = 512
TN_WIDE = 1024
TM_OUT = 256
TM_MLP = 1024
TF_MLP = 512
TM_POOL = 256
TQ_SB = 128
SB_CHAINS = 8
TQ_NSA = 512
TK_SEL = 512
TW_NSA = 256


def _tile_gains(per_tile, tn):
    rows = [jnp.tile(g.astype(F32), tn // HEAD_DIM) if g is not None else jnp.ones((tn,), F32) for g in per_tile]
    return jnp.stack(rows).reshape(len(per_tile), 1, tn)


def _conv_layer(x, seq, norm_g, w_in, conv_w, w_out):
    n_tiles = w_in.shape[1] // TN_WIDE
    bcv = _norm_proj(x, norm_g, w_in, _tile_gains([None] * n_tiles, TN_WIDE),
                     ["plain"] * n_tiles, TM_PROJ, TN_WIDE)
    return _conv_out(bcv, conv_w, w_out, x, seq, TM_OUT)


def _nsa_layer(x, bsz, seq, norm_g, w_in, q_gain, k_gain, cmp_pos, cmp_w1, cmp_w2, w_out):
    T, D = x.shape
    dh, G = HEAD_DIM, NSA_KV_GROUPS
    H = D // dh
    hpg = H // G
    gw = hpg * dh
    assert gw == TN_PROJ and G * dh == TN_PROJ
    scale = LOG2E * dh ** -0.5
    width = w_in.shape[1]
    n_tiles = -(-width // TN_PROJ)
    w_pad = jnp.pad(w_in, ((0, 0), (0, n_tiles * TN_PROJ - width))).astype(BF16)
    modes = ["norm"] * G + ["plain", "plain", "norm", "plain", "norm", "plain", "gate"]
    gains = _tile_gains([q_gain * scale] * G + [None, None, k_gain[1], None, k_gain[2], None, None], TN_PROJ)
    qkv, gates = _norm_proj(x, norm_g, w_pad, gains, modes, TM_PROJ, TN_PROJ)

    kvc_in = qkv[:, H * dh:H * dh + 2 * G * dh].astype(F32)
    pos = jnp.broadcast_to(cmp_pos.reshape(2, 1, CMP_LEN * dh), (2, 8, CMP_LEN * dh)).astype(BF16)
    kvc = _compress(kvc_in, cmp_w1.astype(BF16), cmp_w2.astype(BF16), pos, k_gain[0].reshape(1, dh), bsz, seq, G)

    col = lambda t: t * (TN_PROJ // dh)
    o = _nsa_attention(qkv, kvc, gates, col, bsz, seq, G, hpg, TQ_NSA, TK_SEL, TW_NSA)
    spec = pl.BlockSpec((TM_OUT, H * dh), lambda i: (i, 0))
    return _res_proj([o], [spec], _ident_prologue, w_out, x, TM_OUT, "nsa_out")


def _sb_layer(x, bsz, seq, norm_g, w_in, q_gain, k_gain, w_out):
    T, D = x.shape
    dh = HEAD_DIM
    H = D // dh
    scale = LOG2E * dh ** -0.5
    per = (H * dh) // TN_WIDE
    modes = ["norm"] * (2 * per) + ["plain"] * per
    gains = _tile_gains([q_gain * scale] * per + [k_gain] * per + [None] * per, TN_WIDE)
    qkv = _norm_proj(x, norm_g, w_in, gains, modes, TM_PROJ, TN_WIDE)
    o = _sb_attention(qkv, bsz, seq, H, TQ_SB, SB_CHAINS)
    spec = pl.BlockSpec((TM_OUT, H * dh), lambda i: (i, 0))
    return _res_proj([o], [spec], _ident_prologue, w_out, x, TM_OUT, "sb_out")


def kernel(x, mix_norm, mlp_norm, mlp_w1, mlp_w2, conv_w_in, conv_w, conv_w_out, nsa_w_in, nsa_q_gain, nsa_k_gain, nsa_cmp_pos, nsa_cmp_w1, nsa_cmp_w2, nsa_w_out, pool_w, pool_scale, sb_w_in, sb_q_gain, sb_k_gain, sb_w_out):
    bsz, seq, d = x.shape
    depth = mix_norm.shape[0]
    n_mixers = 4
    xf = x.reshape(bsz * seq, d)
    for i in range(depth):
        kind, j = i % n_mixers, i // n_mixers
        if kind == 0:
            xf = _conv_layer(xf, seq, mix_norm[i], conv_w_in[j], conv_w[j], conv_w_out[j])
        elif kind == 1:
            xf = _nsa_layer(xf, bsz, seq, mix_norm[i], nsa_w_in[j], nsa_q_gain[j], nsa_k_gain[j],
                            nsa_cmp_pos[j], nsa_cmp_w1[j], nsa_cmp_w2[j], nsa_w_out[j])
        elif kind == 2:
            xf = _pool_mixer(xf, mix_norm[i], pool_w[j].astype(BF16), pool_scale[j], seq, TM_POOL)
        else:
            xf = _sb_layer(xf, bsz, seq, mix_norm[i], sb_w_in[j], sb_q_gain[j], sb_k_gain[j], sb_w_out[j])
        xf = _mlp(xf, mlp_norm[i], mlp_w1, mlp_w2, i, TM_MLP, TF_MLP)
    return xf.reshape(bsz, seq, d)
```

```python
import functools

import jax
import jax.numpy as jnp
from jax import lax
from jax.experimental import pallas as pl
from jax.experimental.pallas import tpu as pltpu

F32 = jnp.float32
BF16 = jnp.bfloat16

HEAD_DIM = 128
EPS = 1e-6
NEG = -1e30
BIG = 1e4
CONV_WIDTH = 3
NSA_KV_GROUPS = 4
CMP_LEN = 32
CMP_STRIDE = 16
SEL_LEN = 64
SEL_SHIFT = SEL_LEN.bit_length() - 1
assert 1 << SEL_SHIFT == SEL_LEN
SEL_TOPK = 16
WINDOW = 512
POOL_WINDOWS = (2, 4, 8, 16)
LANES = 128
VMEM_LIMIT = 56 * 1024 * 1024


def _params(*sem):
    return pltpu.CompilerParams(dimension_semantics=sem, vmem_limit_bytes=VMEM_LIMIT)


def _rms(xv, g):
    ms = jnp.mean(xv * xv, axis=-1, keepdims=True)
    return xv * lax.rsqrt(ms + EPS) * g


def _dot(a, b):
    return jnp.dot(a, b, preferred_element_type=F32)


def _dot_nt(a, b):
    return lax.dot_general(a, b, (((1,), (1,)), ((), ())), preferred_element_type=F32)


def _split_dot(a, b):
    hi = a.astype(BF16)
    r1 = a - hi.astype(F32)
    mid = r1.astype(BF16)
    lo = (r1 - mid.astype(F32)).astype(BF16)
    return _dot(hi, b) + _dot(mid, b) + _dot(lo, b)


def _norm_proj_body(modes, tn, has_gate, transposed, x_ref, g_ref, w_ref, gain_ref, *rest):
    if has_gate:
        wg_ref, o_ref, og_ref, h_scr = rest
    else:
        o_ref, h_scr = rest
    j = pl.program_id(1)

    @pl.when(j == 0)
    def _():
        h_scr[...] = _rms(x_ref[...], g_ref[...]).astype(BF16)
        if has_gate:
            og_ref[...] = jax.nn.sigmoid(_dot_nt(h_scr[...], wg_ref[...].astype(BF16)))

    acc = (_dot_nt if transposed else _dot)(h_scr[...], w_ref[...].astype(BF16))

    def tiles_of(mode):
        return [t for t, m in enumerate(modes) if m == mode]

    def any_of(tiles):
        c = j == tiles[0]
        for t in tiles[1:]:
            c = c | (j == t)
        return c

    if tiles_of("plain"):
        @pl.when(any_of(tiles_of("plain")))
        def _():
            o_ref[...] = acc.astype(BF16)

    if tiles_of("norm"):
        @pl.when(any_of(tiles_of("norm")))
        def _():
            gain = gain_ref[0]
            for h in range(tn // HEAD_DIM):
                sl = slice(h * HEAD_DIM, (h + 1) * HEAD_DIM)
                o_ref[:, sl] = _rms(acc[:, sl], gain[:, sl]).astype(BF16)

def _norm_proj(x, g, w, gains, modes, tm, tn, transposed=False, w_gate_t=None):
    T, D = x.shape
    N = len(modes) * tn
    assert w.shape[0 if transposed else 1] >= N and T % tm == 0
    has_gate = w_gate_t is not None
    in_specs = [
        pl.BlockSpec((tm, D), lambda i, j: (i, 0)),
        pl.BlockSpec((1, D), lambda i, j: (0, 0)),
        pl.BlockSpec((tn, D), lambda i, j: (j, 0)) if transposed else pl.BlockSpec((D, tn), lambda i, j: (0, j)),
        pl.BlockSpec((1, 1, tn), lambda i, j: (j, 0, 0)),
    ]
    args = [x, g.reshape(1, D), w, gains]
    out_shape = [jax.ShapeDtypeStruct((T, N), BF16)]
    out_specs = [pl.BlockSpec((tm, tn), lambda i, j: (i, j))]
    if has_gate:
        in_specs.append(pl.BlockSpec((LANES, D), lambda i, j: (0, 0)))
        args.append(w_gate_t)
        out_shape.append(jax.ShapeDtypeStruct((T, LANES), F32))
        out_specs.append(pl.BlockSpec((tm, LANES), lambda i, j: (i, 0)))
    res = pl.pallas_call(
        functools.partial(_norm_proj_body, tuple(modes), tn, has_gate, transposed),
        grid=(T // tm, N // tn),
        in_specs=in_specs,
        out_specs=out_specs,
        out_shape=out_shape,
        scratch_shapes=[pltpu.VMEM((tm, D), BF16)],
        compiler_params=_params("parallel", "arbitrary"),
        name="norm_proj",
    )(*args)
    return res if has_gate else res[0]


def _res_proj_body(prologue, n_rows, *refs):
    row_refs = refs[:n_rows]
    w_ref, x_ref, o_ref, wb_scr = refs[n_rows:n_rows + 4]

    @pl.when(pl.program_id(0) == 0)
    def _():
        wb_scr[...] = w_ref[...].astype(BF16)

    a = prologue(*row_refs)
    o_ref[...] = x_ref[...] + _dot(a, wb_scr[...])


def _res_proj(rows, row_specs, prologue, w, x, tm, name):
    T, D = x.shape
    K = w.shape[0]
    return pl.pallas_call(
        functools.partial(_res_proj_body, prologue, len(rows)),
        grid=(T // tm,),
        in_specs=list(row_specs) + [
            pl.BlockSpec((K, D), lambda i: (0, 0), pipeline_mode=pl.Buffered(1)),
            pl.BlockSpec((tm, D), lambda i: (i, 0)),
        ],
        out_specs=pl.BlockSpec((tm, D), lambda i: (i, 0)),
        out_shape=jax.ShapeDtypeStruct((T, D), F32),
        scratch_shapes=[pltpu.VMEM((K, D), BF16)],
        compiler_params=_params("arbitrary"),
        name=name,
    )(*rows, w, x)


def _mlp_body(x_ref, g_ref, w1_ref, w2_ref, o_ref, h_scr):
    f = pl.program_id(1)

    @pl.when(f == 0)
    def _():
        xv = x_ref[...]
        h_scr[...] = _rms(xv, g_ref[...]).astype(BF16)
        o_ref[...] = xv

    a = jnp.maximum(_dot(h_scr[...], w1_ref[...].astype(BF16)), 0.0)
    o_ref[...] += _dot((a * a).astype(BF16), w2_ref[...].astype(BF16))


def _mlp(x, g, w1, w2, layer, tm, tf):
    T, D = x.shape
    FF = w1.shape[2]
    return pl.pallas_call(
        _mlp_body,
        grid=(T // tm, FF // tf),
        in_specs=[
            pl.BlockSpec((tm, D), lambda i, f: (i, 0)),
            pl.BlockSpec((1, D), lambda i, f: (0, 0)),
            pl.BlockSpec((None, D, tf), lambda i, f: (layer, 0, f)),
            pl.BlockSpec((None, tf, D), lambda i, f: (layer, f, 0)),
        ],
        out_specs=pl.BlockSpec((tm, D), lambda i, f: (i, 0)),
        out_shape=jax.ShapeDtypeStruct((T, D), F32),
        scratch_shapes=[pltpu.VMEM((tm, D), BF16)],
        compiler_params=_params("parallel", "arbitrary"),
        name="mlp",
    )(x, g.reshape(1, D), w1, w2)


HALO = 16


def _conv_prologue(tm, tiles_per_seq, b_ref, c_ref, v_ref, cp_ref, vp_ref, cw_ref):
    i = pl.program_id(0)
    first = (i % tiles_per_seq) == 0
    u = c_ref[...].astype(F32) * v_ref[...].astype(F32)
    up = cp_ref[...].astype(F32) * vp_ref[...].astype(F32)
    up = jnp.where(first, 0.0, up)
    row = lax.broadcasted_iota(jnp.int32, u.shape, 0)
    r1 = jnp.where(row == 0, up[HALO - 1:HALO, :], pltpu.roll(u, 1, 0))
    r2 = pltpu.roll(u, 2, 0)
    r2 = jnp.where(row == 0, up[HALO - 2:HALO - 1, :], jnp.where(row == 1, up[HALO - 1:HALO, :], r2))
    cw = cw_ref[...]
    y = cw[0:1, :] * r2 + cw[1:2, :] * r1 + cw[2:3, :] * u
    return (b_ref[...].astype(F32) * y).astype(BF16)


def _conv_out(bcv, conv_w, w_out, x, seq, tm):
    T, D = x.shape
    hb = tm // HALO
    prev = lambda col: (lambda i: (jnp.maximum(i * hb - 1, 0), col))
    specs = [
        pl.BlockSpec((tm, D), lambda i: (i, 0)),
        pl.BlockSpec((tm, D), lambda i: (i, 1)),
        pl.BlockSpec((tm, D), lambda i: (i, 2)),
        pl.BlockSpec((HALO, D), prev(1)),
        pl.BlockSpec((HALO, D), prev(2)),
        pl.BlockSpec((CONV_WIDTH, D), lambda i: (0, 0)),
    ]
    prologue = functools.partial(_conv_prologue, tm, seq // tm)
    return _res_proj([bcv, bcv, bcv, bcv, bcv, conv_w], specs, prologue, w_out, x, tm, "conv_out")


def _pool_body(tm, tiles_per_seq, x_ref, xp_ref, g_ref, w_ref, sc_ref, o_ref):
    i = pl.program_id(0)
    first = (i % tiles_per_seq) == 0
    xv = x_ref[...]
    g = g_ref[...]
    h = _rms(xv, g)
    hp = jnp.where(first, 0.0, _rms(xp_ref[...], g))
    pos = (i % tiles_per_seq) * tm + lax.broadcasted_iota(jnp.int32, (tm, 1), 0)
    cg = h.shape[1] // len(POOL_WINDOWS)
    for gi, win in enumerate(POOL_WINDOWS):
        sl = slice(gi * cg, (gi + 1) * cg)
        hg = h[:, sl]
        s = jnp.concatenate([hp[:, sl], hg], axis=0)
        k = 1
        while k < win:
            s = s + pltpu.roll(s, k, 0)
            k *= 2
        cnt = jnp.minimum(pos + 1, win).astype(F32)
        pooled = s[HALO:, :] / cnt - hg
        y = _dot(pooled.astype(BF16), w_ref[gi])
        o_ref[:, sl] = xv[:, sl] + y * sc_ref[:, sl]


def _pool_mixer(x, g, w, scale, seq, tm):
    T, D = x.shape
    ng, cg, _ = w.shape
    hb = tm // HALO
    return pl.pallas_call(
        functools.partial(_pool_body, tm, seq // tm),
        grid=(T // tm,),
        in_specs=[
            pl.BlockSpec((tm, D), lambda i: (i, 0)),
            pl.BlockSpec((HALO, D), lambda i: (jnp.maximum(i * hb - 1, 0), 0)),
            pl.BlockSpec((1, D), lambda i: (0, 0)),
            pl.BlockSpec((ng, cg, cg), lambda i: (0, 0, 0)),
            pl.BlockSpec((1, D), lambda i: (0, 0)),
        ],
        out_specs=pl.BlockSpec((tm, D), lambda i: (i, 0)),
        out_shape=jax.ShapeDtypeStruct((T, D), F32),
        compiler_params=_params("parallel"),
        name="pool_mixer",
    )(x, x, g.reshape(1, D), w, scale.reshape(1, D))


LOG2E = 1.4426950408889634
SB_STOP_LOG2 = -135.0


def _sb_body(tq, n_chain, q_ref, k_ref, v_ref, o_ref):
    nq = q_ref.shape[0] // tq
    row = lax.broadcasted_iota(jnp.int32, (tq, tq), 0)
    col = lax.broadcasted_iota(jnp.int32, (tq, tq), 1)
    before_diag = col < row
    r2 = lax.broadcasted_iota(jnp.int32, (2 * tq, 2 * tq), 0)
    c2 = lax.broadcasted_iota(jnp.int32, (2 * tq, 2 * tq), 1)
    key = jnp.where(r2 >= tq, r2 - tq, r2)
    sums = ((c2 >= tq) | (key > c2)).astype(BF16)

    def tiles(qs, js, cs, accs, diag, lives=None):
        chains = range(n_chain)
        offs = [pl.multiple_of(js[r] * tq, tq) for r in chains]
        zs = [_dot_nt(qs[r], k_ref[pl.ds(offs[r], tq), :]) for r in chains]
        log_1m = [jnp.log(1.0 + jnp.exp2(-jnp.abs(z))) * (-LOG2E) - jnp.maximum(z, 0.0) for z in zs]
        if diag:
            log_1m = [jnp.where(before_diag, x, 0.0) for x in log_1m]
        his = [x.astype(BF16) for x in log_1m]
        mids = [(x - hi.astype(F32)).astype(BF16) for x, hi in zip(log_1m, his)]
        ts = [_dot(jnp.concatenate([hi, mid], axis=1), sums) for hi, mid in zip(his, mids)]
        ws = [jnp.exp2(zs[r] + log_1m[r] + ts[r][:, :tq] + cs[r]) for r in chains]
        tile_sums = [t[:, tq:] for t in ts]
        if diag:
            ws = [jnp.where(before_diag, w, 0.0) for w in ws]
        if lives is not None:
            ws = [jnp.where(lives[r], ws[r], 0.0) for r in chains]
            tile_sums = [jnp.where(lives[r], tile_sums[r], 0.0) for r in chains]
        accs = tuple(accs[r] + _dot(ws[r].astype(BF16), v_ref[pl.ds(offs[r], tq), :]) for r in chains)
        return tuple(cs[r] + tile_sums[r] for r in chains), accs

    def q_group(gi, carry):
        base = gi * n_chain
        chains = range(n_chain)
        qs = [q_ref[pl.ds(pl.multiple_of((base + r) * tq, tq), tq), :] for r in chains]
        zero = (jnp.zeros((tq, tq), F32),) * n_chain
        cs, accs = tiles(qs, [base + r for r in chains], zero, zero, True)

        def live_max(cs, n):
            vals = [jnp.where(base + r - n >= 0, jnp.max(cs[r]), -jnp.inf) for r in chains]
            return functools.reduce(jnp.maximum, vals)

        def cond(st):
            n, _, _, cmax = st
            return (n <= base + n_chain - 1) & (cmax > SB_STOP_LOG2)

        def body(st):
            n, cs, accs, _ = st
            js = [base + r - n for r in chains]
            cs, accs = tiles(qs, [jnp.maximum(j, 0) for j in js], cs, accs, False, [j >= 0 for j in js])
            return n + 1, cs, accs, live_max(cs, n + 1)

        _, _, accs, _ = lax.while_loop(cond, body, (1, cs, accs, live_max(cs, 1)))
        for r in chains:
            o_ref[pl.ds(pl.multiple_of((base + r) * tq, tq), tq), :] = accs[r].astype(BF16)
        return carry

    lax.fori_loop(0, nq // n_chain, q_group, 0)


def _sb_attention(qkv, bsz, seq, n_heads, tq, n_chain):
    assert tq == HEAD_DIM == LANES
    assert seq % (tq * n_chain) == 0
    T = bsz * seq
    return pl.pallas_call(
        functools.partial(_sb_body, tq, n_chain),
        grid=(bsz, n_heads),
        in_specs=[
            pl.BlockSpec((seq, HEAD_DIM), lambda b, h: (b, h)),
            pl.BlockSpec((seq, HEAD_DIM), lambda b, h: (b, n_heads + h)),
            pl.BlockSpec((seq, HEAD_DIM), lambda b, h: (b, 2 * n_heads + h)),
        ],
        out_specs=pl.BlockSpec((seq, HEAD_DIM), lambda b, h: (b, h)),
        out_shape=jax.ShapeDtypeStruct((T, n_heads * HEAD_DIM), BF16),
        compiler_params=_params("parallel", "parallel"),
        name="sb_attention",
    )(qkv, qkv, qkv)


def _compress_body(n_chunk, t_ref, w1_ref, w2_ref, pos_ref, gain_ref, o_ref):
    kv = pl.program_id(0)
    a = jnp.concatenate([t_ref[pl.ds(p, n_chunk, stride=CMP_STRIDE), :].astype(BF16) for p in range(CMP_STRIDE)],
                        axis=1)
    half = a.shape[1]
    w1 = w1_ref[0]
    const = _dot(pos_ref[0], w1)[0:1, :]
    first = _dot(a, w1[:half, :])
    second = _dot(a, w1[half:, :])
    pre = first + pltpu.roll(second, n_chunk - 1, 0) + const
    hid = pre * jax.nn.sigmoid(pre)
    out = _dot(hid.astype(BF16), w2_ref[0])
    normed = _rms(out, gain_ref[...])
    o_ref[0, 0, 0] = jnp.where(kv == 0, normed, out).astype(BF16)


def _compress(t, w1, w2, pos, gain, bsz, seq, ng):
    dh = HEAD_DIM
    n_chunk = seq // CMP_STRIDE
    width = CMP_STRIDE * dh
    return pl.pallas_call(
        functools.partial(_compress_body, n_chunk),
        grid=(2, bsz, ng),
        in_specs=[
            pl.BlockSpec((seq, dh), lambda s, b, g: (b, s * ng + g)),
            pl.BlockSpec((1, 2 * width, dh), lambda s, b, g: (s, 0, 0)),
            pl.BlockSpec((1, dh, dh), lambda s, b, g: (s, 0, 0)),
            pl.BlockSpec((1, 8, 2 * width), lambda s, b, g: (s, 0, 0)),
            pl.BlockSpec((1, dh), lambda s, b, g: (0, 0)),
        ],
        out_specs=pl.BlockSpec((1, 1, 1, n_chunk, dh), lambda s, b, g: (s, b, g, 0, 0)),
        out_shape=jax.ShapeDtypeStruct((2, bsz, ng, n_chunk, dh), BF16),
        compiler_params=_params("parallel", "parallel", "parallel"),
        name="nsa_compress",
    )(t, w1, w2, pos, gain)


def _stack_heads(q_ref, hpg):
    return jnp.concatenate([q_ref[:, h * HEAD_DIM:(h + 1) * HEAD_DIM] for h in range(hpg)], axis=0)


def _gate_column(gates, col):
    lane = lax.broadcasted_iota(jnp.int32, gates.shape, 1)
    return jnp.sum(jnp.where(lane == col, gates, 0.0), axis=1, keepdims=True)


def _cmp_branch(qs, kc, vc, i, tq, hpg, n_sel, vt_scr, st_scr):
    n_chunk = kc.shape[0]
    s = _dot_nt(qs, kc)
    t1 = i * tq + lax.broadcasted_iota(jnp.int32, (tq, n_chunk), 0)
    c1 = lax.broadcasted_iota(jnp.int32, (tq, n_chunk), 1)
    valid1 = c1 * CMP_STRIDE + (CMP_LEN - 1) <= t1
    valid = jnp.concatenate([valid1] * hpg, axis=0)
    s = jnp.where(valid, s, NEG)
    m = jnp.max(s, axis=1, keepdims=True)
    e = jnp.where(valid, jnp.exp2(s - m), 0.0)
    l = jnp.sum(e, axis=1, keepdims=True)
    p = e * (1.0 / jnp.where(l > 0.0, l, 1.0))
    o = _dot(p.astype(BF16), vc)

    p_sum = p[0:tq, :]
    for h in range(1, hpg):
        p_sum = p_sum + p[h * tq:(h + 1) * tq, :]
    ci = lax.broadcasted_iota(jnp.int32, (n_chunk, LANES), 0)
    sj = lax.broadcasted_iota(jnp.int32, (n_chunk, LANES), 1)
    overlap = ((ci * CMP_STRIDE < (sj + 1) * SEL_LEN) & (ci * CMP_STRIDE + CMP_LEN > sj * SEL_LEN)
               & (ci < n_chunk - 1) & (sj < n_sel)).astype(BF16)
    imp = _split_dot(p_sum, overlap)
    t2 = i * tq + lax.broadcasted_iota(jnp.int32, (tq, LANES), 0)
    blk = lax.broadcasted_iota(jnp.int32, (tq, LANES), 1)
    cur = lax.shift_right_logical(t2, SEL_SHIFT)
    forced = (blk == 0) | (blk == cur) | (blk == cur - 1)
    blk_valid = blk * SEL_LEN <= t2
    score = jnp.where(forced, BIG, jnp.where(blk_valid, imp, -BIG))

    vt_scr[...] = score.T
    SUB = 8
    vts = [vt_scr[a * SUB:(a + 1) * SUB, :] for a in range(n_sel // SUB)]
    cnts = [jnp.zeros((SUB, tq), F32) for _ in vts]
    jrow = lax.broadcasted_iota(jnp.int32, (SUB, tq), 0)
    for b in range(n_sel):
        vb = vt_scr[b:b + 1, :]
        for a, vt in enumerate(vts):
            if b < a * SUB:
                ahead = vb >= vt
            elif b >= (a + 1) * SUB:
                ahead = vb > vt
            else:
                ahead = (vb > vt) | ((vb == vt) & (jrow > b - a * SUB))
            cnts[a] = cnts[a] + ahead.astype(F32)
    st_scr[...] = jnp.zeros_like(st_scr)
    for a, cnt in enumerate(cnts):
        st_scr[a * SUB:(a + 1) * SUB, :] = (cnt < float(min(SEL_TOPK, n_sel))).astype(F32)
    return o, st_scr[...].T.astype(BF16)


def _fill_values_and_ones(vo_scr, v_ref):
    vo_scr[:, :HEAD_DIM] = v_ref[...]
    vo_scr[:, HEAD_DIM:] = jnp.ones((vo_scr.shape[0], LANES), BF16)


def _sel_branch(qs, k_ref, vo_scr, sel, i, tq, tk, hpg, m_scr, acc_scr):
    dh = HEAD_DIM
    m_scr[...] = jnp.full_like(m_scr, NEG)
    acc_scr[...] = jnp.zeros_like(acc_scr)
    qpos = i * tq + lax.broadcasted_iota(jnp.int32, (tq, tk), 0)
    kcol = lax.broadcasted_iota(jnp.int32, (tq, tk), 1)
    eb = lax.broadcasted_iota(jnp.int32, (LANES, tk), 0)
    ek = lax.broadcasted_iota(jnp.int32, (LANES, tk), 1)
    n_tiles = ((i + 1) * tq + tk - 1) // tk

    def scores(j):
        k = k_ref[pl.ds(pl.multiple_of(j * tk, tk), tk), :]
        expand = (eb == lax.shift_right_logical(j * tk + ek, SEL_SHIFT)).astype(BF16)
        mask1 = (_dot(sel, expand) > 0.5) & (j * tk + kcol <= qpos)
        bias1 = jnp.where(mask1, 0.0, NEG).astype(BF16)
        return _dot_nt(qs, k).astype(BF16) + jnp.concatenate([bias1] * hpg, axis=0)

    def step(n, s):
        j = n_tiles - 1 - n
        s_next = scores(jnp.maximum(j - 1, 0))
        vo = vo_scr[pl.ds(pl.multiple_of(j * tk, tk), tk), :]
        chunks = [s[:, c * LANES:(c + 1) * LANES] for c in range(tk // LANES)]
        mx = chunks[0]
        for ch in chunks[1:]:
            mx = jnp.maximum(mx, ch)
        m_old = m_scr[...]
        m_new = jnp.maximum(m_old, jnp.max(mx, axis=1, keepdims=True).astype(F32))
        alpha = jnp.exp2(m_old - m_new)
        m_b = m_new.astype(BF16)
        p = jnp.concatenate([jnp.exp2(ch - m_b) for ch in chunks], axis=1)
        pv = _dot(p, vo)
        acc_scr[:, :dh] = alpha * acc_scr[:, :dh] + pv[:, :dh]
        acc_scr[:, dh:] = alpha * acc_scr[:, dh:] + pv[:, dh:]
        m_scr[...] = m_new
        return s_next

    lax.fori_loop(0, n_tiles, step, scores(n_tiles - 1))
    return acc_scr[:, :dh] / acc_scr[:, dh:]


def _win_branch(qs, k_ref, vo_scr, i, tq, tw, hpg):
    dh = HEAD_DIM
    span = WINDOW + tw
    outs = []
    for r in range(tq // tw):
        q0 = i * tq + r * tw
        qr = jnp.concatenate([qs[h * tq + r * tw:h * tq + (r + 1) * tw, :] for h in range(hpg)], axis=0)
        off = pl.multiple_of(jnp.maximum(q0 - WINDOW, 0), tw)
        k = k_ref[pl.ds(off, span), :]
        vo = vo_scr[pl.ds(off, span), :]
        qpos = q0 + lax.broadcasted_iota(jnp.int32, (tw, span), 0)
        kpos = off + lax.broadcasted_iota(jnp.int32, (tw, span), 1)
        bias1 = jnp.where((kpos <= qpos) & (qpos - kpos < WINDOW), 0.0, NEG).astype(BF16)
        s = _dot_nt(qr, k).astype(BF16) + jnp.concatenate([bias1] * hpg, axis=0)
        p = jnp.exp2(s - jnp.max(s, axis=1, keepdims=True))
        pv = _dot(p, vo)
        outs.append(pv[:, :dh] / pv[:, dh:])
    return jnp.concatenate([o[h * tw:(h + 1) * tw, :] for h in range(hpg) for o in outs], axis=0)


def _nsa_attn_body(tq, tk, tw, hpg, n_sel, q_ref, kc_ref, vc_ref, ks_ref, vs_ref, kw_ref, vw_ref, gates_ref, o_ref,
                   vos_scr, vow_scr, vt_scr, st_scr, m_scr, acc_scr):
    g = pl.program_id(1)
    i = pl.program_id(2)

    @pl.when(i == 0)
    def _():
        _fill_values_and_ones(vos_scr, vs_ref)
        _fill_values_and_ones(vow_scr, vw_ref)

    qs = _stack_heads(q_ref, hpg)
    o_cmp, sel = _cmp_branch(qs, kc_ref[0, 0, 0], vc_ref[0, 0, 0], i, tq, hpg, n_sel, vt_scr, st_scr)
    o_win = _win_branch(qs, kw_ref, vow_scr, i, tq, tw, hpg)
    o_sel = _sel_branch(qs, ks_ref, vos_scr, sel, i, tq, tk, hpg, m_scr, acc_scr)
    gates = gates_ref[...]
    for h in range(hpg):
        rows = slice(h * tq, (h + 1) * tq)
        col = 3 * (g * hpg + h)
        o = (_gate_column(gates, col) * o_cmp[rows, :] + _gate_column(gates, col + 1) * o_sel[rows, :]
             + _gate_column(gates, col + 2) * o_win[rows, :])
        o_ref[:, h * HEAD_DIM:(h + 1) * HEAD_DIM] = o.astype(BF16)


def _nsa_attention(qkv, kvc, gates, col, bsz, seq, ng, hpg, tq, tk, tw):
    T = bsz * seq
    nq = seq // tq
    gw = hpg * HEAD_DIM
    rows = hpg * tq
    n_chunk = kvc.shape[3]
    n_sel = seq // SEL_LEN
    assert n_sel <= LANES and n_sel % 8 == 0
    assert seq >= WINDOW + tw and WINDOW % tw == 0 and tq % tw == 0
    resident = lambda t: pl.BlockSpec((seq, HEAD_DIM), lambda b, g, i: (b, col(t) + g))
    compressed = lambda s: pl.BlockSpec((1, 1, 1, n_chunk, HEAD_DIM), lambda b, g, i: (s, b, g, 0, 0))
    return pl.pallas_call(
        functools.partial(_nsa_attn_body, tq, tk, tw, hpg, n_sel),
        grid=(bsz, ng, nq),
        in_specs=[
            pl.BlockSpec((tq, gw), lambda b, g, i: (b * nq + i, g)),
            compressed(0), compressed(1),
            resident(ng + 2), resident(ng + 3), resident(ng + 4), resident(ng + 5),
            pl.BlockSpec((tq, LANES), lambda b, g, i: (b * nq + i, 0)),
        ],
        out_specs=pl.BlockSpec((tq, gw), lambda b, g, i: (b * nq + i, g)),
        out_shape=jax.ShapeDtypeStruct((T, ng * gw), BF16),
        scratch_shapes=[
            pltpu.VMEM((seq, HEAD_DIM + LANES), BF16), pltpu.VMEM((seq, HEAD_DIM + LANES), BF16),
            pltpu.VMEM((LANES, tq), F32), pltpu.VMEM((LANES, tq), F32),
            pltpu.VMEM((rows, LANES), F32), pltpu.VMEM((rows, HEAD_DIM + LANES), F32),
        ],
        compiler_params=_params("parallel", "parallel", "arbitrary"),
        name="nsa_attention",
    )(qkv, kvc, kvc, qkv, qkv, qkv, qkv, gates)


def _ident_prologue(a_ref):
    return a_ref[...]


TM_PROJ = 1024
TN_PROJ = 512
TN_WIDE = 1024
TM_OUT = 256
TM_MLP = 1024
TF_MLP = 512
TM_POOL = 256
TQ_SB = 128
SB_CHAINS = 8
TQ_NSA = 512
TK_SEL = 512
TW_NSA = 256


def _tile_gains(per_tile, tn):
    rows = [jnp.tile(g.astype(F32), tn // HEAD_DIM) if g is not None else jnp.ones((tn,), F32) for g in per_tile]
    return jnp.stack(rows).reshape(len(per_tile), 1, tn)


def _conv_layer(x, seq, norm_g, w_in, conv_w, w_out):
    n_tiles = w_in.shape[1] // TN_WIDE
    bcv = _norm_proj(x, norm_g, w_in, _tile_gains([None] * n_tiles, TN_WIDE),
                     ["plain"] * n_tiles, TM_PROJ, TN_WIDE)
    return _conv_out(bcv, conv_w, w_out, x, seq, TM_OUT)


def _nsa_layer(x, bsz, seq, norm_g, w_in, q_gain, k_gain, cmp_pos, cmp_w1, cmp_w2, w_out):
    T, D = x.shape
    dh, G = HEAD_DIM, NSA_KV_GROUPS
    H = D // dh
    hpg = H // G
    gw = hpg * dh
    assert gw == TN_PROJ and G * dh == TN_PROJ
    scale = LOG2E * dh ** -0.5
    modes = ["norm"] * G + ["plain", "plain", "norm", "plain", "norm", "plain"]
    gains = _tile_gains([q_gain * scale] * G + [None, None, k_gain[1], None, k_gain[2], None], TN_PROJ)
    n_main = len(modes) * TN_PROJ
    n_gate = w_in.shape[1] - n_main
    assert n_gate == 3 * H and n_gate <= LANES
    w_t = w_in.T
    w_gate_t = jnp.pad(w_t[n_main:], ((0, LANES - n_gate), (0, 0)))
    qkv, gates = _norm_proj(x, norm_g, w_t, gains, modes, TM_PROJ, TN_PROJ, transposed=True, w_gate_t=w_gate_t)

    kvc_in = qkv[:, H * dh:H * dh + 2 * G * dh].astype(F32)
    pos = jnp.broadcast_to(cmp_pos.reshape(2, 1, CMP_LEN * dh), (2, 8, CMP_LEN * dh)).astype(BF16)
    kvc = _compress(kvc_in, cmp_w1.astype(BF16), cmp_w2.astype(BF16), pos, k_gain[0].reshape(1, dh), bsz, seq, G)

    col = lambda t: t * (TN_PROJ // dh)
    o = _nsa_attention(qkv, kvc, gates, col, bsz, seq, G, hpg, TQ_NSA, TK_SEL, TW_NSA)
    spec = pl.BlockSpec((TM_OUT, H * dh), lambda i: (i, 0))
    return _res_proj([o], [spec], _ident_prologue, w_out, x, TM_OUT, "nsa_out")


def _sb_layer(x, bsz, seq, norm_g, w_in, q_gain, k_gain, w_out):
    T, D = x.shape
    dh = HEAD_DIM
    H = D // dh
    scale = LOG2E * dh ** -0.5
    per = (H * dh) // TN_WIDE
    modes = ["norm"] * (2 * per) + ["plain"] * per
    gains = _tile_gains([q_gain * scale] * per + [k_gain] * per + [None] * per, TN_WIDE)
    qkv = _norm_proj(x, norm_g, w_in, gains, modes, TM_PROJ, TN_WIDE)
    o = _sb_attention(qkv, bsz, seq, H, TQ_SB, SB_CHAINS)
    spec = pl.BlockSpec((TM_OUT, H * dh), lambda i: (i, 0))
    return _res_proj([o], [spec], _ident_prologue, w_out, x, TM_OUT, "sb_out")


def kernel(x, mix_norm, mlp_norm, mlp_w1, mlp_w2, conv_w_in, conv_w, conv_w_out, nsa_w_in, nsa_q_gain, nsa_k_gain, nsa_cmp_pos, nsa_cmp_w1, nsa_cmp_w2, nsa_w_out, pool_w, pool_scale, sb_w_in, sb_q_gain, sb_k_gain, sb_w_out):
    bsz, seq, d = x.shape
    depth = mix_norm.shape[0]
    n_mixers = 4
    xf = x.reshape(bsz * seq, d)
    for i in range(depth):
        kind, j = i % n_mixers, i // n_mixers
        if kind == 0:
            xf = _conv_layer(xf, seq, mix_norm[i], conv_w_in[j], conv_w[j], conv_w_out[j])
        elif kind == 1:
            xf = _nsa_layer(xf, bsz, seq, mix_norm[i], nsa_w_in[j], nsa_q_gain[j], nsa_k_gain[j],
                            nsa_cmp_pos[j], nsa_cmp_w1[j], nsa_cmp_w2[j], nsa_w_out[j])
        elif kind == 2:
            xf = _pool_mixer(xf, mix_norm[i], pool_w[j].astype(BF16), pool_scale[j], seq, TM_POOL)
        else:
            xf = _sb_layer(xf, bsz, seq, mix_norm[i], sb_w_in[j], sb_q_gain[j], sb_k_gain[j], sb_w_out[j])
        xf = _mlp(xf, mlp_norm[i], mlp_w1, mlp_w2, i, TM_MLP, TF_MLP)
    return xf.reshape(bsz, seq, d)
```

```python
import functools

import jax
import jax.numpy as jnp
from jax import lax
from jax.experimental import pallas as pl
from jax.experimental.pallas import tpu as pltpu

F32 = jnp.float32
BF16 = jnp.bfloat16

HEAD_DIM = 128
EPS = 1e-6
NEG = -1e30
BIG = 1e4
CONV_WIDTH = 3
NSA_KV_GROUPS = 4
CMP_LEN = 32
CMP_STRIDE = 16
SEL_LEN = 64
SEL_SHIFT = SEL_LEN.bit_length() - 1
assert 1 << SEL_SHIFT == SEL_LEN
SEL_TOPK = 16
WINDOW = 512
POOL_WINDOWS = (2, 4, 8, 16)
LANES = 128
VMEM_LIMIT = 56 * 1024 * 1024


def _params(*sem):
    return pltpu.CompilerParams(dimension_semantics=sem, vmem_limit_bytes=VMEM_LIMIT)


def _rms(xv, g):
    ms = jnp.mean(xv * xv, axis=-1, keepdims=True)
    return xv * lax.rsqrt(ms + EPS) * g


def _dot(a, b):
    return jnp.dot(a, b, preferred_element_type=F32)


def _dot_nt(a, b):
    return lax.dot_general(a, b, (((1,), (1,)), ((), ())), preferred_element_type=F32)


def _split_dot(a, b):
    hi = a.astype(BF16)
    r1 = a - hi.astype(F32)
    mid = r1.astype(BF16)
    lo = (r1 - mid.astype(F32)).astype(BF16)
    return _dot(hi, b) + _dot(mid, b) + _dot(lo, b)


def _norm_proj_body(modes, tn, has_gate, transposed, x_ref, g_ref, w_ref, gain_ref, *rest):
    rest = list(rest)
    h_scr = rest.pop()
    o32_ref = rest.pop() if "plain32" in modes else None
    if has_gate:
        wg_ref, o_ref, og_ref = rest
    else:
        (o_ref,) = rest
    j = pl.program_id(1)

    @pl.when(j == 0)
    def _():
        h_scr[...] = _rms(x_ref[...], g_ref[...]).astype(BF16)
        if has_gate:
            og_ref[...] = jax.nn.sigmoid(_dot_nt(h_scr[...], wg_ref[...].astype(BF16)))

    half = tn // 2
    assert half % HEAD_DIM == 0

    def product(c):
        if transposed:
            return _dot_nt(h_scr[...], w_ref[c * half:(c + 1) * half, :].astype(BF16))
        return _dot(h_scr[...], w_ref[:, c * half:(c + 1) * half].astype(BF16))

    def plain():
        for c in range(2):
            o_ref[:, c * half:(c + 1) * half] = product(c).astype(BF16)

    def head_norm():
        gain = gain_ref[0]
        for c in range(2):
            acc = product(c)
            for h in range(half // HEAD_DIM):
                sl = slice(c * half + h * HEAD_DIM, c * half + (h + 1) * HEAD_DIM)
                o_ref[:, sl] = _rms(acc[:, h * HEAD_DIM:(h + 1) * HEAD_DIM], gain[:, sl]).astype(BF16)

    def plain32():
        for c in range(2):
            acc = product(c)
            o_ref[:, c * half:(c + 1) * half] = acc.astype(BF16)
            o32_ref[:, c * half:(c + 1) * half] = acc

    epilogues = {"plain": plain, "norm": head_norm, "plain32": plain32}
    present = [m for m in epilogues if m in modes]
    if len(present) == 1:
        epilogues[present[0]]()
    else:
        for m in present:
            tiles = [t for t, mode in enumerate(modes) if mode == m]
            cond = j == tiles[0]
            for t in tiles[1:]:
                cond = cond | (j == t)
            pl.when(cond)(epilogues[m])


def _norm_proj(x, g, w, gains, modes, tm, tn, transposed=False, w_gate_t=None):
    T, D = x.shape
    N = len(modes) * tn
    assert w.shape[0 if transposed else 1] >= N and T % tm == 0
    has_gate = w_gate_t is not None
    in_specs = [
        pl.BlockSpec((tm, D), lambda i, j: (i, 0)),
        pl.BlockSpec((1, D), lambda i, j: (0, 0)),
        pl.BlockSpec((tn, D), lambda i, j: (j, 0)) if transposed else pl.BlockSpec((D, tn), lambda i, j: (0, j)),
        pl.BlockSpec((1, 1, tn), lambda i, j: (j, 0, 0)),
    ]
    args = [x, g.reshape(1, D), w, gains]
    out_shape = [jax.ShapeDtypeStruct((T, N), BF16)]
    out_specs = [pl.BlockSpec((tm, tn), lambda i, j: (i, j))]
    if has_gate:
        in_specs.append(pl.BlockSpec((LANES, D), lambda i, j: (0, 0)))
        args.append(w_gate_t)
        out_shape.append(jax.ShapeDtypeStruct((T, LANES), F32))
        out_specs.append(pl.BlockSpec((tm, LANES), lambda i, j: (i, 0)))
    t32 = [t for t, m in enumerate(modes) if m == "plain32"]
    if t32:
        assert t32 == list(range(t32[0], t32[0] + len(t32)))
        out_shape.append(jax.ShapeDtypeStruct((T, len(t32) * tn), F32))
        out_specs.append(pl.BlockSpec((tm, tn), lambda i, j: (i, jnp.clip(j - t32[0], 0, len(t32) - 1))))
    res = pl.pallas_call(
        functools.partial(_norm_proj_body, tuple(modes), tn, has_gate, transposed),
        grid=(T // tm, N // tn),
        in_specs=in_specs,
        out_specs=out_specs,
        out_shape=out_shape,
        scratch_shapes=[pltpu.VMEM((tm, D), BF16)],
        compiler_params=_params("parallel", "arbitrary"),
        name="norm_proj",
    )(*args)
    return res if len(res) > 1 else res[0]


def _res_proj_body(prologue, n_rows, *refs):
    row_refs = refs[:n_rows]
    w_ref, x_ref, o_ref, wb_scr = refs[n_rows:n_rows + 4]

    @pl.when(pl.program_id(0) == 0)
    def _():
        wb_scr[...] = w_ref[...].astype(BF16)

    a = prologue(*row_refs)
    o_ref[...] = x_ref[...] + _dot(a, wb_scr[...])


def _res_proj(rows, row_specs, prologue, w, x, tm, name):
    T, D = x.shape
    K = w.shape[0]
    return pl.pallas_call(
        functools.partial(_res_proj_body, prologue, len(rows)),
        grid=(T // tm,),
        in_specs=list(row_specs) + [
            pl.BlockSpec((K, D), lambda i: (0, 0), pipeline_mode=pl.Buffered(1)),
            pl.BlockSpec((tm, D), lambda i: (i, 0)),
        ],
        out_specs=pl.BlockSpec((tm, D), lambda i: (i, 0)),
        out_shape=jax.ShapeDtypeStruct((T, D), F32),
        scratch_shapes=[pltpu.VMEM((K, D), BF16)],
        compiler_params=_params("arbitrary"),
        name=name,
    )(*rows, w, x)


def _mlp_body(x_ref, g_ref, w1_ref, w2_ref, o_ref, h_scr):
    f = pl.program_id(1)

    @pl.when(f == 0)
    def _():
        xv = x_ref[...]
        h_scr[...] = _rms(xv, g_ref[...]).astype(BF16)
        o_ref[...] = xv

    a = jnp.maximum(_dot(h_scr[...], w1_ref[...].astype(BF16)), 0.0)
    o_ref[...] += _dot((a * a).astype(BF16), w2_ref[...].astype(BF16))


def _mlp(x, g, w1, w2, layer, tm, tf):
    T, D = x.shape
    FF = w1.shape[2]
    return pl.pallas_call(
        _mlp_body,
        grid=(T // tm, FF // tf),
        in_specs=[
            pl.BlockSpec((tm, D), lambda i, f: (i, 0)),
            pl.BlockSpec((1, D), lambda i, f: (0, 0)),
            pl.BlockSpec((None, D, tf), lambda i, f: (layer, 0, f)),
            pl.BlockSpec((None, tf, D), lambda i, f: (layer, f, 0)),
        ],
        out_specs=pl.BlockSpec((tm, D), lambda i, f: (i, 0)),
        out_shape=jax.ShapeDtypeStruct((T, D), F32),
        scratch_shapes=[pltpu.VMEM((tm, D), BF16)],
        compiler_params=_params("parallel", "arbitrary"),
        name="mlp",
    )(x, g.reshape(1, D), w1, w2)


HALO = 16


def _conv_prologue(tm, tiles_per_seq, b_ref, c_ref, v_ref, cp_ref, vp_ref, cw_ref):
    i = pl.program_id(0)
    first = (i % tiles_per_seq) == 0
    u = c_ref[...].astype(F32) * v_ref[...].astype(F32)
    up = cp_ref[...].astype(F32) * vp_ref[...].astype(F32)
    up = jnp.where(first, 0.0, up)
    row = lax.broadcasted_iota(jnp.int32, u.shape, 0)
    r1 = jnp.where(row == 0, up[HALO - 1:HALO, :], pltpu.roll(u, 1, 0))
    r2 = pltpu.roll(u, 2, 0)
    r2 = jnp.where(row == 0, up[HALO - 2:HALO - 1, :], jnp.where(row == 1, up[HALO - 1:HALO, :], r2))
    cw = cw_ref[...]
    y = cw[0:1, :] * r2 + cw[1:2, :] * r1 + cw[2:3, :] * u
    return (b_ref[...].astype(F32) * y).astype(BF16)


def _conv_out(bcv, conv_w, w_out, x, seq, tm):
    T, D = x.shape
    hb = tm // HALO
    prev = lambda col: (lambda i: (jnp.maximum(i * hb - 1, 0), col))
    specs = [
        pl.BlockSpec((tm, D), lambda i: (i, 0)),
        pl.BlockSpec((tm, D), lambda i: (i, 1)),
        pl.BlockSpec((tm, D), lambda i: (i, 2)),
        pl.BlockSpec((HALO, D), prev(1)),
        pl.BlockSpec((HALO, D), prev(2)),
        pl.BlockSpec((CONV_WIDTH, D), lambda i: (0, 0)),
    ]
    prologue = functools.partial(_conv_prologue, tm, seq // tm)
    return _res_proj([bcv, bcv, bcv, bcv, bcv, conv_w], specs, prologue, w_out, x, tm, "conv_out")


def _pool_body(tm, tiles_per_seq, x_ref, xp_ref, g_ref, w_ref, sc_ref, o_ref):
    i = pl.program_id(0)
    first = (i % tiles_per_seq) == 0
    xv = x_ref[...]
    g = g_ref[...]
    h = _rms(xv, g)
    hp = jnp.where(first, 0.0, _rms(xp_ref[...], g))
    pos = (i % tiles_per_seq) * tm + lax.broadcasted_iota(jnp.int32, (tm, 1), 0)
    cg = h.shape[1] // len(POOL_WINDOWS)
    for gi, win in enumerate(POOL_WINDOWS):
        sl = slice(gi * cg, (gi + 1) * cg)
        hg = h[:, sl]
        s = jnp.concatenate([hp[:, sl], hg], axis=0)
        k = 1
        while k < win:
            s = s + pltpu.roll(s, k, 0)
            k *= 2
        cnt = jnp.minimum(pos + 1, win).astype(F32)
        pooled = s[HALO:, :] / cnt - hg
        y = _dot(pooled.astype(BF16), w_ref[gi])
        o_ref[:, sl] = xv[:, sl] + y * sc_ref[:, sl]


def _pool_mixer(x, g, w, scale, seq, tm):
    T, D = x.shape
    ng, cg, _ = w.shape
    hb = tm // HALO
    return pl.pallas_call(
        functools.partial(_pool_body, tm, seq // tm),
        grid=(T // tm,),
        in_specs=[
            pl.BlockSpec((tm, D), lambda i: (i, 0)),
            pl.BlockSpec((HALO, D), lambda i: (jnp.maximum(i * hb - 1, 0), 0)),
            pl.BlockSpec((1, D), lambda i: (0, 0)),
            pl.BlockSpec((ng, cg, cg), lambda i: (0, 0, 0)),
            pl.BlockSpec((1, D), lambda i: (0, 0)),
        ],
        out_specs=pl.BlockSpec((tm, D), lambda i: (i, 0)),
        out_shape=jax.ShapeDtypeStruct((T, D), F32),
        compiler_params=_params("parallel"),
        name="pool_mixer",
    )(x, x, g.reshape(1, D), w, scale.reshape(1, D))


LOG2E = 1.4426950408889634
SB_STOP_LOG2 = -135.0


def _sb_body(tq, n_chain, q_ref, k_ref, v_ref, o_ref):
    nq = q_ref.shape[0] // tq
    row = lax.broadcasted_iota(jnp.int32, (tq, tq), 0)
    col = lax.broadcasted_iota(jnp.int32, (tq, tq), 1)
    before_diag = col < row
    r2 = lax.broadcasted_iota(jnp.int32, (2 * tq, 2 * tq), 0)
    c2 = lax.broadcasted_iota(jnp.int32, (2 * tq, 2 * tq), 1)
    key = jnp.where(r2 >= tq, r2 - tq, r2)
    sums = ((c2 >= tq) | (key > c2)).astype(BF16)

    def tiles(qs, js, cs, accs, diag, lives=None):
        chains = range(n_chain)
        offs = [pl.multiple_of(js[r] * tq, tq) for r in chains]
        zs = [_dot_nt(qs[r], k_ref[pl.ds(offs[r], tq), :]) for r in chains]
        log_1m = [jnp.log(1.0 + jnp.exp2(-jnp.abs(z))) * (-LOG2E) - jnp.maximum(z, 0.0) for z in zs]
        if diag:
            log_1m = [jnp.where(before_diag, x, 0.0) for x in log_1m]
        his = [x.astype(BF16) for x in log_1m]
        mids = [(x - hi.astype(F32)).astype(BF16) for x, hi in zip(log_1m, his)]
        ts = [_dot(jnp.concatenate([hi, mid], axis=1), sums) for hi, mid in zip(his, mids)]
        ws = [jnp.exp2(zs[r] + log_1m[r] + ts[r][:, :tq] + cs[r]) for r in chains]
        tile_sums = [t[:, tq:] for t in ts]
        if diag:
            ws = [jnp.where(before_diag, w, 0.0) for w in ws]
        if lives is not None:
            ws = [jnp.where(lives[r], ws[r], 0.0) for r in chains]
            tile_sums = [jnp.where(lives[r], tile_sums[r], 0.0) for r in chains]
        accs = tuple(accs[r] + _dot(ws[r].astype(BF16), v_ref[pl.ds(offs[r], tq), :]) for r in chains)
        return tuple(cs[r] + tile_sums[r] for r in chains), accs

    def q_group(gi, carry):
        base = gi * n_chain
        chains = range(n_chain)
        qs = [q_ref[pl.ds(pl.multiple_of((base + r) * tq, tq), tq), :] for r in chains]
        zero = (jnp.zeros((tq, tq), F32),) * n_chain
        cs, accs = tiles(qs, [base + r for r in chains], zero, zero, True)

        def live_max(cs, n):
            vals = [jnp.where(base + r - n >= 0, jnp.max(cs[r]), -jnp.inf) for r in chains]
            return functools.reduce(jnp.maximum, vals)

        def cond(st):
            n, _, _, cmax = st
            return (n <= base + n_chain - 1) & (cmax > SB_STOP_LOG2)

        def body(st):
            n, cs, accs, _ = st
            js = [base + r - n for r in chains]
            cs, accs = tiles(qs, [jnp.maximum(j, 0) for j in js], cs, accs, False, [j >= 0 for j in js])
            return n + 1, cs, accs, live_max(cs, n + 1)

        _, _, accs, _ = lax.while_loop(cond, body, (1, cs, accs, live_max(cs, 1)))
        for r in chains:
            o_ref[pl.ds(pl.multiple_of((base + r) * tq, tq), tq), :] = accs[r].astype(BF16)
        return carry

    lax.fori_loop(0, nq // n_chain, q_group, 0)


def _sb_attention(qkv, bsz, seq, n_heads, tq, n_chain):
    assert tq == HEAD_DIM == LANES
    assert seq % (tq * n_chain) == 0
    T = bsz * seq
    return pl.pallas_call(
        functools.partial(_sb_body, tq, n_chain),
        grid=(bsz, n_heads),
        in_specs=[
            pl.BlockSpec((seq, HEAD_DIM), lambda b, h: (b, h)),
            pl.BlockSpec((seq, HEAD_DIM), lambda b, h: (b, n_heads + h)),
            pl.BlockSpec((seq, HEAD_DIM), lambda b, h: (b, 2 * n_heads + h)),
        ],
        out_specs=pl.BlockSpec((seq, HEAD_DIM), lambda b, h: (b, h)),
        out_shape=jax.ShapeDtypeStruct((T, n_heads * HEAD_DIM), BF16),
        compiler_params=_params("parallel", "parallel"),
        name="sb_attention",
    )(qkv, qkv, qkv)


def _compress_body(n_chunk, t_ref, w1_ref, w2_ref, pos_ref, gain_ref, o_ref):
    kv = pl.program_id(0)
    a = jnp.concatenate([t_ref[pl.ds(p, n_chunk, stride=CMP_STRIDE), :].astype(BF16) for p in range(CMP_STRIDE)],
                        axis=1)
    half = a.shape[1]
    w1 = w1_ref[0]
    const = _dot(pos_ref[0], w1)[0:1, :]
    first = _dot(a, w1[:half, :])
    second = _dot(a, w1[half:, :])
    pre = first + pltpu.roll(second, n_chunk - 1, 0) + const
    hid = pre * jax.nn.sigmoid(pre)
    out = _dot(hid.astype(BF16), w2_ref[0])
    normed = _rms(out, gain_ref[...])
    o_ref[0, 0, 0] = jnp.where(kv == 0, normed, out).astype(BF16)


def _compress(t, w1, w2, pos, gain, bsz, seq, ng):
    dh = HEAD_DIM
    n_chunk = seq // CMP_STRIDE
    width = CMP_STRIDE * dh
    return pl.pallas_call(
        functools.partial(_compress_body, n_chunk),
        grid=(2, bsz, ng),
        in_specs=[
            pl.BlockSpec((seq, dh), lambda s, b, g: (b, s * ng + g)),
            pl.BlockSpec((1, 2 * width, dh), lambda s, b, g: (s, 0, 0)),
            pl.BlockSpec((1, dh, dh), lambda s, b, g: (s, 0, 0)),
            pl.BlockSpec((1, 8, 2 * width), lambda s, b, g: (s, 0, 0)),
            pl.BlockSpec((1, dh), lambda s, b, g: (0, 0)),
        ],
        out_specs=pl.BlockSpec((1, 1, 1, n_chunk, dh), lambda s, b, g: (s, b, g, 0, 0)),
        out_shape=jax.ShapeDtypeStruct((2, bsz, ng, n_chunk, dh), BF16),
        compiler_params=_params("parallel", "parallel", "parallel"),
        name="nsa_compress",
    )(t, w1, w2, pos, gain)


def _stack_heads(q_ref, hpg):
    return jnp.concatenate([q_ref[:, h * HEAD_DIM:(h + 1) * HEAD_DIM] for h in range(hpg)], axis=0)


def _gate_column(gates, col):
    lane = lax.broadcasted_iota(jnp.int32, gates.shape, 1)
    return jnp.sum(jnp.where(lane == col, gates, 0.0), axis=1, keepdims=True)


def _cmp_branch(qs, kc, vc, i, tq, hpg, n_sel, vt_scr, st_scr):
    n_chunk = kc.shape[0]
    s = _dot_nt(qs, kc)
    t1 = i * tq + lax.broadcasted_iota(jnp.int32, (tq, n_chunk), 0)
    c1 = lax.broadcasted_iota(jnp.int32, (tq, n_chunk), 1)
    valid1 = c1 * CMP_STRIDE + (CMP_LEN - 1) <= t1
    valid = jnp.concatenate([valid1] * hpg, axis=0)
    s = jnp.where(valid, s, NEG)
    m = jnp.max(s, axis=1, keepdims=True)
    e = jnp.where(valid, jnp.exp2(s - m), 0.0)
    l = jnp.sum(e, axis=1, keepdims=True)
    p = e * (1.0 / jnp.where(l > 0.0, l, 1.0))
    o = _dot(p.astype(BF16), vc)

    p_sum = p[0:tq, :]
    for h in range(1, hpg):
        p_sum = p_sum + p[h * tq:(h + 1) * tq, :]
    ci = lax.broadcasted_iota(jnp.int32, (n_chunk, LANES), 0)
    sj = lax.broadcasted_iota(jnp.int32, (n_chunk, LANES), 1)
    overlap = ((ci * CMP_STRIDE < (sj + 1) * SEL_LEN) & (ci * CMP_STRIDE + CMP_LEN > sj * SEL_LEN)
               & (ci < n_chunk - 1) & (sj < n_sel)).astype(BF16)
    imp = _split_dot(p_sum, overlap)
    t2 = i * tq + lax.broadcasted_iota(jnp.int32, (tq, LANES), 0)
    blk = lax.broadcasted_iota(jnp.int32, (tq, LANES), 1)
    cur = lax.shift_right_logical(t2, SEL_SHIFT)
    forced = (blk == 0) | (blk == cur) | (blk == cur - 1)
    blk_valid = blk * SEL_LEN <= t2
    score = jnp.where(forced, BIG, jnp.where(blk_valid, imp, -BIG))

    vt_scr[...] = score.T
    SUB = 8
    vts = [vt_scr[a * SUB:(a + 1) * SUB, :] for a in range(n_sel // SUB)]
    cnts = [jnp.zeros((SUB, tq), F32) for _ in vts]
    jrow = lax.broadcasted_iota(jnp.int32, (SUB, tq), 0)
    for b in range(n_sel):
        vb = vt_scr[b:b + 1, :]
        for a, vt in enumerate(vts):
            if b < a * SUB:
                ahead = vb >= vt
            elif b >= (a + 1) * SUB:
                ahead = vb > vt
            else:
                ahead = (vb > vt) | ((vb == vt) & (jrow > b - a * SUB))
            cnts[a] = cnts[a] + ahead.astype(F32)
    st_scr[...] = jnp.zeros_like(st_scr)
    for a, cnt in enumerate(cnts):
        st_scr[a * SUB:(a + 1) * SUB, :] = (cnt < float(min(SEL_TOPK, n_sel))).astype(F32)
    return o, st_scr[...].T.astype(BF16)


def _fill_values_and_ones(vo_scr, v_ref):
    vo_scr[:, :HEAD_DIM] = v_ref[...]
    vo_scr[:, HEAD_DIM:] = jnp.ones((vo_scr.shape[0], LANES), BF16)


def _sel_branch(qs, k_ref, vo_scr, sel, i, tq, tk, hpg, m_scr, acc_scr):
    dh = HEAD_DIM
    m_scr[...] = jnp.full_like(m_scr, NEG)
    acc_scr[...] = jnp.zeros_like(acc_scr)
    qpos = i * tq + lax.broadcasted_iota(jnp.int32, (tq, tk), 0)
    kcol = lax.broadcasted_iota(jnp.int32, (tq, tk), 1)
    eb = lax.broadcasted_iota(jnp.int32, (LANES, tk), 0)
    ek = lax.broadcasted_iota(jnp.int32, (LANES, tk), 1)
    n_tiles = ((i + 1) * tq + tk - 1) // tk

    def scores(j):
        k = k_ref[pl.ds(pl.multiple_of(j * tk, tk), tk), :]
        expand = (eb == lax.shift_right_logical(j * tk + ek, SEL_SHIFT)).astype(BF16)
        mask1 = (_dot(sel, expand) > 0.5) & (j * tk + kcol <= qpos)
        bias1 = jnp.where(mask1, 0.0, NEG).astype(BF16)
        return _dot_nt(qs, k).astype(BF16) + jnp.concatenate([bias1] * hpg, axis=0)

    def step(n, s):
        j = n_tiles - 1 - n
        s_next = scores(jnp.maximum(j - 1, 0))
        vo = vo_scr[pl.ds(pl.multiple_of(j * tk, tk), tk), :]
        chunks = [s[:, c * LANES:(c + 1) * LANES] for c in range(tk // LANES)]
        mx = chunks[0]
        for ch in chunks[1:]:
            mx = jnp.maximum(mx, ch)
        m_old = m_scr[...]
        m_new = jnp.maximum(m_old, jnp.max(mx, axis=1, keepdims=True).astype(F32))
        alpha = jnp.exp2(m_old - m_new)
        m_b = m_new.astype(BF16)
        p = jnp.concatenate([jnp.exp2(ch - m_b) for ch in chunks], axis=1)
        pv = _dot(p, vo)
        acc_scr[:, :dh] = alpha * acc_scr[:, :dh] + pv[:, :dh]
        acc_scr[:, dh:] = alpha * acc_scr[:, dh:] + pv[:, dh:]
        m_scr[...] = m_new
        return s_next

    lax.fori_loop(0, n_tiles, step, scores(n_tiles - 1))
    return acc_scr[:, :dh] / acc_scr[:, dh:]


def _win_branch(qs, k_ref, vo_scr, i, tq, tw, hpg):
    dh = HEAD_DIM
    span = WINDOW + tw
    outs = []
    for r in range(tq // tw):
        q0 = i * tq + r * tw
        qr = jnp.concatenate([qs[h * tq + r * tw:h * tq + (r + 1) * tw, :] for h in range(hpg)], axis=0)
        off = pl.multiple_of(jnp.maximum(q0 - WINDOW, 0), tw)
        k = k_ref[pl.ds(off, span), :]
        vo = vo_scr[pl.ds(off, span), :]
        qpos = q0 + lax.broadcasted_iota(jnp.int32, (tw, span), 0)
        kpos = off + lax.broadcasted_iota(jnp.int32, (tw, span), 1)
        bias1 = jnp.where((kpos <= qpos) & (qpos - kpos < WINDOW), 0.0, NEG).astype(BF16)
        s = _dot_nt(qr, k).astype(BF16) + jnp.concatenate([bias1] * hpg, axis=0)
        p = jnp.exp2(s - jnp.max(s, axis=1, keepdims=True))
        pv = _dot(p, vo)
        outs.append(pv[:, :dh] / pv[:, dh:])
    return jnp.concatenate([o[h * tw:(h + 1) * tw, :] for h in range(hpg) for o in outs], axis=0)


def _nsa_attn_body(tq, tk, tw, hpg, n_sel, q_ref, kc_ref, vc_ref, ks_ref, vs_ref, kw_ref, vw_ref, gates_ref, o_ref,
                   vos_scr, vow_scr, vt_scr, st_scr, m_scr, acc_scr):
    g = pl.program_id(1)
    i = pl.program_id(2)

    @pl.when(i == 0)
    def _():
        _fill_values_and_ones(vos_scr, vs_ref)
        _fill_values_and_ones(vow_scr, vw_ref)

    qs = _stack_heads(q_ref, hpg)
    o_cmp, sel = _cmp_branch(qs, kc_ref[0, 0, 0], vc_ref[0, 0, 0], i, tq, hpg, n_sel, vt_scr, st_scr)
    o_win = _win_branch(qs, kw_ref, vow_scr, i, tq, tw, hpg)
    o_sel = _sel_branch(qs, ks_ref, vos_scr, sel, i, tq, tk, hpg, m_scr, acc_scr)
    gates = gates_ref[...]
    for h in range(hpg):
        rows = slice(h * tq, (h + 1) * tq)
        col = 3 * (g * hpg + h)
        o = (_gate_column(gates, col) * o_cmp[rows, :] + _gate_column(gates, col + 1) * o_sel[rows, :]
             + _gate_column(gates, col + 2) * o_win[rows, :])
        o_ref[:, h * HEAD_DIM:(h + 1) * HEAD_DIM] = o.astype(BF16)


def _nsa_attention(qkv, kvc, gates, col, bsz, seq, ng, hpg, tq, tk, tw):
    T = bsz * seq
    nq = seq // tq
    gw = hpg * HEAD_DIM
    rows = hpg * tq
    n_chunk = kvc.shape[3]
    n_sel = seq // SEL_LEN
    assert n_sel <= LANES and n_sel % 8 == 0
    assert seq >= WINDOW + tw and WINDOW % tw == 0 and tq % tw == 0
    resident = lambda t: pl.BlockSpec((seq, HEAD_DIM), lambda b, g, i: (b, col(t) + g))
    compressed = lambda s: pl.BlockSpec((1, 1, 1, n_chunk, HEAD_DIM), lambda b, g, i: (s, b, g, 0, 0))
    return pl.pallas_call(
        functools.partial(_nsa_attn_body, tq, tk, tw, hpg, n_sel),
        grid=(bsz, ng, nq),
        in_specs=[
            pl.BlockSpec((tq, gw), lambda b, g, i: (b * nq + i, g)),
            compressed(0), compressed(1),
            resident(ng + 2), resident(ng + 3), resident(ng + 4), resident(ng + 5),
            pl.BlockSpec((tq, LANES), lambda b, g, i: (b * nq + i, 0)),
        ],
        out_specs=pl.BlockSpec((tq, gw), lambda b, g, i: (b * nq + i, g)),
        out_shape=jax.ShapeDtypeStruct((T, ng * gw), BF16),
        scratch_shapes=[
            pltpu.VMEM((seq, HEAD_DIM + LANES), BF16), pltpu.VMEM((seq, HEAD_DIM + LANES), BF16),
            pltpu.VMEM((LANES, tq), F32), pltpu.VMEM((LANES, tq), F32),
            pltpu.VMEM((rows, LANES), F32), pltpu.VMEM((rows, HEAD_DIM + LANES), F32),
        ],
        compiler_params=_params("parallel", "parallel", "arbitrary"),
        name="nsa_attention",
    )(qkv, kvc, kvc, qkv, qkv, qkv, qkv, gates)


def _ident_prologue(a_ref):
    return a_ref[...]


TM_PROJ = 1024
TN_PROJ = 512
TN_WIDE = 1024
TM_OUT = 256
TM_OUT_ATTN = 512
TM_MLP = 1024
TF_MLP = 512
TM_POOL = 512
TQ_SB = 128
SB_CHAINS = 8
TQ_NSA = 512
TK_SEL = 512
TW_NSA = 256


def _tile_gains(per_tile, tn):
    rows = [jnp.tile(g.astype(F32), tn // HEAD_DIM) if g is not None else jnp.ones((tn,), F32) for g in per_tile]
    return jnp.stack(rows).reshape(len(per_tile), 1, tn)


def _conv_layer(x, seq, norm_g, w_in, conv_w, w_out):
    n_tiles = w_in.shape[1] // TN_WIDE
    bcv = _norm_proj(x, norm_g, w_in, _tile_gains([None] * n_tiles, TN_WIDE),
                     ["plain"] * n_tiles, TM_PROJ, TN_WIDE)
    return _conv_out(bcv, conv_w, w_out, x, seq, TM_OUT)


def _nsa_layer(x, bsz, seq, norm_g, w_in, q_gain, k_gain, cmp_pos, cmp_w1, cmp_w2, w_out):
    T, D = x.shape
    dh, G = HEAD_DIM, NSA_KV_GROUPS
    H = D // dh
    hpg = H // G
    gw = hpg * dh
    assert gw == TN_PROJ and G * dh == TN_PROJ
    scale = LOG2E * dh ** -0.5
    modes = ["norm"] * G + ["plain32", "plain32", "norm", "plain", "norm", "plain"]
    gains = _tile_gains([q_gain * scale] * G + [None, None, k_gain[1], None, k_gain[2], None], TN_PROJ)
    n_main = len(modes) * TN_PROJ
    n_gate = w_in.shape[1] - n_main
    assert n_gate == 3 * H and n_gate <= LANES
    w_t = w_in.T
    w_gate_t = jnp.pad(w_t[n_main:], ((0, LANES - n_gate), (0, 0)))
    qkv, gates, kvc_in = _norm_proj(x, norm_g, w_t, gains, modes, TM_PROJ, TN_PROJ, transposed=True,
                                    w_gate_t=w_gate_t)
    pos = jnp.broadcast_to(cmp_pos.reshape(2, 1, CMP_LEN * dh), (2, 8, CMP_LEN * dh)).astype(BF16)
    kvc = _compress(kvc_in, cmp_w1.astype(BF16), cmp_w2.astype(BF16), pos, k_gain[0].reshape(1, dh), bsz, seq, G)

    col = lambda t: t * (TN_PROJ // dh)
    o = _nsa_attention(qkv, kvc, gates, col, bsz, seq, G, hpg, TQ_NSA, TK_SEL, TW_NSA)
    spec = pl.BlockSpec((TM_OUT_ATTN, H * dh), lambda i: (i, 0))
    return _res_proj([o], [spec], _ident_prologue, w_out, x, TM_OUT_ATTN, "nsa_out")


def _sb_layer(x, bsz, seq, norm_g, w_in, q_gain, k_gain, w_out):
    T, D = x.shape
    dh = HEAD_DIM
    H = D // dh
    scale = LOG2E * dh ** -0.5
    per = (H * dh) // TN_WIDE
    modes = ["norm"] * (2 * per) + ["plain"] * per
    gains = _tile_gains([q_gain * scale] * per + [k_gain] * per + [None] * per, TN_WIDE)
    qkv = _norm_proj(x, norm_g, w_in, gains, modes, TM_PROJ, TN_WIDE)
    o = _sb_attention(qkv, bsz, seq, H, TQ_SB, SB_CHAINS)
    spec = pl.BlockSpec((TM_OUT_ATTN, H * dh), lambda i: (i, 0))
    return _res_proj([o], [spec], _ident_prologue, w_out, x, TM_OUT_ATTN, "sb_out")


def kernel(x, mix_norm, mlp_norm, mlp_w1, mlp_w2, conv_w_in, conv_w, conv_w_out, nsa_w_in, nsa_q_gain, nsa_k_gain, nsa_cmp_pos, nsa_cmp_w1, nsa_cmp_w2, nsa_w_out, pool_w, pool_scale, sb_w_in, sb_q_gain, sb_k_gain, sb_w_out):
    bsz, seq, d = x.shape
    depth = mix_norm.shape[0]
    n_mixers = 4
    xf = x.reshape(bsz * seq, d)
    for i in range(depth):
        kind, j = i % n_mixers, i // n_mixers
        if kind == 0:
            xf = _conv_layer(xf, seq, mix_norm[i], conv_w_in[j], conv_w[j], conv_w_out[j])
        elif kind == 1:
            xf = _nsa_layer(xf, bsz, seq, mix_norm[i], nsa_w_in[j], nsa_q_gain[j], nsa_k_gain[j],
                            nsa_cmp_pos[j], nsa_cmp_w1[j], nsa_cmp_w2[j], nsa_w_out[j])
        elif kind == 2:
            xf = _pool_mixer(xf, mix_norm[i], pool_w[j].astype(BF16), pool_scale[j], seq, TM_POOL)
        else:
            xf = _sb_layer(xf, bsz, seq, mix_norm[i], sb_w_in[j], sb_q_gain[j], sb_k_gain[j], sb_w_out[j])
        xf = _mlp(xf, mlp_norm[i], mlp_w1, mlp_w2, i, TM_MLP, TF_MLP)
    return xf.reshape(bsz, seq, d)
```
